```python
import math
import jax, jax.numpy as jnp
from jax import lax
import numpy as np

D_MODEL = 1024
BATCH = 1
SEQ = 16384
DEPTH = 2

CTX_LEN = 256
GRID_W = 64
EPS = 1e-6
ROPE_THETA = 10000.0
Q_BLOCK = 128

GLA_HEADS = 4
GLA_DK = 64
GLA_DV = 128
GLA_GATE_RANK = 16
GLA_TAU = 16.0
GLA_CHUNK = 64
GLA_QK = GLA_HEADS * GLA_DK
GLA_VW = GLA_HEADS * GLA_DV

MLA_HEADS = 4
MLA_Q_LORA = 256
MLA_KV_LORA = 128
MLA_NOPE = 128
MLA_ROPE = 64
MLA_V = 128
MLA_VW = MLA_HEADS * MLA_V

MIX_WIDTH = GLA_VW + MLA_VW
AB_SIZES = (GLA_QK, GLA_QK, GLA_VW, GLA_VW, GLA_GATE_RANK, GLA_GATE_RANK, MLA_Q_LORA, MLA_KV_LORA, MLA_ROPE)
IN_AB = GLA_QK + GLA_QK + GLA_VW + GLA_VW + GLA_GATE_RANK + GLA_GATE_RANK + MLA_Q_LORA + MLA_KV_LORA + MLA_ROPE

NA_HEADS = 16
NA_DH = D_MODEL // NA_HEADS
NA_KH_MAX = 8
NA_KW = 16

N_EXPERTS = 32
TOP_K = 4
D_FF_EXPERT = D_MODEL
SWIGLU_LIMIT = 7.0
SWIGLU_ALPHA = 1.702
MOE_BLOCK = 128

N_EVEN = (DEPTH + 1) // 2
N_ODD = DEPTH // 2

kernel_name = "hybrid_gla_mla_natten_moe_dit"


def rmsnorm(x, g):
    xf = x.astype(jnp.float32)
    y = xf * lax.rsqrt(jnp.mean(xf * xf, axis=-1, keepdims=True) + EPS)
    return (y * g.astype(jnp.float32)).astype(x.dtype)


def axial_rope(n_tok, rot_dim, dtype):
    t = jnp.arange(n_tok)
    row = (t // GRID_W).astype(jnp.float32)
    col = (t % GRID_W).astype(jnp.float32)
    n_freq = rot_dim // 4
    inv = ROPE_THETA ** (-jnp.arange(n_freq, dtype=jnp.float32) / n_freq)
    ang = jnp.concatenate([row[:, None] * inv, col[:, None] * inv], axis=-1)
    return jnp.cos(ang).astype(dtype), jnp.sin(ang).astype(dtype)


def apply_rope(x, cos, sin):
    x1, x2 = jnp.split(x, 2, axis=-1)
    return jnp.concatenate([x1 * cos - x2 * sin, x1 * sin + x2 * cos], axis=-1)


def dense_attention_blocks(q, k, v, scale):
    B, S, H, dq = q.shape
    nb = S // Q_BLOCK
    qb = jnp.moveaxis(q.reshape(B, nb, Q_BLOCK, H, dq), 1, 0)

    def one(qi):
        s = jnp.einsum('bqhd,bkhd->bhqk', qi, k).astype(jnp.float32) * scale
        p = jax.nn.softmax(s, axis=-1).astype(v.dtype)
        return jnp.einsum('bhqk,bkhv->bqhv', p, v)

    o = lax.map(one, qb)
    return jnp.moveaxis(o, 0, 1).reshape(B, S, H, v.shape[-1])


def gla_scan(q, k, v, log_a, s0):
    B, T, H, DK = q.shape
    DV = v.shape[-1]
    nc = T // GLA_CHUNK

    def to_chunks(a):
        return jnp.moveaxis(a.astype(jnp.float32).reshape(B, nc, GLA_CHUNK, H, a.shape[-1]), 1, 0)

    causal = jnp.tril(jnp.ones((GLA_CHUNK, GLA_CHUNK), dtype=bool))

    def step(s, inp):
        qc, kc, vc, gc = inp
        b = jnp.cumsum(gc, axis=1)
        diff = b[:, :, None] - b[:, None, :]
        decay = jnp.exp(jnp.where(causal[None, :, :, None, None], diff, -jnp.inf))
        a = jnp.sum(qc[:, :, None] * kc[:, None] * decay, axis=-1)
        o_intra = jnp.einsum('btsh,bshv->bthv', a, vc)
        o_inter = jnp.einsum('bthd,bhdv->bthv', qc * jnp.exp(b), s)
        b_last = b[:, -1]
        k_dec = kc * jnp.exp(b_last[:, None] - b)
        s_new = jnp.exp(b_last)[..., None] * s + jnp.einsum('bshd,bshv->bhdv', k_dec, vc)
        return s_new, o_intra + o_inter

    s_fin, o = lax.scan(step, s0, (to_chunks(q), to_chunks(k), to_chunks(v), to_chunks(log_a)))
    o = jnp.moveaxis(o, 0, 1).reshape(B, T, H, DV)
    return o.astype(v.dtype), s_fin


def mixer_gla_mla(h_l, h_c, in_w, wa_f, ba_f, wa_b, ba_b, onorm, qnorm, wuq, kvnorm, wukv, out_w, cos, sin, need_ctx):
    split_pts = np.cumsum(AB_SIZES)[:-1].tolist()

    def project(h, rotary):
        B, T, _ = h.shape
        q, k, v, r, a_f, a_b, cq, ckv, kr = jnp.split(h @ in_w, split_pts, axis=-1)
        q = q.reshape(B, T, GLA_HEADS, GLA_DK) * (GLA_DK ** -0.5)
        k = k.reshape(B, T, GLA_HEADS, GLA_DK)
        v = v.reshape(B, T, GLA_HEADS, GLA_DV)
        la_f = jax.nn.log_sigmoid((a_f @ wa_f + ba_f).astype(jnp.float32)).reshape(B, T, GLA_HEADS, GLA_DK) / GLA_TAU
        la_b = jax.nn.log_sigmoid((a_b @ wa_b + ba_b).astype(jnp.float32)).reshape(B, T, GLA_HEADS, GLA_DK) / GLA_TAU
        qm = (rmsnorm(cq, qnorm) @ wuq).reshape(B, T, MLA_HEADS, MLA_NOPE + MLA_ROPE)
        kvm = (rmsnorm(ckv, kvnorm) @ wukv).reshape(B, T, MLA_HEADS, MLA_NOPE + MLA_V)
        q_nope, q_rope = jnp.split(qm, [MLA_NOPE], axis=-1)
        k_nope, vm = jnp.split(kvm, [MLA_NOPE], axis=-1)
        if rotary:
            q_rope = apply_rope(q_rope, cos[:, None, :], sin[:, None, :])
            kr = apply_rope(kr, cos, sin)
        qm = jnp.concatenate([q_nope, q_rope], axis=-1)
        km = jnp.concatenate([k_nope, jnp.broadcast_to(kr[:, :, None, :], (B, T, MLA_HEADS, MLA_ROPE))], axis=-1)
        return (q, k, v, la_f, la_b, r), (qm, km, vm)

    (q_l, k_l, v_l, fl, bl, r_l), (qm_l, km_l, vm_l) = project(h_l, True)
    (q_c, k_c, v_c, fc, bc, r_c), (qm_c, km_c, vm_c) = project(h_c, False)
    B, S, _ = h_l.shape
    flip = lambda a: a[:, ::-1]
    s0 = jnp.zeros((q_l.shape[0], GLA_HEADS, GLA_DK, GLA_DV), jnp.float32)
    o_cf, s_cf = gla_scan(q_c, k_c, v_c, fc, s0)
    o_cb, s_cb = gla_scan(flip(q_c), flip(k_c), flip(v_c), flip(bc), s0)
    o_lf, _ = gla_scan(q_l, k_l, v_l, fl, s_cf)
    o_lb, _ = gla_scan(flip(q_l), flip(k_l), flip(v_l), flip(bl), s_cb)

    def gla_out(o, r):
        return rmsnorm(o, onorm).reshape(o.shape[0], o.shape[1], GLA_VW) * jax.nn.silu(r)

    scale = (MLA_NOPE + MLA_ROPE) ** -0.5
    k_all = jnp.concatenate([km_c, km_l], axis=1)
    v_all = jnp.concatenate([vm_c, vm_l], axis=1)
    o_mla_l = dense_attention_blocks(qm_l, k_all, v_all, scale)
    y_l = jnp.concatenate([gla_out(o_lf + flip(o_lb), r_l), o_mla_l.reshape(B, S, MLA_VW)], axis=-1) @ out_w
    if not need_ctx:
        return y_l, None
    o_mla_c = dense_attention_blocks(qm_c, km_c, vm_c, scale)
    y_c = jnp.concatenate([gla_out(o_cf + flip(o_cb), r_c), o_mla_c.reshape(B, CTX_LEN, MLA_VW)], axis=-1) @ out_w
    return y_l, y_c


def neighbourhood_attention(q, k, v, k_ctx, v_ctx, rpb):
    B, S, H, dh = q.shape
    rows = S // GRID_W
    kh = min(NA_KH_MAX, rows)
    kw = NA_KW
    scale = dh ** -0.5
    qg = q.reshape(B, rows, GRID_W, H, dh)
    kg = k.reshape(B, rows, GRID_W, H, dh)
    vg = v.reshape(B, rows, GRID_W, H, dh)
    col = jnp.arange(GRID_W)
    c0 = jnp.clip(col - kw // 2, 0, GRID_W - kw)
    col_idx = c0[:, None] + jnp.arange(kw)[None, :]
    col_bias_idx = col_idx - col[:, None] + NA_KW - 1
    rpb_f = rpb.astype(jnp.float32)

    def one_row(r):
        r0 = jnp.clip(r - kh // 2, 0, rows - kh)
        q_r = lax.dynamic_index_in_dim(qg, r, axis=1, keepdims=False)
        k_rows = lax.dynamic_slice_in_dim(kg, r0, kh, axis=1)
        v_rows = lax.dynamic_slice_in_dim(vg, r0, kh, axis=1)
        k_nb = k_rows[:, :, col_idx]
        v_nb = v_rows[:, :, col_idx]
        s_nb = jnp.einsum('bqhd,bkqwhd->bhqkw', q_r, k_nb).astype(jnp.float32) * scale
        row_bias_idx = r0 + jnp.arange(kh) - r + NA_KH_MAX - 1
        bias = jnp.transpose(rpb_f[:, row_bias_idx][:, :, col_bias_idx], (0, 2, 1, 3))
        s_nb = (s_nb + bias[None]).reshape(B, H, GRID_W, kh * kw)
        s_ctx = jnp.einsum('bqhd,bkhd->bhqk', q_r, k_ctx).astype(jnp.float32) * scale
        p = jax.nn.softmax(jnp.concatenate([s_nb, s_ctx], axis=-1), axis=-1).astype(v.dtype)
        p_nb = p[..., :kh * kw].reshape(B, H, GRID_W, kh, kw)
        p_ctx = p[..., kh * kw:]
        return (jnp.einsum('bhqkw,bkqwhd->bqhd', p_nb, v_nb)
                + jnp.einsum('bhqk,bkhd->bqhd', p_ctx, v_ctx))

    o = lax.map(one_row, jnp.arange(rows))
    return jnp.moveaxis(o, 0, 1).reshape(B, S, H, dh)


def mixer_na(h_l, h_c, qkv_w, rpb, out_w, need_ctx):
    def proj(h):
        B, T, _ = h.shape
        p = (h @ qkv_w).reshape(B, T, 3, NA_HEADS, NA_DH)
        return p[:, :, 0], p[:, :, 1], p[:, :, 2]

    q_l, k_l, v_l = proj(h_l)
    q_c, k_c, v_c = proj(h_c)
    B, S, _ = h_l.shape
    y_l = neighbourhood_attention(q_l, k_l, v_l, k_c, v_c, rpb).reshape(B, S, D_MODEL) @ out_w
    if not need_ctx:
        return y_l, None
    y_c = dense_attention_blocks(q_c, k_c, v_c, NA_DH ** -0.5).reshape(B, CTX_LEN, D_MODEL) @ out_w
    return y_l, y_c


def moe_ffn(h, router_w, router_b, w1, b1, w2, b2):
    T, D = h.shape
    logits = (h @ router_w + router_b).astype(jnp.float32)
    top_val, top_idx = lax.top_k(logits, TOP_K)
    gates = jax.nn.softmax(top_val, axis=-1)
    n_assign = T * TOP_K
    flat_e = top_idx.reshape(-1)
    flat_tok = jnp.arange(n_assign, dtype=jnp.int32) // TOP_K
    order = jnp.argsort(flat_e)
    e_sorted = flat_e[order]
    counts = jnp.bincount(flat_e, length=N_EXPERTS)
    padded = ((counts + MOE_BLOCK - 1) // MOE_BLOCK) * MOE_BLOCK
    pad_end = jnp.cumsum(padded)
    pad_start = pad_end - padded
    start = jnp.cumsum(counts) - counts
    slot = pad_start[e_sorted] + jnp.arange(n_assign) - start[e_sorted]
    n_blocks = -(-n_assign // MOE_BLOCK) + N_EXPERTS
    n_slots = n_blocks * MOE_BLOCK
    slot_tok = jnp.full((n_slots,), T, jnp.int32).at[slot].set(flat_tok[order])
    slot_gate = jnp.zeros((n_slots,), jnp.float32).at[slot].set(gates.reshape(-1)[order])
    block_e = jnp.minimum(jnp.searchsorted(pad_end, jnp.arange(n_blocks) * MOE_BLOCK, side='right'), N_EXPERTS - 1)
    h_pad = jnp.concatenate([h, jnp.zeros((1, D), h.dtype)], axis=0)
    xb = h_pad[slot_tok].reshape(n_blocks, MOE_BLOCK, D)

    def expert_block(args):
        x_blk, e = args
        u = x_blk @ w1[e] + b1[e]
        glu = jnp.minimum(u[:, ::2], SWIGLU_LIMIT)
        lin = jnp.clip(u[:, 1::2], -SWIGLU_LIMIT, SWIGLU_LIMIT)
        act = glu * jax.nn.sigmoid(SWIGLU_ALPHA * glu) * (lin + 1)
        return act @ w2[e] + b2[e]

    yb = lax.map(expert_block, (xb, block_e)).reshape(n_slots, D)
    y = jnp.zeros((T + 1, D), h.dtype).at[slot_tok].add(yb * slot_gate[:, None].astype(h.dtype))
    return y[:T]


def setup_inputs(seed: int = 0) -> dict:
    key = jax.random.key(seed)
    ks = iter(jax.random.split(key, 40))
    f32 = jnp.float32

    def nrm(shape, fan_in, mult=1.0):
        return jax.random.normal(next(ks), shape, f32) * (mult * fan_in ** -0.5)

    def gain(shape):
        return 1.0 + 0.05 * jax.random.normal(next(ks), shape, f32)

    def small(shape, s=0.02):
        return s * jax.random.normal(next(ks), shape, f32)

    D = D_MODEL
    F = D_FF_EXPERT
    return {
        "x": jax.random.normal(next(ks), (BATCH, SEQ, D), f32),
        "c": jax.random.normal(next(ks), (BATCH, D), f32),
        "ctx": jax.random.normal(next(ks), (BATCH, CTX_LEN, D), f32),
        "c_ctx": jax.random.normal(next(ks), (D,), f32),
        "ada_w": nrm((DEPTH, D, 6 * D), D, 0.2),
        "ada_b": small((DEPTH, 6 * D)),
        "norm_g": gain((DEPTH, 4, D)),
        "router_w": nrm((DEPTH, D, N_EXPERTS), D),
        "router_b": small((DEPTH, N_EXPERTS), 0.01),
        "moe_w1": nrm((DEPTH, N_EXPERTS, D, 2 * F), D),
        "moe_b1": small((DEPTH, N_EXPERTS, 2 * F)),
        "moe_w2": nrm((DEPTH, N_EXPERTS, F, D), F),
        "moe_b2": small((DEPTH, N_EXPERTS, D)),
        "ab_in_w": nrm((N_EVEN, D, IN_AB), D),
        "gla_wa_f": nrm((N_EVEN, GLA_GATE_RANK, GLA_QK), GLA_GATE_RANK),
        "gla_ba_f": small((N_EVEN, GLA_QK), 0.1),
        "gla_wa_b": nrm((N_EVEN, GLA_GATE_RANK, GLA_QK), GLA_GATE_RANK),
        "gla_ba_b": small((N_EVEN, GLA_QK), 0.1),
        "gla_onorm": gain((N_EVEN, GLA_DV)),
        "mla_qnorm": gain((N_EVEN, MLA_Q_LORA)),
        "mla_wuq": nrm((N_EVEN, MLA_Q_LORA, MLA_HEADS * (MLA_NOPE + MLA_ROPE)), MLA_Q_LORA),
        "mla_kvnorm": gain((N_EVEN, MLA_KV_LORA)),
        "mla_wukv": nrm((N_EVEN, MLA_KV_LORA, MLA_HEADS * (MLA_NOPE + MLA_V)), MLA_KV_LORA),
        "ab_out_w": nrm((N_EVEN, MIX_WIDTH, D), MIX_WIDTH),
        "na_qkv_w": nrm((N_ODD, D, 3 * D), D),
        "na_rpb": small((N_ODD, NA_HEADS, 2 * NA_KH_MAX - 1, 2 * NA_KW - 1), 0.1),
        "na_out_w": nrm((N_ODD, D, D), D),
    }


def reference(x, c, ctx, c_ctx, ada_w, ada_b, norm_g, router_w, router_b, moe_w1, moe_b1, moe_w2, moe_b2,
              ab_in_w, gla_wa_f, gla_ba_f, gla_wa_b, gla_ba_b, gla_onorm, mla_qnorm, mla_wuq, mla_kvnorm, mla_wukv,
              ab_out_w, na_qkv_w, na_rpb, na_out_w):
    B, S, D = x.shape
    cos, sin = axial_rope(S, MLA_ROPE, x.dtype)
    silu_c = jax.nn.silu(c)
    silu_cc = jax.nn.silu(c_ctx)[None]
    xc = ctx
    for l in range(DEPTH):
        last = l == DEPTH - 1
        sh1_l, sc1_l, g1_l, sh2_l, sc2_l, g2_l = jnp.split((silu_c @ ada_w[l] + ada_b[l])[:, None, :], 6, axis=-1)
        sh1_c, sc1_c, g1_c, sh2_c, sc2_c, g2_c = jnp.split((silu_cc @ ada_w[l] + ada_b[l])[:, None, :], 6, axis=-1)
        h_l = rmsnorm(x, norm_g[l, 0]) * (1 + sc1_l) + sh1_l
        h_c = rmsnorm(xc, norm_g[l, 0]) * (1 + sc1_c) + sh1_c
        i = l // 2
        if l % 2 == 0:
            o_l, o_c = mixer_gla_mla(h_l, h_c, ab_in_w[i], gla_wa_f[i], gla_ba_f[i], gla_wa_b[i], gla_ba_b[i],
                                     gla_onorm[i], mla_qnorm[i], mla_wuq[i], mla_kvnorm[i], mla_wukv[i],
                                     ab_out_w[i], cos, sin, not last)
        else:
            o_l, o_c = mixer_na(h_l, h_c, na_qkv_w[i], na_rpb[i], na_out_w[i], not last)
        x = x + rmsnorm(o_l, norm_g[l, 1]) * g1_l
        h_l = rmsnorm(x, norm_g[l, 2]) * (1 + sc2_l) + sh2_l
        if not last:
            xc = xc + rmsnorm(o_c, norm_g[l, 1]) * g1_c
            h_c = rmsnorm(xc, norm_g[l, 2]) * (1 + sc2_c) + sh2_c
            tokens = jnp.concatenate([h_c.reshape(-1, D), h_l.reshape(-1, D)], axis=0)
            y = moe_ffn(tokens, router_w[l], router_b[l], moe_w1[l], moe_b1[l], moe_w2[l], moe_b2[l])
            y_c = y[:B * CTX_LEN].reshape(B, CTX_LEN, D)
            y_l = y[B * CTX_LEN:].reshape(B, S, D)
            xc = xc + rmsnorm(y_c, norm_g[l, 3]) * g2_c
        else:
            y_l = moe_ffn(h_l.reshape(-1, D), router_w[l], router_b[l], moe_w1[l], moe_b1[l], moe_w2[l],
                          moe_b2[l]).reshape(B, S, D)
        x = x + rmsnorm(y_l, norm_g[l, 3]) * g2_l
    return x
```

```python
import functools

import numpy as np
import jax
import jax.numpy as jnp
from jax import lax
from jax.experimental import pallas as pl
from jax.experimental.pallas import tpu as pltpu

F32 = jnp.float32
BF16 = jnp.bfloat16

D_MODEL = 1024
DEPTH = 2
CTX_LEN = 256
GRID_W = 64
EPS = 1e-6
ROPE_THETA = 10000.0

GLA_HEADS = 4
GLA_DK = 64
GLA_DV = 128
GLA_GATE_RANK = 16
GLA_TAU = 16.0
GLA_QK = GLA_HEADS * GLA_DK
GLA_VW = GLA_HEADS * GLA_DV

MLA_HEADS = 4
MLA_Q_LORA = 256
MLA_KV_LORA = 128
MLA_NOPE = 128
MLA_ROPE = 64
MLA_V = 128
MLA_VW = MLA_HEADS * MLA_V
MLA_QK_PAD = 256

NA_HEADS = 16
NA_DH = D_MODEL // NA_HEADS
NA_KH = 8
NA_KW = 16
NA_QROWS = 4
NA_WROWS = NA_QROWS + NA_KH

N_EXPERTS = 32
TOP_K = 4
SWIGLU_LIMIT = 7.0
SWIGLU_ALPHA = 1.702

LANES = 128
ROW_TILE = 256
MOE_ROWS = 256
VMEM_LIMIT = 56 * 1024 * 1024
NEG_BIG = -1e30

GLA_CHUNK = 64
GLA_SUB = 16
FLASH_TQ = 512
FLASH_TK = 1280


def _cparams(sem):
    return pltpu.CompilerParams(dimension_semantics=sem, vmem_limit_bytes=VMEM_LIMIT)


def _dot(a, b):
    return jnp.dot(a, b, preferred_element_type=F32)


def _dot_nt(a, b):
    return lax.dot_general(a, b, (((1,), (1,)), ((), ())), preferred_element_type=F32)


def _rms(x, g):
    return x * lax.rsqrt(jnp.mean(x * x, axis=-1, keepdims=True) + EPS) * g


def _stream(i):
    return jnp.where(i == 0, 0, 1)


def _ada_kernel(c_ref, w_ref, b_ref, o_ref):
    c = c_ref[...]
    s = (c * (1.0 / (1.0 + jnp.exp(-c)))).astype(BF16)
    o_ref[0] = _dot(s, w_ref[0].astype(BF16)) + b_ref[0]


def _ada(cc, ada_w, ada_b):
    depth, d, n = ada_w.shape
    tn = 1536
    return pl.pallas_call(
        _ada_kernel,
        grid=(depth, n // tn),
        in_specs=[pl.BlockSpec((8, d), lambda l, j: (0, 0)),
                  pl.BlockSpec((1, d, tn), lambda l, j: (l, 0, j)),
                  pl.BlockSpec((1, 1, tn), lambda l, j: (l, 0, j))],
        out_specs=pl.BlockSpec((1, 8, tn), lambda l, j: (l, 0, j)),
        out_shape=jax.ShapeDtypeStruct((depth, 8, n), F32),
        compiler_params=_cparams(("arbitrary", "arbitrary")),
        name="ada",
    )(cc, ada_w, ada_b.reshape(depth, 1, n))


def _proj0_kernel(x_ref, sc_ref, sh_ref, cos_ref, sin_ref, win_ref, wa_ref, ba_ref, qn_ref, wuq_ref,
                  kvn_ref, wukv_ref,
                  qg_ref, kg_ref, vg_ref, r_ref, gf_ref, gb_ref, qm_ref, km_ref, vm_ref, *, mla_scale):
    x = x_ref[...]
    h = x * lax.rsqrt(jnp.mean(x * x, axis=-1, keepdims=True) + EPS) * sc_ref[0] + sh_ref[0]
    u = _dot(h.astype(BF16), win_ref[...])
    qg_ref[...] = u[:, 0:256]
    kg_ref[...] = u[:, 256:512]
    vg_ref[...] = u[:, 512:1024].astype(BF16)
    r_ref[...] = u[:, 1024:1536]
    cq = u[:, 1536:1792]
    ckv = u[:, 1792:1920]
    kr = u[:, 1920:2048]
    krs = u[:, 2048:2176]
    a = u[:, 2176:2304]

    ga = _dot(a.astype(BF16), wa_ref[...]) + ba_ref[...]
    la = (jnp.minimum(ga, 0.0) - jnp.log1p(jnp.exp(-jnp.abs(ga)))) * (1.0 / GLA_TAU)
    gf_ref[...] = la[:, :GLA_QK]
    gb_ref[...] = la[:, GLA_QK:]

    cos = cos_ref[...]
    sin = sin_ref[...]
    qm = _dot(_rms(cq, qn_ref[...]).astype(BF16), wuq_ref[...])
    kvm = _dot(_rms(ckv, kvn_ref[...]).astype(BF16), wukv_ref[...])
    kroped = (kr * cos + krs * sin).astype(BF16)
    for hd in range(MLA_HEADS):
        nope = qm[:, hd * 128:(hd + 1) * 128]
        rp = qm[:, 512 + hd * 128:512 + (hd + 1) * 128]
        rps = qm[:, 1024 + hd * 128:1024 + (hd + 1) * 128]
        roped = rp * cos + rps * sin
        qm_ref[hd] = (jnp.concatenate([nope, roped], axis=1) * mla_scale).astype(BF16)
        km_ref[hd] = jnp.concatenate([kvm[:, hd * 128:(hd + 1) * 128].astype(BF16), kroped], axis=1)
        vm_ref[hd] = kvm[:, 512 + hd * 128:512 + (hd + 1) * 128].astype(BF16)


def _proj0(xs, scale, shift, cos_t, sin_t, w_in, wa, ba, qn, wuq, kvn, wukv):
    t, d = xs.shape
    tm = ROW_TILE
    row = lambda i: (i, 0)
    const = lambda i: (0, 0)
    strm = lambda i: (_stream(i), 0, 0)
    hrow = lambda i: (0, i, 0)
    kern = functools.partial(_proj0_kernel, mla_scale=float((MLA_NOPE + MLA_ROPE) ** -0.5))
    return pl.pallas_call(
        kern,
        grid=(t // tm,),
        in_specs=[pl.BlockSpec((tm, d), row),
                  pl.BlockSpec((1, 1, d), strm), pl.BlockSpec((1, 1, d), strm),
                  pl.BlockSpec((tm, LANES), row), pl.BlockSpec((tm, LANES), row),
                  pl.BlockSpec(w_in.shape, const), pl.BlockSpec(wa.shape, const), pl.BlockSpec(ba.shape, const),
                  pl.BlockSpec(qn.shape, const), pl.BlockSpec(wuq.shape, const),
                  pl.BlockSpec(kvn.shape, const), pl.BlockSpec(wukv.shape, const)],
        out_specs=[pl.BlockSpec((tm, GLA_QK), row), pl.BlockSpec((tm, GLA_QK), row),
                   pl.BlockSpec((tm, GLA_VW), row), pl.BlockSpec((tm, GLA_VW), row),
                   pl.BlockSpec((tm, GLA_QK), row), pl.BlockSpec((tm, GLA_QK), row),
                   pl.BlockSpec((MLA_HEADS, tm, MLA_QK_PAD), hrow),
                   pl.BlockSpec((MLA_HEADS, tm, MLA_QK_PAD), hrow),
                   pl.BlockSpec((MLA_HEADS, tm, MLA_V), hrow)],
        out_shape=[jax.ShapeDtypeStruct((t, GLA_QK), F32), jax.ShapeDtypeStruct((t, GLA_QK), F32),
                   jax.ShapeDtypeStruct((t, GLA_VW), BF16), jax.ShapeDtypeStruct((t, GLA_VW), F32),
                   jax.ShapeDtypeStruct((t, GLA_QK), F32), jax.ShapeDtypeStruct((t, GLA_QK), F32),
                   jax.ShapeDtypeStruct((MLA_HEADS, t, MLA_QK_PAD), BF16),
                   jax.ShapeDtypeStruct((MLA_HEADS, t, MLA_QK_PAD), BF16),
                   jax.ShapeDtypeStruct((MLA_HEADS, t, MLA_V), BF16)],
        compiler_params=_cparams(("arbitrary",)),
        name="proj0",
    )(xs, scale, shift, cos_t, sin_t, w_in, wa, ba, qn, wuq, kvn, wukv)


def _gla_tables():
    c, s = GLA_CHUNK, GLA_SUB
    t = np.arange(c)[:, None]
    u = np.arange(c)[None, :]
    bt, bu = t // s, u // s
    half = (c // s) // 2
    mats = [(bu == bt) & (u <= t), (bu == bt) & (u > t), bu < bt, bu > bt,
            (bu >= half) & (bu < bt), (bu > bt) & (bu < half)]
    m = np.concatenate([x.astype(np.float32) for x in mats], axis=0)
    sel = np.stack([(bt == bu) & (u <= t), bt == bu + 1, (bt >= half) & (bu < half)]).astype(np.float32)
    m_b = np.concatenate([x.astype(np.float32)[::-1, ::-1] for x in mats], axis=0)
    sel_b = sel[:, ::-1, ::-1]
    return np.stack([m, m_b]), np.stack([sel, sel_b])


def _gla_chunk(q, k, g, v_ref, vt_ref, j, m, sel, st_ref, o_ref):
    rows = pl.ds(j * GLA_CHUNK, GLA_CHUNK)
    c = GLA_CHUNK
    g_hi = g.astype(BF16)
    g_r1 = g - g_hi.astype(F32)
    g_mid = g_r1.astype(BF16)
    g_lo = (g_r1 - g_mid.astype(F32)).astype(BF16)
    cs = _dot(m, g_hi) + _dot(m, g_mid) + _dot(m, g_lo)
    own, rest, before, after = cs[0:c], cs[c:2 * c], cs[2 * c:3 * c], cs[3 * c:4 * c]
    far_q, far_k = cs[4 * c:5 * c], cs[5 * c:6 * c]
    qd = q * jnp.exp(own)
    kd = k * jnp.exp(-own)
    kh = k * jnp.exp(rest)
    q_in = qd * jnp.exp(before)
    k_out = kh * jnp.exp(after)
    q_far = qd * jnp.exp(far_q)
    k_far = kh * jnp.exp(far_k)
    decay = jnp.exp(own + rest + before + after)
    lane = lax.broadcasted_iota(jnp.int32, (c, LANES), 1)
    s_diag, s_adj, s_far = sel[0] > 0.5, sel[1] > 0.5, sel[2] > 0.5
    for p in range(GLA_HEADS // 2):
        lo, hi = p * LANES, (p + 1) * LANES
        kd_p = kd[:, lo:hi].astype(BF16)
        kh_p = kh[:, lo:hi].astype(BF16)
        kf_p = k_far[:, lo:hi].astype(BF16)
        st = st_ref[p]
        st_b = st.astype(BF16)
        upd = st * decay[0:1, lo:hi]
        for half in range(2):
            hd = 2 * p + half
            msk = (lane < GLA_DK) if half == 0 else (lane >= GLA_DK)
            pick = lambda z: jnp.where(msk, z[:, lo:hi], 0.0).astype(BF16)
            qd_h = pick(qd)
            a = jnp.where(s_diag, _dot_nt(qd_h, kd_p),
                          jnp.where(s_adj, _dot_nt(qd_h, kh_p),
                                    jnp.where(s_far, _dot_nt(pick(q_far), kf_p), 0.0)))
            v_h = v_ref[rows, hd * GLA_DV:(hd + 1) * GLA_DV]
            o_ref[rows, hd * GLA_DV:(hd + 1) * GLA_DV] = _dot(a.astype(BF16), v_h) + _dot_nt(pick(q_in), st_b)
            upd = upd + _dot(vt_ref[j, hd * GLA_DV:(hd + 1) * GLA_DV, :], pick(k_out))
        st_ref[p] = upd


def _gla_kernel(qf_ref, kf_ref, gf_ref, vf_ref, vtf_ref, qb_ref, kb_ref, gb_ref, vb_ref, vtb_ref, m_ref, sel_ref,
                of_ref, ob_ref, stf_ref, stb_ref):
    @pl.when(pl.program_id(0) == 0)
    def _():
        stf_ref[...] = jnp.zeros_like(stf_ref)
        stb_ref[...] = jnp.zeros_like(stb_ref)

    n = ROW_TILE // GLA_CHUNK
    for j in range(n):
        rows = pl.ds(j * GLA_CHUNK, GLA_CHUNK)
        _gla_chunk(qf_ref[rows, :], kf_ref[rows, :], gf_ref[rows, :], vf_ref, vtf_ref, j,
                   m_ref[0], sel_ref[0], stf_ref, of_ref)
        jb = n - 1 - j
        rows = pl.ds(jb * GLA_CHUNK, GLA_CHUNK)
        _gla_chunk(qb_ref[rows, :], kb_ref[rows, :], gb_ref[rows, :], vb_ref, vtb_ref, jb,
                   m_ref[1], sel_ref[1], stb_ref, ob_ref)


def _gla(q, k, v, gf, gb):
    t = q.shape[0]
    tm = ROW_TILE
    nb = t // tm
    n = tm // GLA_CHUNK
    m_np, sel_np = _gla_tables()
    m = jnp.asarray(m_np, BF16)
    sel = jnp.asarray(sel_np, F32)
    vt = jnp.transpose(v.reshape(t // GLA_CHUNK, GLA_CHUNK, GLA_VW), (0, 2, 1))
    fwd = lambda i: (i, 0)
    bwd = lambda i: (jnp.where(i == 0, 0, nb - i), 0)
    fwd_t = lambda i: (i, 0, 0)
    bwd_t = lambda i: (jnp.where(i == 0, 0, nb - i), 0, 0)
    c3 = lambda i: (0, 0, 0)
    c4 = lambda i: (0, 0, 0, 0)
    spec = lambda w, im: pl.BlockSpec((tm, w), im)
    return pl.pallas_call(
        _gla_kernel,
        grid=(nb,),
        in_specs=[spec(GLA_QK, fwd), spec(GLA_QK, fwd), spec(GLA_QK, fwd), spec(GLA_VW, fwd),
                  pl.BlockSpec((n, GLA_VW, GLA_CHUNK), fwd_t),
                  spec(GLA_QK, bwd), spec(GLA_QK, bwd), spec(GLA_QK, bwd), spec(GLA_VW, bwd),
                  pl.BlockSpec((n, GLA_VW, GLA_CHUNK), bwd_t),
                  pl.BlockSpec(m.shape, c3), pl.BlockSpec(sel.shape, c4)],
        out_specs=[spec(GLA_VW, fwd), spec(GLA_VW, bwd)],
        out_shape=[jax.ShapeDtypeStruct((t, GLA_VW), F32), jax.ShapeDtypeStruct((t, GLA_VW), F32)],
        scratch_shapes=[pltpu.VMEM((GLA_HEADS // 2, GLA_DV, LANES), F32),
                        pltpu.VMEM((GLA_HEADS // 2, GLA_DV, LANES), F32)],
        compiler_params=_cparams(("arbitrary",)),
        name="gla",
    )(q, k, gf, v, vt, q, k, gb, v, vt, m, sel)


def _flash_kernel(q_ref, k_ref, v_ref, o_ref, m_ref, l_ref, acc_ref, *, tk, n_chunks):
    m_ref[...] = jnp.full_like(m_ref, -jnp.inf)
    l_ref[...] = jnp.zeros_like(l_ref)
    acc_ref[...] = jnp.zeros_like(acc_ref)
    q = q_ref[0]

    def body(j, carry):
        off = pl.multiple_of(j * tk, tk)
        s = _dot_nt(q, k_ref[0, pl.ds(off, tk), :])
        m_old = m_ref[...]
        m_new = jnp.maximum(m_old, jnp.max(s, axis=-1, keepdims=True))
        alpha = jnp.exp(m_old - m_new)
        p = jnp.exp(s - m_new)
        l_ref[...] = alpha * l_ref[...] + jnp.sum(p, axis=-1, keepdims=True)
        acc_ref[...] = alpha * acc_ref[...] + _dot(p.astype(BF16), v_ref[0, pl.ds(off, tk), :])
        m_ref[...] = m_new
        return carry

    lax.fori_loop(0, n_chunks, body, 0)
    o_ref[...] = (acc_ref[...] * (1.0 / l_ref[...])).astype(o_ref.dtype)


def _flash(q, k, v, tq, tk):
    h, s, dq = q.shape
    t, dv = k.shape[1], v.shape[2]
    assert s % tq == 0 and t % tk == 0
    kern = functools.partial(_flash_kernel, tk=tk, n_chunks=t // tk)
    return pl.pallas_call(
        kern,
        grid=(h, s // tq),
        in_specs=[pl.BlockSpec((1, tq, dq), lambda hh, i: (hh, i, 0)),
                  pl.BlockSpec((1, t, dq), lambda hh, i: (hh, 0, 0)),
                  pl.BlockSpec((1, t, dv), lambda hh, i: (hh, 0, 0))],
        out_specs=pl.BlockSpec((tq, dv), lambda hh, i: (i, hh)),
        out_shape=jax.ShapeDtypeStruct((s, h * dv), BF16),
        scratch_shapes=[pltpu.VMEM((tq, 1), F32), pltpu.VMEM((tq, 1), F32), pltpu.VMEM((tq, dv), F32)],
        compiler_params=_cparams(("arbitrary", "arbitrary")),
        name="flash",
    )(q, k, v)


def _na_bias_tiles(rpb, rows):
    w = GRID_W
    qc = np.arange(w)[:, None]
    kc = np.arange(w)[None, :]
    c0 = np.clip(qc - NA_KW // 2, 0, w - NA_KW)
    col_ok = (kc >= c0) & (kc < c0 + NA_KW)
    dc = np.clip(kc - qc + NA_KW - 1, 0, 2 * NA_KW - 2)
    onehot = (np.arange(2 * NA_KW - 1)[:, None, None] == dc[None]).astype(np.float32)
    e1 = jnp.einsum('hrd,dqk->hrqk', rpb.astype(F32), jnp.asarray(onehot), precision=lax.Precision.HIGHEST)
    e1 = jnp.where(jnp.asarray(col_ok)[None, None], e1, NEG_BIG)
    e1 = jnp.concatenate([e1, jnp.full((NA_HEADS, 1, w, w), NEG_BIG, F32)], axis=1)
    idx = np.zeros((3, NA_QROWS, NA_WROWS), np.int32)
    for vi, (r_first, ks) in enumerate([(0, 0), (NA_QROWS, 0), (rows - NA_QROWS, rows - NA_WROWS)]):
        for a in range(NA_QROWS):
            qr = r_first + a
            r0 = min(max(qr - NA_KH // 2, 0), rows - NA_KH)
            for b in range(NA_WROWS):
                kr = ks + b
                idx[vi, a, b] = kr - qr + NA_KH - 1 if r0 <= kr < r0 + NA_KH else 2 * NA_KH - 1
    tiles = jnp.take(e1, jnp.asarray(idx.reshape(-1)), axis=1)
    tiles = tiles.reshape(NA_HEADS, 3, NA_QROWS, NA_WROWS, w, w)
    return jnp.transpose(tiles, (1, 0, 2, 4, 3, 5)).reshape(3, NA_HEADS, NA_QROWS * w, NA_WROWS * w)


def _na_kernel(q_ref, kv_k_ref, kv_v_ref, bias_ref, o_ref, *, rows):
    g = pl.program_id(1)
    nq = NA_QROWS * GRID_W
    nw = NA_WROWS * GRID_W
    ks = jnp.clip(g * NA_QROWS - NA_KH // 2, 0, rows - NA_WROWS)
    off = pl.multiple_of(CTX_LEN + ks * GRID_W, GRID_W)
    q = q_ref[...]
    k_w = kv_k_ref[pl.ds(off, nw), :]
    v_w = kv_v_ref[pl.ds(off, nw), :]
    k_c = kv_k_ref[0:CTX_LEN, :]
    v_c = kv_v_ref[0:CTX_LEN, :]
    lane = lax.broadcasted_iota(jnp.int32, (nq, LANES), 1)
    outs = []
    for half in range(2):
        msk = (lane < NA_DH) if half == 0 else (lane >= NA_DH)
        q_h = jnp.where(msk, q, jnp.zeros_like(q))
        s_w = _dot_nt(q_h, k_w) + bias_ref[0, half]
        s_c = _dot_nt(q_h, k_c)
        m = jnp.maximum(jnp.max(s_w, axis=-1, keepdims=True), jnp.max(s_c, axis=-1, keepdims=True))
        p_w = jnp.exp(s_w - m)
        p_c = jnp.exp(s_c - m)
        l = jnp.sum(p_w, axis=-1, keepdims=True) + jnp.sum(p_c, axis=-1, keepdims=True)
        o = _dot(p_w.astype(BF16), v_w) + _dot(p_c.astype(BF16), v_c)
        outs.append(o * (1.0 / l))
    o_ref[...] = jnp.where(lane < NA_DH, outs[0], outs[1]).astype(o_ref.dtype)


def _na(q, k, v, bias, rows):
    t, d = q.shape
    s = t - CTX_LEN
    nq = NA_QROWS * GRID_W
    assert nq == ROW_TILE and rows >= NA_WROWS and rows % NA_QROWS == 0
    n_groups = rows // NA_QROWS
    variant = lambda g: jnp.where(g == 0, 0, jnp.where(g == n_groups - 1, 2, 1))
    return pl.pallas_call(
        functools.partial(_na_kernel, rows=rows),
        grid=(NA_HEADS // 2, n_groups),
        in_specs=[pl.BlockSpec((nq, LANES), lambda p, g: (g + 1, p)),
                  pl.BlockSpec((t, LANES), lambda p, g: (0, p)),
                  pl.BlockSpec((t, LANES), lambda p, g: (0, p)),
                  pl.BlockSpec((1, 2, nq, NA_WROWS * GRID_W), lambda p, g: (variant(g), p, 0, 0))],
        out_specs=pl.BlockSpec((nq, LANES), lambda p, g: (g, p)),
        out_shape=jax.ShapeDtypeStruct((s, d), BF16),
        compiler_params=_cparams(("arbitrary", "arbitrary")),
        name="na",
    )(q, k, v, bias)


def _attn_out_kernel(*refs, gla):
    if gla:
        (x_ref, of_ref, ob_ref, r_ref, om_ref, on_ref, w_ref, g1_ref, gate_ref, sc_ref, sh_ref,
         rwh_ref, rwl_ref, rb_ref, xo_ref, h_ref, lg_ref) = refs
        o = of_ref[...] + ob_ref[...]
        r = r_ref[...]
        parts = []
        for hd in range(GLA_HEADS):
            sl = slice(hd * GLA_DV, (hd + 1) * GLA_DV)
            rh = r[:, sl]
            parts.append(_rms(o[:, sl], on_ref[...]) * (rh * (1.0 / (1.0 + jnp.exp(-rh)))))
        mix_a = jnp.concatenate(parts, axis=1).astype(BF16)
        y = _dot(mix_a, w_ref[0:GLA_VW, :]) + _dot(om_ref[...], w_ref[GLA_VW:, :])
    else:
        (x_ref, om_ref, w_ref, g1_ref, gate_ref, sc_ref, sh_ref,
         rwh_ref, rwl_ref, rb_ref, xo_ref, h_ref, lg_ref) = refs
        y = _dot(om_ref[...], w_ref[...])
    x = x_ref[...] + _rms(y, g1_ref[...]) * gate_ref[0]
    xo_ref[...] = x
    h = x * lax.rsqrt(jnp.mean(x * x, axis=-1, keepdims=True) + EPS) * sc_ref[0] + sh_ref[0]
    h_hi = h.astype(BF16)
    h_ref[...] = h_hi
    h_lo = (h - h_hi.astype(F32)).astype(BF16)
    lg_ref[...] = _dot(h_hi, rwh_ref[...]) + _dot(h_lo, rwh_ref[...]) + _dot(h_hi, rwl_ref[...]) + rb_ref[...]


def _attn_out(xs, mix_inputs, w_out, g1, gate, scale, shift, rw_hi, rw_lo, rb, *, gla, first_tile):
    d = xs.shape[1]
    tm = ROW_TILE
    t = mix_inputs[0].shape[0]
    row = lambda i: (i, 0)
    xrow = lambda i: (i + first_tile, 0)
    const = lambda i: (0, 0)
    strm = lambda i: (_stream(i + first_tile), 0, 0)
    if gla:
        of, ob, r, om, onorm = mix_inputs
        mix_specs = [pl.BlockSpec((tm, GLA_VW), row), pl.BlockSpec((tm, GLA_VW), row), pl.BlockSpec((tm, GLA_VW), row),
                     pl.BlockSpec((tm, MLA_VW), row), pl.BlockSpec(onorm.shape, const)]
    else:
        mix_specs = [pl.BlockSpec((tm, d), row)]
    return pl.pallas_call(
        functools.partial(_attn_out_kernel, gla=gla),
        grid=(t // tm,),
        in_specs=[pl.BlockSpec((tm, d), xrow)] + mix_specs + [
            pl.BlockSpec(w_out.shape, const), pl.BlockSpec(g1.shape, const),
            pl.BlockSpec((1, 1, d), strm), pl.BlockSpec((1, 1, d), strm), pl.BlockSpec((1, 1, d), strm),
            pl.BlockSpec(rw_hi.shape, const), pl.BlockSpec(rw_lo.shape, const), pl.BlockSpec(rb.shape, const)],
        out_specs=[pl.BlockSpec((tm, d), row), pl.BlockSpec((tm, d), row), pl.BlockSpec((tm, LANES), row)],
        out_shape=[jax.ShapeDtypeStruct((t, d), F32), jax.ShapeDtypeStruct((t, d), BF16),
                   jax.ShapeDtypeStruct((t, LANES), F32)],
        compiler_params=_cparams(("arbitrary",)),
        name="attn_out",
    )(xs, *mix_inputs, w_out, g1, gate, scale, shift, rw_hi, rw_lo, rb)


def _moe_kernel(be_ref, nv_ref, x_ref, gate_ref, w1_ref, b1_ref, w2_ref, b2_ref, o_ref):
    i = pl.program_id(0)

    @pl.when(i < nv_ref[0])
    def _():
        f = w2_ref.shape[1]
        u = _dot(x_ref[...], w1_ref[0]) + b1_ref[0]
        glu = jnp.minimum(u[:, :f], SWIGLU_LIMIT)
        lin = jnp.clip(u[:, f:], -SWIGLU_LIMIT, SWIGLU_LIMIT)
        act = glu * (1.0 / (1.0 + jnp.exp(-SWIGLU_ALPHA * glu))) * (lin + 1.0)
        y = _dot(act.astype(BF16), w2_ref[0]) + b2_ref[0]
        o_ref[...] = y * gate_ref[...]

    @pl.when(i >= nv_ref[0])
    def _():
        o_ref[...] = jnp.zeros_like(o_ref)


def _moe_experts(block_e, n_valid, xb, slot_gate, w1, b1, w2, b2):
    n_slots, d = xb.shape
    bm = MOE_ROWS
    f = w2.shape[1]
    grid_spec = pltpu.PrefetchScalarGridSpec(
        num_scalar_prefetch=2,
        grid=(n_slots // bm,),
        in_specs=[pl.BlockSpec((bm, d), lambda i, be, nv: (i, 0)),
                  pl.BlockSpec((bm, 1), lambda i, be, nv: (i, 0)),
                  pl.BlockSpec((1, d, 2 * f), lambda i, be, nv: (be[i], 0, 0)),
                  pl.BlockSpec((1, 1, 2 * f), lambda i, be, nv: (be[i], 0, 0)),
                  pl.BlockSpec((1, f, d), lambda i, be, nv: (be[i], 0, 0)),
                  pl.BlockSpec((1, 1, d), lambda i, be, nv: (be[i], 0, 0))],
        out_specs=pl.BlockSpec((bm, d), lambda i, be, nv: (i, 0)),
    )
    return pl.pallas_call(
        _moe_kernel,
        grid_spec=grid_spec,
        out_shape=jax.ShapeDtypeStruct((n_slots, d), F32),
        compiler_params=_cparams(("arbitrary",)),
        name="moe_experts",
    )(block_e, n_valid, xb, slot_gate, w1, b1, w2, b2)


def _moe(h, logits, w1, b1, w2, b2):
    t, d = h.shape
    bm = MOE_ROWS
    top_val, top_idx = lax.top_k(logits, TOP_K)
    gates = jax.nn.softmax(top_val, axis=-1)
    n_assign = t * TOP_K
    flat_e = top_idx.reshape(-1)
    flat_tok = jnp.arange(n_assign, dtype=jnp.int32) // TOP_K
    order = jnp.argsort(flat_e)
    e_sorted = flat_e[order]
    counts = jnp.bincount(flat_e, length=N_EXPERTS)
    padded = ((counts + bm - 1) // bm) * bm
    pad_end = jnp.cumsum(padded)
    pad_start = pad_end - padded
    start = jnp.cumsum(counts) - counts
    slot = (pad_start[e_sorted] + jnp.arange(n_assign) - start[e_sorted]).astype(jnp.int32)
    n_blocks = -(-n_assign // bm) + N_EXPERTS
    n_slots = n_blocks * bm
    slot_tok = jnp.full((n_slots,), t, jnp.int32).at[slot].set(flat_tok[order])
    slot_gate = jnp.zeros((n_slots,), F32).at[slot].set(gates.reshape(-1)[order])
    block_e = jnp.minimum(jnp.searchsorted(pad_end, jnp.arange(n_blocks) * bm, side='right'),
                          N_EXPERTS - 1).astype(jnp.int32)
    n_valid = (pad_end[-1] // bm).astype(jnp.int32).reshape(1)
    slot_of = jnp.zeros((n_assign,), jnp.int32).at[order].set(slot)
    h_pad = jnp.concatenate([h, jnp.zeros((1, d), h.dtype)], axis=0)
    xb = h_pad[slot_tok]
    yb = _moe_experts(block_e, n_valid, xb, slot_gate.reshape(n_slots, 1), w1, b1, w2, b2)
    return yb[slot_of.reshape(t, TOP_K)].sum(axis=1)


def _moe_weights(w1, b1, w2, b2):
    e, d, f2 = w1.shape
    w1p = jnp.concatenate([w1[..., 0::2], w1[..., 1::2]], axis=-1).astype(BF16)
    b1p = jnp.concatenate([b1[..., 0::2], b1[..., 1::2]], axis=-1).reshape(e, 1, f2)
    return w1p, b1p, w2.astype(BF16), b2.reshape(e, 1, w2.shape[2])


def _resid_kernel(*refs, project):
    if project:
        x_ref, y_ref, g3_ref, gate_ref, sc_ref, sh_ref, w_ref, xo_ref, q_ref, k_ref, v_ref = refs
    else:
        x_ref, y_ref, g3_ref, gate_ref, xo_ref = refs
    x = x_ref[...] + _rms(y_ref[...], g3_ref[...]) * gate_ref[0]
    xo_ref[...] = x
    if project:
        d = x.shape[1]
        h = x * lax.rsqrt(jnp.mean(x * x, axis=-1, keepdims=True) + EPS) * sc_ref[0] + sh_ref[0]
        u = _dot(h.astype(BF16), w_ref[...])
        q_ref[...] = u[:, 0:d].astype(BF16)
        k_ref[...] = u[:, d:2 * d].astype(BF16)
        v_ref[...] = u[:, 2 * d:3 * d].astype(BF16)


def _resid(xs, y, g3, gate, scale=None, shift=None, w=None, *, first_tile=0):
    t, d = xs.shape
    tm = ROW_TILE
    project = w is not None
    row = lambda i: (i, 0)
    const = lambda i: (0, 0)
    strm = lambda i: (_stream(i + first_tile), 0, 0)
    vec = pl.BlockSpec((1, 1, d), strm)
    in_specs = [pl.BlockSpec((tm, d), row), pl.BlockSpec((tm, d), row), pl.BlockSpec(g3.shape, const), vec]
    out_specs = [pl.BlockSpec((tm, d), row)]
    out_shape = [jax.ShapeDtypeStruct((t, d), F32)]
    args = [xs, y, g3, gate]
    if project:
        in_specs += [vec, vec, pl.BlockSpec(w.shape, const)]
        out_specs += [pl.BlockSpec((tm, d), row)] * 3
        out_shape += [jax.ShapeDtypeStruct((t, d), BF16)] * 3
        args += [scale, shift, w]
    return pl.pallas_call(
        functools.partial(_resid_kernel, project=project),
        grid=(t // tm,),
        in_specs=in_specs, out_specs=out_specs, out_shape=out_shape,
        compiler_params=_cparams(("arbitrary",)),
        name="resid",
    )(*args)


def _rope_tables(s):
    tok = jnp.arange(s)
    row = (tok // GRID_W).astype(F32)
    col = (tok % GRID_W).astype(F32)
    n_freq = MLA_ROPE // 4
    inv = ROPE_THETA ** (-jnp.arange(n_freq, dtype=F32) / n_freq)
    ang = jnp.concatenate([row[:, None] * inv, col[:, None] * inv], axis=-1)
    cos, sin = jnp.cos(ang), jnp.sin(ang)
    cos_t = jnp.concatenate([cos, cos, cos, cos], axis=-1)
    sin_t = jnp.concatenate([-sin, sin, -sin, sin], axis=-1)
    cos_t = jnp.concatenate([jnp.ones((CTX_LEN, LANES), F32), cos_t], axis=0)
    sin_t = jnp.concatenate([jnp.zeros((CTX_LEN, LANES), F32), sin_t], axis=0)
    return cos_t, sin_t


def _pad_cols(w, n):
    return jnp.pad(w, ((0, 0), (0, n - w.shape[1])))


def _swap_halves(w):
    half = w.shape[-1] // 2
    return jnp.concatenate([w[..., half:], w[..., :half]], axis=-1)


def _proj0_weights(in_w, wa_f, ba_f, wa_b, ba_b, wuq, wukv):
    o = np.cumsum((0, GLA_QK, GLA_QK, GLA_VW, GLA_VW, GLA_GATE_RANK, GLA_GATE_RANK, MLA_Q_LORA, MLA_KV_LORA,
                   MLA_ROPE))
    q, k, v, r, a_f, a_b, cq, ckv, kr = [in_w[:, o[i]:o[i + 1]] for i in range(9)]
    w_in = jnp.concatenate([q * (GLA_DK ** -0.5), k, v, r, cq, ckv, _pad_cols(kr, LANES),
                            _pad_cols(_swap_halves(kr), LANES),
                            _pad_cols(jnp.concatenate([a_f, a_b], axis=1), LANES)], axis=1).astype(BF16)
    wa = jnp.zeros((LANES, 2 * GLA_QK), F32)
    wa = wa.at[0:GLA_GATE_RANK, 0:GLA_QK].set(wa_f).at[GLA_GATE_RANK:2 * GLA_GATE_RANK, GLA_QK:].set(wa_b)
    ba = jnp.concatenate([ba_f, ba_b]).reshape(1, 2 * GLA_QK)
    wq = wuq.reshape(MLA_Q_LORA, MLA_HEADS, MLA_NOPE + MLA_ROPE)
    nope = wq[:, :, :MLA_NOPE].reshape(MLA_Q_LORA, MLA_HEADS * MLA_NOPE)
    rope = wq[:, :, MLA_NOPE:]
    pad3 = lambda z: jnp.pad(z, ((0, 0), (0, 0), (0, LANES - MLA_ROPE))).reshape(MLA_Q_LORA, MLA_HEADS * LANES)
    wuq_x = jnp.concatenate([nope, pad3(rope), pad3(_swap_halves(rope))], axis=1).astype(BF16)
    wkv = wukv.reshape(MLA_KV_LORA, MLA_HEADS, MLA_NOPE + MLA_V)
    wukv_x = jnp.concatenate([wkv[:, :, :MLA_NOPE].reshape(MLA_KV_LORA, -1),
                              wkv[:, :, MLA_NOPE:].reshape(MLA_KV_LORA, -1)], axis=1).astype(BF16)
    return w_in, wa.astype(BF16), ba, wuq_x, wukv_x


def _router_weights(rw, rb):
    rw = _pad_cols(rw, LANES)
    hi = rw.astype(BF16)
    lo = (rw - hi.astype(F32)).astype(BF16)
    return hi, lo, _pad_cols(rb.reshape(1, -1), LANES)


def kernel(x, c, ctx, c_ctx, ada_w, ada_b, norm_g, router_w, router_b, moe_w1, moe_b1, moe_w2, moe_b2, ab_in_w,
           gla_wa_f, gla_ba_f, gla_wa_b, gla_ba_b, gla_onorm, mla_qnorm, mla_wuq, mla_kvnorm, mla_wukv, ab_out_w,
           na_qkv_w, na_rpb, na_out_w):
    b, s, d = x.shape
    assert b == 1 and d == D_MODEL and ctx.shape == (1, CTX_LEN, d) and s % (NA_QROWS * GRID_W) == 0
    rows = s // GRID_W
    xs = jnp.concatenate([ctx[0], x[0]], axis=0)

    cc = jnp.zeros((8, d), F32).at[0].set(c[0]).at[1].set(c_ctx)
    mod = _ada(cc, ada_w, ada_b)

    def vectors(l):
        parts = jnp.split(mod[l, 0:2][::-1], 6, axis=-1)
        sh1, sc1, g1, sh2, sc2, g2 = [p.reshape(2, 1, d) for p in parts]
        return ((1 + sc1) * norm_g[l, 0], sh1, g1, (1 + sc2) * norm_g[l, 2], sh2, g2)

    sc1, sh1, g1, sc2, sh2, g2 = vectors(0)
    cos_t, sin_t = _rope_tables(s)
    w_in, wa, ba, wuq_x, wukv_x = _proj0_weights(ab_in_w[0], gla_wa_f[0], gla_ba_f[0], gla_wa_b[0], gla_ba_b[0],
                                                 mla_wuq[0], mla_wukv[0])
    qg, kg, vg, r, gf, gb, qm, km, vm = _proj0(xs, sc1, sh1, cos_t, sin_t, w_in, wa, ba,
                                               mla_qnorm[0].reshape(1, -1), wuq_x,
                                               mla_kvnorm[0].reshape(1, -1), wukv_x)
    o_f, o_b = _gla(qg, kg, vg, gf, gb)
    om_l = _flash(qm[:, CTX_LEN:], km, vm, FLASH_TQ, FLASH_TK)
    om_c = _flash(qm[:, :CTX_LEN], km[:, :CTX_LEN], vm[:, :CTX_LEN], CTX_LEN, CTX_LEN)
    om = jnp.concatenate([om_c, om_l], axis=0)
    rw_hi, rw_lo, rb = _router_weights(router_w[0], router_b[0])
    xs, h2, logits = _attn_out(xs, (o_f, o_b, r, om, gla_onorm[0].reshape(1, -1)), ab_out_w[0].astype(BF16),
                               norm_g[0, 1].reshape(1, -1), g1, sc2, sh2, rw_hi, rw_lo, rb, gla=True, first_tile=0)
    y = _moe(h2, logits[:, :N_EXPERTS], *_moe_weights(moe_w1[0], moe_b1[0], moe_w2[0], moe_b2[0]))

    sc1n, sh1n, g1n, sc2n, sh2n, g2n = vectors(1)
    qkv_w = jnp.concatenate([na_qkv_w[0][:, :d] * (NA_DH ** -0.5), na_qkv_w[0][:, d:]], axis=1).astype(BF16)
    xs, q, k, v = _resid(xs, y, norm_g[0, 3].reshape(1, -1), g2, sc1n, sh1n, qkv_w)
    o_na = _na(q, k, v, _na_bias_tiles(na_rpb[0], rows), rows)
    rw_hi, rw_lo, rb = _router_weights(router_w[1], router_b[1])
    x_l, h2, logits = _attn_out(xs, (o_na,), na_out_w[0].astype(BF16), norm_g[1, 1].reshape(1, -1), g1n, sc2n, sh2n,
                                rw_hi, rw_lo, rb, gla=False, first_tile=1)
    y = _moe(h2, logits[:, :N_EXPERTS], *_moe_weights(moe_w1[1], moe_b1[1], moe_w2[1], moe_b2[1]))
    (x_l,) = _resid(x_l, y, norm_g[1, 3].reshape(1, -1), g2n, first_tile=1)
    return x_l.reshape(1, s, d)
```

```python
import functools

import numpy as np
import jax
import jax.numpy as jnp
from jax import lax
from jax.experimental import pallas as pl
from jax.experimental.pallas import tpu as pltpu

F32 = jnp.float32
BF16 = jnp.bfloat16

D_MODEL = 1024
DEPTH = 2
CTX_LEN = 256
GRID_W = 64
EPS = 1e-6
ROPE_THETA = 10000.0

GLA_HEADS = 4
GLA_DK = 64
GLA_DV = 128
GLA_GATE_RANK = 16
GLA_TAU = 16.0
GLA_QK = GLA_HEADS * GLA_DK
GLA_VW = GLA_HEADS * GLA_DV

MLA_HEADS = 4
MLA_Q_LORA = 256
MLA_KV_LORA = 128
MLA_NOPE = 128
MLA_ROPE = 64
MLA_V = 128
MLA_VW = MLA_HEADS * MLA_V
MLA_QK_PAD = 256

NA_HEADS = 16
NA_DH = D_MODEL // NA_HEADS
NA_KH = 8
NA_KW = 16
NA_QROWS = 4
NA_WROWS = NA_QROWS + NA_KH

N_EXPERTS = 32
TOP_K = 4
SWIGLU_LIMIT = 7.0
SWIGLU_ALPHA = 1.702

LANES = 128
ROW_TILE = 256
MOE_ROWS = 256
VMEM_LIMIT = 56 * 1024 * 1024
NEG_BIG = -1e30

GLA_CHUNK = 64
GLA_SUB = 16
FLASH_TQ = 512
FLASH_TK = 1280
FLASH_SUB = 256


def _cparams(sem):
    return pltpu.CompilerParams(dimension_semantics=sem, vmem_limit_bytes=VMEM_LIMIT)


def _dot(a, b):
    return jnp.dot(a, b, preferred_element_type=F32)


def _dot_nt(a, b):
    return lax.dot_general(a, b, (((1,), (1,)), ((), ())), preferred_element_type=F32)


def _rms(x, g):
    return x * lax.rsqrt(jnp.mean(x * x, axis=-1, keepdims=True) + EPS) * g


def _stream(i):
    return jnp.where(i == 0, 0, 1)


def _ada_kernel(c_ref, w_ref, b_ref, o_ref):
    c = c_ref[...]
    s = (c * (1.0 / (1.0 + jnp.exp(-c)))).astype(BF16)
    o_ref[0] = _dot(s, w_ref[0].astype(BF16)) + b_ref[0]


def _ada(cc, ada_w, ada_b):
    depth, d, n = ada_w.shape
    tn = 1536
    return pl.pallas_call(
        _ada_kernel,
        grid=(depth, n // tn),
        in_specs=[pl.BlockSpec((8, d), lambda l, j: (0, 0)),
                  pl.BlockSpec((1, d, tn), lambda l, j: (l, 0, j)),
                  pl.BlockSpec((1, 1, tn), lambda l, j: (l, 0, j))],
        out_specs=pl.BlockSpec((1, 8, tn), lambda l, j: (l, 0, j)),
        out_shape=jax.ShapeDtypeStruct((depth, 8, n), F32),
        compiler_params=_cparams(("arbitrary", "arbitrary")),
        name="ada",
    )(cc, ada_w, ada_b.reshape(depth, 1, n))


def _proj0_kernel(x_ref, sc_ref, sh_ref, cos_ref, sin_ref, win_ref, wa_ref, ba_ref, qn_ref, wuq_ref,
                  kvn_ref, wukv_ref,
                  qg_ref, kg_ref, vg_ref, r_ref, gf_ref, gb_ref, qm_ref, km_ref, vm_ref, *, mla_scale):
    x = x_ref[...]
    h = x * lax.rsqrt(jnp.mean(x * x, axis=-1, keepdims=True) + EPS) * sc_ref[0] + sh_ref[0]
    u = _dot(h.astype(BF16), win_ref[...])
    qg_ref[...] = u[:, 0:256]
    kg_ref[...] = u[:, 256:512]
    vg_ref[...] = u[:, 512:1024].astype(BF16)
    r_ref[...] = u[:, 1024:1536]
    cq = u[:, 1536:1792]
    ckv = u[:, 1792:1920]
    kr = u[:, 1920:2048]
    krs = u[:, 2048:2176]
    a = u[:, 2176:2304]

    ga = _dot(a.astype(BF16), wa_ref[...]) + ba_ref[...]
    la = (jnp.minimum(ga, 0.0) - jnp.log1p(jnp.exp(-jnp.abs(ga)))) * (1.0 / GLA_TAU)
    gf_ref[...] = la[:, :GLA_QK]
    gb_ref[...] = la[:, GLA_QK:]

    cos = cos_ref[...]
    sin = sin_ref[...]
    qm = _dot(_rms(cq, qn_ref[...]).astype(BF16), wuq_ref[...])
    kvm = _dot(_rms(ckv, kvn_ref[...]).astype(BF16), wukv_ref[...])
    kroped = (kr * cos + krs * sin).astype(BF16)
    for hd in range(MLA_HEADS):
        nope = qm[:, hd * 128:(hd + 1) * 128]
        rp = qm[:, 512 + hd * 128:512 + (hd + 1) * 128]
        rps = qm[:, 1024 + hd * 128:1024 + (hd + 1) * 128]
        roped = rp * cos + rps * sin
        qm_ref[hd] = (jnp.concatenate([nope, roped], axis=1) * mla_scale).astype(BF16)
        km_ref[hd] = jnp.concatenate([kvm[:, hd * 128:(hd + 1) * 128].astype(BF16), kroped], axis=1)
        vm_ref[hd] = kvm[:, 512 + hd * 128:512 + (hd + 1) * 128].astype(BF16)


def _proj0(xs, scale, shift, cos_t, sin_t, w_in, wa, ba, qn, wuq, kvn, wukv):
    t, d = xs.shape
    tm = ROW_TILE
    row = lambda i: (i, 0)
    const = lambda i: (0, 0)
    strm = lambda i: (_stream(i), 0, 0)
    hrow = lambda i: (0, i, 0)
    kern = functools.partial(_proj0_kernel, mla_scale=float((MLA_NOPE + MLA_ROPE) ** -0.5))
    return pl.pallas_call(
        kern,
        grid=(t // tm,),
        in_specs=[pl.BlockSpec((tm, d), row),
                  pl.BlockSpec((1, 1, d), strm), pl.BlockSpec((1, 1, d), strm),
                  pl.BlockSpec((tm, LANES), row), pl.BlockSpec((tm, LANES), row),
                  pl.BlockSpec(w_in.shape, const), pl.BlockSpec(wa.shape, const), pl.BlockSpec(ba.shape, const),
                  pl.BlockSpec(qn.shape, const), pl.BlockSpec(wuq.shape, const),
                  pl.BlockSpec(kvn.shape, const), pl.BlockSpec(wukv.shape, const)],
        out_specs=[pl.BlockSpec((tm, GLA_QK), row), pl.BlockSpec((tm, GLA_QK), row),
                   pl.BlockSpec((tm, GLA_VW), row), pl.BlockSpec((tm, GLA_VW), row),
                   pl.BlockSpec((tm, GLA_QK), row), pl.BlockSpec((tm, GLA_QK), row),
                   pl.BlockSpec((MLA_HEADS, tm, MLA_QK_PAD), hrow),
                   pl.BlockSpec((MLA_HEADS, tm, MLA_QK_PAD), hrow),
                   pl.BlockSpec((MLA_HEADS, tm, MLA_V), hrow)],
        out_shape=[jax.ShapeDtypeStruct((t, GLA_QK), F32), jax.ShapeDtypeStruct((t, GLA_QK), F32),
                   jax.ShapeDtypeStruct((t, GLA_VW), BF16), jax.ShapeDtypeStruct((t, GLA_VW), F32),
                   jax.ShapeDtypeStruct((t, GLA_QK), F32), jax.ShapeDtypeStruct((t, GLA_QK), F32),
                   jax.ShapeDtypeStruct((MLA_HEADS, t, MLA_QK_PAD), BF16),
                   jax.ShapeDtypeStruct((MLA_HEADS, t, MLA_QK_PAD), BF16),
                   jax.ShapeDtypeStruct((MLA_HEADS, t, MLA_V), BF16)],
        compiler_params=_cparams(("arbitrary",)),
        name="proj0",
    )(xs, scale, shift, cos_t, sin_t, w_in, wa, ba, qn, wuq, kvn, wukv)


def _gla_tables():
    c, s = GLA_CHUNK, GLA_SUB
    t = np.arange(c)[:, None]
    u = np.arange(c)[None, :]
    bt, bu = t // s, u // s
    half = (c // s) // 2
    mats = [(bu == bt) & (u <= t), (bu == bt) & (u > t), bu < bt, bu > bt,
            (bu >= half) & (bu < bt), (bu > bt) & (bu < half)]
    m = np.concatenate([x.astype(np.float32) for x in mats], axis=0)
    sel = np.stack([(bt == bu) & (u <= t), bt == bu + 1, (bt >= half) & (bu < half)]).astype(np.float32)
    m_b = np.concatenate([x.astype(np.float32)[::-1, ::-1] for x in mats], axis=0)
    sel_b = sel[:, ::-1, ::-1]
    return np.stack([m, m_b]), np.stack([sel, sel_b])


def _gla_chunk(q, k, g, v_ref, vt_ref, j, m, sel, st_ref, o_ref):
    rows = pl.ds(j * GLA_CHUNK, GLA_CHUNK)
    c = GLA_CHUNK
    g_hi = g.astype(BF16)
    g_r1 = g - g_hi.astype(F32)
    g_mid = g_r1.astype(BF16)
    g_lo = (g_r1 - g_mid.astype(F32)).astype(BF16)
    cs = _dot(m, g_hi) + _dot(m, g_mid) + _dot(m, g_lo)
    own, rest, before, after = cs[0:c], cs[c:2 * c], cs[2 * c:3 * c], cs[3 * c:4 * c]
    far_q, far_k = cs[4 * c:5 * c], cs[5 * c:6 * c]
    qd = q * jnp.exp(own)
    kd = k * jnp.exp(-own)
    kh = k * jnp.exp(rest)
    q_in = qd * jnp.exp(before)
    k_out = kh * jnp.exp(after)
    q_far = qd * jnp.exp(far_q)
    k_far = kh * jnp.exp(far_k)
    decay = jnp.exp(own + rest + before + after)
    lane = lax.broadcasted_iota(jnp.int32, (c, LANES), 1)
    s_diag, s_adj, s_far = sel[0] > 0.5, sel[1] > 0.5, sel[2] > 0.5
    for p in range(GLA_HEADS // 2):
        lo, hi = p * LANES, (p + 1) * LANES
        kd_p = kd[:, lo:hi].astype(BF16)
        kh_p = kh[:, lo:hi].astype(BF16)
        kf_p = k_far[:, lo:hi].astype(BF16)
        st = st_ref[p]
        st_b = st.astype(BF16)
        upd = st * decay[0:1, lo:hi]
        for half in range(2):
            hd = 2 * p + half
            msk = (lane < GLA_DK) if half == 0 else (lane >= GLA_DK)
            pick = lambda z: jnp.where(msk, z[:, lo:hi], 0.0).astype(BF16)
            qd_h = pick(qd)
            a = jnp.where(s_diag, _dot_nt(qd_h, kd_p),
                          jnp.where(s_adj, _dot_nt(qd_h, kh_p),
                                    jnp.where(s_far, _dot_nt(pick(q_far), kf_p), 0.0)))
            v_h = v_ref[rows, hd * GLA_DV:(hd + 1) * GLA_DV]
            o_ref[rows, hd * GLA_DV:(hd + 1) * GLA_DV] = _dot(a.astype(BF16), v_h) + _dot_nt(pick(q_in), st_b)
            upd = upd + _dot(vt_ref[j, hd * GLA_DV:(hd + 1) * GLA_DV, :], pick(k_out))
        st_ref[p] = upd


def _gla_kernel(qf_ref, kf_ref, gf_ref, vf_ref, vtf_ref, qb_ref, kb_ref, gb_ref, vb_ref, vtb_ref, m_ref, sel_ref,
                of_ref, ob_ref, stf_ref, stb_ref):
    @pl.when(pl.program_id(0) == 0)
    def _():
        stf_ref[...] = jnp.zeros_like(stf_ref)
        stb_ref[...] = jnp.zeros_like(stb_ref)

    n = ROW_TILE // GLA_CHUNK
    for j in range(n):
        rows = pl.ds(j * GLA_CHUNK, GLA_CHUNK)
        _gla_chunk(qf_ref[rows, :], kf_ref[rows, :], gf_ref[rows, :], vf_ref, vtf_ref, j,
                   m_ref[0], sel_ref[0], stf_ref, of_ref)
        jb = n - 1 - j
        rows = pl.ds(jb * GLA_CHUNK, GLA_CHUNK)
        _gla_chunk(qb_ref[rows, :], kb_ref[rows, :], gb_ref[rows, :], vb_ref, vtb_ref, jb,
                   m_ref[1], sel_ref[1], stb_ref, ob_ref)


def _gla(q, k, v, gf, gb):
    t = q.shape[0]
    tm = ROW_TILE
    nb = t // tm
    n = tm // GLA_CHUNK
    m_np, sel_np = _gla_tables()
    m = jnp.asarray(m_np, BF16)
    sel = jnp.asarray(sel_np, F32)
    vt = jnp.transpose(v.reshape(t // GLA_CHUNK, GLA_CHUNK, GLA_VW), (0, 2, 1))
    fwd = lambda i: (i, 0)
    bwd = lambda i: (jnp.where(i == 0, 0, nb - i), 0)
    fwd_t = lambda i: (i, 0, 0)
    bwd_t = lambda i: (jnp.where(i == 0, 0, nb - i), 0, 0)
    c3 = lambda i: (0, 0, 0)
    c4 = lambda i: (0, 0, 0, 0)
    spec = lambda w, im: pl.BlockSpec((tm, w), im)
    return pl.pallas_call(
        _gla_kernel,
        grid=(nb,),
        in_specs=[spec(GLA_QK, fwd), spec(GLA_QK, fwd), spec(GLA_QK, fwd), spec(GLA_VW, fwd),
                  pl.BlockSpec((n, GLA_VW, GLA_CHUNK), fwd_t),
                  spec(GLA_QK, bwd), spec(GLA_QK, bwd), spec(GLA_QK, bwd), spec(GLA_VW, bwd),
                  pl.BlockSpec((n, GLA_VW, GLA_CHUNK), bwd_t),
                  pl.BlockSpec(m.shape, c3), pl.BlockSpec(sel.shape, c4)],
        out_specs=[spec(GLA_VW, fwd), spec(GLA_VW, bwd)],
        out_shape=[jax.ShapeDtypeStruct((t, GLA_VW), F32), jax.ShapeDtypeStruct((t, GLA_VW), F32)],
        scratch_shapes=[pltpu.VMEM((GLA_HEADS // 2, GLA_DV, LANES), F32),
                        pltpu.VMEM((GLA_HEADS // 2, GLA_DV, LANES), F32)],
        compiler_params=_cparams(("arbitrary",)),
        name="gla",
    )(q, k, gf, v, vt, q, k, gb, v, vt, m, sel)


def _flash_kernel(q_ref, k_ref, v_ref, o_ref, s_ref, mc_ref, m_ref, l_ref, acc_ref, *, tk, n_chunks):
    m_ref[...] = jnp.full_like(m_ref, -jnp.inf)
    l_ref[...] = jnp.zeros_like(l_ref)
    acc_ref[...] = jnp.zeros_like(acc_ref)
    q = q_ref[0]
    sub = min(tk, FLASH_SUB)

    def scores(j, slot):
        off = pl.multiple_of(j * tk, tk)
        s = _dot_nt(q, k_ref[0, pl.ds(off, tk), :])
        s_ref[slot] = s
        mc_ref[slot] = jnp.max(s, axis=-1, keepdims=True)

    def softmax_pv(j, slot):
        off = pl.multiple_of(j * tk, tk)
        m_old = m_ref[...]
        m_new = jnp.maximum(m_old, mc_ref[slot])
        alpha = jnp.exp(m_old - m_new)
        l = alpha * l_ref[...]
        acc = alpha * acc_ref[...]
        for c in range(tk // sub):
            p = jnp.exp(s_ref[slot, :, c * sub:(c + 1) * sub] - m_new)
            l = l + jnp.sum(p, axis=-1, keepdims=True)
            acc = acc + _dot(p.astype(BF16), v_ref[0, pl.ds(off + c * sub, sub), :])
        l_ref[...] = l
        acc_ref[...] = acc
        m_ref[...] = m_new

    scores(0, 0)

    def pair(i, carry):
        scores(2 * i + 1, 1)
        softmax_pv(2 * i, 0)
        scores(2 * i + 2, 0)
        softmax_pv(2 * i + 1, 1)
        return carry

    n_pairs = (n_chunks - 1) // 2
    lax.fori_loop(0, n_pairs, pair, 0)
    if n_chunks % 2 == 0:
        scores(n_chunks - 1, 1)
        softmax_pv(n_chunks - 2, 0)
        softmax_pv(n_chunks - 1, 1)
    else:
        softmax_pv(n_chunks - 1, 0)
    o_ref[...] = (acc_ref[...] * (1.0 / l_ref[...])).astype(o_ref.dtype)


def _flash(q, k, v, tq, tk):
    h, s, dq = q.shape
    t, dv = k.shape[1], v.shape[2]
    assert s % tq == 0 and t % tk == 0
    kern = functools.partial(_flash_kernel, tk=tk, n_chunks=t // tk)
    return pl.pallas_call(
        kern,
        grid=(h, s // tq),
        in_specs=[pl.BlockSpec((1, tq, dq), lambda hh, i: (hh, i, 0)),
                  pl.BlockSpec((1, t, dq), lambda hh, i: (hh, 0, 0)),
                  pl.BlockSpec((1, t, dv), lambda hh, i: (hh, 0, 0))],
        out_specs=pl.BlockSpec((tq, dv), lambda hh, i: (i, hh)),
        out_shape=jax.ShapeDtypeStruct((s, h * dv), BF16),
        scratch_shapes=[pltpu.VMEM((2, tq, tk), F32), pltpu.VMEM((2, tq, 1), F32),
                        pltpu.VMEM((tq, 1), F32), pltpu.VMEM((tq, 1), F32), pltpu.VMEM((tq, dv), F32)],
        compiler_params=_cparams(("arbitrary", "arbitrary")),
        name="flash",
    )(q, k, v)


def _na_bias_tiles(rpb, rows):
    w = GRID_W
    qc = np.arange(w)[:, None]
    kc = np.arange(w)[None, :]
    c0 = np.clip(qc - NA_KW // 2, 0, w - NA_KW)
    col_ok = (kc >= c0) & (kc < c0 + NA_KW)
    dc = np.clip(kc - qc + NA_KW - 1, 0, 2 * NA_KW - 2)
    onehot = (np.arange(2 * NA_KW - 1)[:, None, None] == dc[None]).astype(np.float32)
    e1 = jnp.einsum('hrd,dqk->hrqk', rpb.astype(F32), jnp.asarray(onehot), precision=lax.Precision.HIGHEST)
    e1 = jnp.where(jnp.asarray(col_ok)[None, None], e1, NEG_BIG)
    e1 = jnp.concatenate([e1, jnp.full((NA_HEADS, 1, w, w), NEG_BIG, F32)], axis=1)
    idx = np.zeros((3, NA_QROWS, NA_WROWS), np.int32)
    for vi, (r_first, ks) in enumerate([(0, 0), (NA_QROWS, 0), (rows - NA_QROWS, rows - NA_WROWS)]):
        for a in range(NA_QROWS):
            qr = r_first + a
            r0 = min(max(qr - NA_KH // 2, 0), rows - NA_KH)
            for b in range(NA_WROWS):
                kr = ks + b
                idx[vi, a, b] = kr - qr + NA_KH - 1 if r0 <= kr < r0 + NA_KH else 2 * NA_KH - 1
    tiles = jnp.take(e1, jnp.asarray(idx.reshape(-1)), axis=1)
    tiles = tiles.reshape(NA_HEADS, 3, NA_QROWS, NA_WROWS, w, w)
    return jnp.transpose(tiles, (1, 0, 2, 4, 3, 5)).reshape(3, NA_HEADS, NA_QROWS * w, NA_WROWS * w)


def _na_kernel(q_ref, kv_k_ref, kv_v_ref, bias_ref, o_ref, *, rows):
    g = pl.program_id(1)
    nq = NA_QROWS * GRID_W
    nw = NA_WROWS * GRID_W
    ks = jnp.clip(g * NA_QROWS - NA_KH // 2, 0, rows - NA_WROWS)
    off = pl.multiple_of(CTX_LEN + ks * GRID_W, GRID_W)
    q = q_ref[...]
    k_w = kv_k_ref[pl.ds(off, nw), :]
    v_w = kv_v_ref[pl.ds(off, nw), :]
    k_c = kv_k_ref[0:CTX_LEN, :]
    v_c = kv_v_ref[0:CTX_LEN, :]
    lane = lax.broadcasted_iota(jnp.int32, (nq, LANES), 1)
    outs = []
    for half in range(2):
        msk = (lane < NA_DH) if half == 0 else (lane >= NA_DH)
        q_h = jnp.where(msk, q, jnp.zeros_like(q))
        s_w = _dot_nt(q_h, k_w) + bias_ref[0, half]
        s_c = _dot_nt(q_h, k_c)
        m = jnp.maximum(jnp.max(s_w, axis=-1, keepdims=True), jnp.max(s_c, axis=-1, keepdims=True))
        p_w = jnp.exp(s_w - m)
        p_c = jnp.exp(s_c - m)
        l = jnp.sum(p_w, axis=-1, keepdims=True) + jnp.sum(p_c, axis=-1, keepdims=True)
        o = _dot(p_w.astype(BF16), v_w) + _dot(p_c.astype(BF16), v_c)
        outs.append(o * (1.0 / l))
    o_ref[...] = jnp.where(lane < NA_DH, outs[0], outs[1]).astype(o_ref.dtype)


def _na(q, k, v, bias, rows):
    t, d = q.shape
    s = t - CTX_LEN
    nq = NA_QROWS * GRID_W
    assert nq == ROW_TILE and rows >= NA_WROWS and rows % NA_QROWS == 0
    n_groups = rows // NA_QROWS
    variant = lambda g: jnp.where(g == 0, 0, jnp.where(g == n_groups - 1, 2, 1))
    return pl.pallas_call(
        functools.partial(_na_kernel, rows=rows),
        grid=(NA_HEADS // 2, n_groups),
        in_specs=[pl.BlockSpec((nq, LANES), lambda p, g: (g + 1, p)),
                  pl.BlockSpec((t, LANES), lambda p, g: (0, p)),
                  pl.BlockSpec((t, LANES), lambda p, g: (0, p)),
                  pl.BlockSpec((1, 2, nq, NA_WROWS * GRID_W), lambda p, g: (variant(g), p, 0, 0))],
        out_specs=pl.BlockSpec((nq, LANES), lambda p, g: (g, p)),
        out_shape=jax.ShapeDtypeStruct((s, d), BF16),
        compiler_params=_cparams(("arbitrary", "arbitrary")),
        name="na",
    )(q, k, v, bias)


def _attn_out_kernel(*refs, gla):
    if gla:
        (x_ref, of_ref, ob_ref, r_ref, om_ref, on_ref, w_ref, g1_ref, gate_ref, sc_ref, sh_ref,
         rwh_ref, rwl_ref, rb_ref, xo_ref, h_ref, lg_ref) = refs
        o = of_ref[...] + ob_ref[...]
        r = r_ref[...]
        parts = []
        for hd in range(GLA_HEADS):
            sl = slice(hd * GLA_DV, (hd + 1) * GLA_DV)
            rh = r[:, sl]
            parts.append(_rms(o[:, sl], on_ref[...]) * (rh * (1.0 / (1.0 + jnp.exp(-rh)))))
        mix_a = jnp.concatenate(parts, axis=1).astype(BF16)
        y = _dot(mix_a, w_ref[0:GLA_VW, :]) + _dot(om_ref[...], w_ref[GLA_VW:, :])
    else:
        (x_ref, om_ref, w_ref, g1_ref, gate_ref, sc_ref, sh_ref,
         rwh_ref, rwl_ref, rb_ref, xo_ref, h_ref, lg_ref) = refs
        y = _dot(om_ref[...], w_ref[...])
    x = x_ref[...] + _rms(y, g1_ref[...]) * gate_ref[0]
    xo_ref[...] = x
    h = x * lax.rsqrt(jnp.mean(x * x, axis=-1, keepdims=True) + EPS) * sc_ref[0] + sh_ref[0]
    h_hi = h.astype(BF16)
    h_ref[...] = h_hi
    h_lo = (h - h_hi.astype(F32)).astype(BF16)
    lg_ref[...] = _dot(h_hi, rwh_ref[...]) + _dot(h_lo, rwh_ref[...]) + _dot(h_hi, rwl_ref[...]) + rb_ref[...]


def _attn_out(xs, mix_inputs, w_out, g1, gate, scale, shift, rw_hi, rw_lo, rb, *, gla, first_tile):
    d = xs.shape[1]
    tm = ROW_TILE
    t = mix_inputs[0].shape[0]
    row = lambda i: (i, 0)
    xrow = lambda i: (i + first_tile, 0)
    const = lambda i: (0, 0)
    strm = lambda i: (_stream(i + first_tile), 0, 0)
    if gla:
        of, ob, r, om, onorm = mix_inputs
        mix_specs = [pl.BlockSpec((tm, GLA_VW), row), pl.BlockSpec((tm, GLA_VW), row), pl.BlockSpec((tm, GLA_VW), row),
                     pl.BlockSpec((tm, MLA_VW), row), pl.BlockSpec(onorm.shape, const)]
    else:
        mix_specs = [pl.BlockSpec((tm, d), row)]
    return pl.pallas_call(
        functools.partial(_attn_out_kernel, gla=gla),
        grid=(t // tm,),
        in_specs=[pl.BlockSpec((tm, d), xrow)] + mix_specs + [
            pl.BlockSpec(w_out.shape, const), pl.BlockSpec(g1.shape, const),
            pl.BlockSpec((1, 1, d), strm), pl.BlockSpec((1, 1, d), strm), pl.BlockSpec((1, 1, d), strm),
            pl.BlockSpec(rw_hi.shape, const), pl.BlockSpec(rw_lo.shape, const), pl.BlockSpec(rb.shape, const)],
        out_specs=[pl.BlockSpec((tm, d), row), pl.BlockSpec((tm, d), row), pl.BlockSpec((tm, LANES), row)],
        out_shape=[jax.ShapeDtypeStruct((t, d), F32), jax.ShapeDtypeStruct((t, d), BF16),
                   jax.ShapeDtypeStruct((t, LANES), F32)],
        compiler_params=_cparams(("arbitrary",)),
        name="attn_out",
    )(xs, *mix_inputs, w_out, g1, gate, scale, shift, rw_hi, rw_lo, rb)


def _moe_kernel(be_ref, nv_ref, x_ref, gate_ref, w1_ref, b1_ref, w2_ref, b2_ref, o_ref):
    i = pl.program_id(0)

    @pl.when(i < nv_ref[0])
    def _():
        f = w2_ref.shape[1]
        u = _dot(x_ref[...], w1_ref[0]) + b1_ref[0]
        acts = []
        for c in range(f // LANES):
            glu = jnp.minimum(u[:, 2 * c * LANES:(2 * c + 1) * LANES], SWIGLU_LIMIT)
            lin = jnp.clip(u[:, (2 * c + 1) * LANES:(2 * c + 2) * LANES], -SWIGLU_LIMIT, SWIGLU_LIMIT)
            acts.append((glu * (1.0 / (1.0 + jnp.exp(-SWIGLU_ALPHA * glu))) * (lin + 1.0)).astype(BF16))
        y = _dot(jnp.concatenate(acts, axis=1), w2_ref[0]) + b2_ref[0]
        o_ref[...] = (y * gate_ref[...]).astype(o_ref.dtype)

    @pl.when(i >= nv_ref[0])
    def _():
        o_ref[...] = jnp.zeros_like(o_ref)


def _w1_prep_kernel(w_ref, p_ref, o_ref):
    w = w_ref[0].astype(BF16)
    for c in range(w.shape[1] // (2 * LANES)):
        cols = slice(c * 2 * LANES, (c + 1) * 2 * LANES)
        o_ref[0, :, cols] = _dot(w[:, cols], p_ref[...]).astype(BF16)


def _w1_prep(w1):
    e, d, f2 = w1.shape
    grp = 2 * LANES
    src = np.concatenate([np.arange(0, grp, 2), np.arange(1, grp, 2)])
    perm = np.zeros((grp, grp), np.float32)
    perm[src, np.arange(grp)] = 1.0
    tn = 1024
    return pl.pallas_call(
        _w1_prep_kernel,
        grid=(e, f2 // tn),
        in_specs=[pl.BlockSpec((1, d, tn), lambda i, j: (i, 0, j)),
                  pl.BlockSpec((grp, grp), lambda i, j: (0, 0))],
        out_specs=pl.BlockSpec((1, d, tn), lambda i, j: (i, 0, j)),
        out_shape=jax.ShapeDtypeStruct((e, d, f2), BF16),
        compiler_params=_cparams(("arbitrary", "arbitrary")),
        name="w1_prep",
    )(w1, jnp.asarray(perm, BF16))


def _moe_experts(block_e, n_valid, xb, slot_gate, w1, b1, w2, b2):
    n_slots, d = xb.shape
    bm = MOE_ROWS
    f = w2.shape[1]
    grid_spec = pltpu.PrefetchScalarGridSpec(
        num_scalar_prefetch=2,
        grid=(n_slots // bm,),
        in_specs=[pl.BlockSpec((bm, d), lambda i, be, nv: (i, 0)),
                  pl.BlockSpec((bm, 1), lambda i, be, nv: (i, 0)),
                  pl.BlockSpec((1, d, 2 * f), lambda i, be, nv: (be[i], 0, 0)),
                  pl.BlockSpec((1, 1, 2 * f), lambda i, be, nv: (be[i], 0, 0)),
                  pl.BlockSpec((1, f, d), lambda i, be, nv: (be[i], 0, 0)),
                  pl.BlockSpec((1, 1, d), lambda i, be, nv: (be[i], 0, 0))],
        out_specs=pl.BlockSpec((bm, d), lambda i, be, nv: (i, 0)),
    )
    return pl.pallas_call(
        _moe_kernel,
        grid_spec=grid_spec,
        out_shape=jax.ShapeDtypeStruct((n_slots, d), BF16),
        compiler_params=_cparams(("arbitrary",)),
        name="moe_experts",
    )(block_e, n_valid, xb, slot_gate, w1, b1, w2, b2)


def _moe(h, logits, w1, b1, w2, b2):
    t, d = h.shape
    bm = MOE_ROWS
    top_val, top_idx = lax.top_k(logits, TOP_K)
    gates = jax.nn.softmax(top_val, axis=-1)
    n_assign = t * TOP_K
    n_blocks = -(-n_assign // bm) + N_EXPERTS
    n_slots = n_blocks * bm
    experts = jnp.arange(N_EXPERTS, dtype=jnp.int32)
    hit = jnp.any(top_idx[:, :, None] == experts, axis=1).astype(jnp.int32)
    counts = jnp.sum(hit, axis=0)
    before = jnp.cumsum(hit, axis=0) - hit
    padded = ((counts + bm - 1) // bm) * bm
    pad_end = jnp.cumsum(padded)
    pad_start = pad_end - padded
    start = jnp.cumsum(counts) - counts
    slot_of = pad_start[top_idx] + jnp.take_along_axis(before, top_idx, axis=1)
    block_e = jnp.minimum(jnp.sum(jnp.arange(n_blocks)[:, None] * bm >= pad_end[None, :], axis=1),
                          N_EXPERTS - 1).astype(jnp.int32)
    n_valid = (pad_end[-1] // bm).astype(jnp.int32).reshape(1)
    order = jnp.argsort(top_idx.reshape(-1))
    slot_e = jnp.repeat(block_e, bm)
    local = jnp.arange(n_slots, dtype=jnp.int32) - pad_start[slot_e]
    valid = local < counts[slot_e]
    assign = order[jnp.clip(start[slot_e] + local, 0, n_assign - 1)]
    slot_tok = jnp.where(valid, assign // TOP_K, 0)
    slot_gate = jnp.where(valid, gates.reshape(-1)[assign], 0.0)
    xb = h[slot_tok]
    yb = _moe_experts(block_e, n_valid, xb, slot_gate.reshape(n_slots, 1), w1, b1, w2, b2)
    return yb[slot_of.reshape(-1)].reshape(t, TOP_K * d)


def _moe_weights(w1, b1, w2, b2):
    e, d, f2 = w1.shape
    b1p = jnp.transpose(b1.reshape(e, f2 // (2 * LANES), LANES, 2), (0, 1, 3, 2)).reshape(e, 1, f2)
    return _w1_prep(w1), b1p, w2.astype(BF16), b2.reshape(e, 1, w2.shape[2])


def _resid_kernel(*refs, project):
    if project:
        x_ref, y_ref, g3_ref, gate_ref, sc_ref, sh_ref, w_ref, xo_ref, q_ref, k_ref, v_ref = refs
    else:
        x_ref, y_ref, g3_ref, gate_ref, xo_ref = refs
    d = x_ref.shape[1]
    y = y_ref[:, 0:d].astype(F32)
    for j in range(1, TOP_K):
        y = y + y_ref[:, j * d:(j + 1) * d].astype(F32)
    x = x_ref[...] + _rms(y, g3_ref[...]) * gate_ref[0]
    xo_ref[...] = x
    if project:
        h = x * lax.rsqrt(jnp.mean(x * x, axis=-1, keepdims=True) + EPS) * sc_ref[0] + sh_ref[0]
        u = _dot(h.astype(BF16), w_ref[...])
        q_ref[...] = u[:, 0:d].astype(BF16)
        k_ref[...] = u[:, d:2 * d].astype(BF16)
        v_ref[...] = u[:, 2 * d:3 * d].astype(BF16)


def _resid(xs, y, g3, gate, scale=None, shift=None, w=None, *, first_tile=0):
    t, d = xs.shape
    tm = ROW_TILE
    project = w is not None
    row = lambda i: (i, 0)
    const = lambda i: (0, 0)
    strm = lambda i: (_stream(i + first_tile), 0, 0)
    vec = pl.BlockSpec((1, 1, d), strm)
    in_specs = [pl.BlockSpec((tm, d), row), pl.BlockSpec((tm, TOP_K * d), row), pl.BlockSpec(g3.shape, const), vec]
    out_specs = [pl.BlockSpec((tm, d), row)]
    out_shape = [jax.ShapeDtypeStruct((t, d), F32)]
    args = [xs, y, g3, gate]
    if project:
        in_specs += [vec, vec, pl.BlockSpec(w.shape, const)]
        out_specs += [pl.BlockSpec((tm, d), row)] * 3
        out_shape += [jax.ShapeDtypeStruct((t, d), BF16)] * 3
        args += [scale, shift, w]
    return pl.pallas_call(
        functools.partial(_resid_kernel, project=project),
        grid=(t // tm,),
        in_specs=in_specs, out_specs=out_specs, out_shape=out_shape,
        compiler_params=_cparams(("arbitrary",)),
        name="resid",
    )(*args)


def _rope_tables(s):
    tok = jnp.arange(s)
    row = (tok // GRID_W).astype(F32)
    col = (tok % GRID_W).astype(F32)
    n_freq = MLA_ROPE // 4
    inv = ROPE_THETA ** (-jnp.arange(n_freq, dtype=F32) / n_freq)
    ang = jnp.concatenate([row[:, None] * inv, col[:, None] * inv], axis=-1)
    cos, sin = jnp.cos(ang), jnp.sin(ang)
    cos_t = jnp.concatenate([cos, cos, cos, cos], axis=-1)
    sin_t = jnp.concatenate([-sin, sin, -sin, sin], axis=-1)
    cos_t = jnp.concatenate([jnp.ones((CTX_LEN, LANES), F32), cos_t], axis=0)
    sin_t = jnp.concatenate([jnp.zeros((CTX_LEN, LANES), F32), sin_t], axis=0)
    return cos_t, sin_t


def _pad_cols(w, n):
    return jnp.pad(w, ((0, 0), (0, n - w.shape[1])))


def _swap_halves(w):
    half = w.shape[-1] // 2
    return jnp.concatenate([w[..., half:], w[..., :half]], axis=-1)


def _proj0_weights(in_w, wa_f, ba_f, wa_b, ba_b, wuq, wukv):
    o = np.cumsum((0, GLA_QK, GLA_QK, GLA_VW, GLA_VW, GLA_GATE_RANK, GLA_GATE_RANK, MLA_Q_LORA, MLA_KV_LORA,
                   MLA_ROPE))
    q, k, v, r, a_f, a_b, cq, ckv, kr = [in_w[:, o[i]:o[i + 1]] for i in range(9)]
    w_in = jnp.concatenate([q * (GLA_DK ** -0.5), k, v, r, cq, ckv, _pad_cols(kr, LANES),
                            _pad_cols(_swap_halves(kr), LANES),
                            _pad_cols(jnp.concatenate([a_f, a_b], axis=1), LANES)], axis=1).astype(BF16)
    wa = jnp.zeros((LANES, 2 * GLA_QK), F32)
    wa = wa.at[0:GLA_GATE_RANK, 0:GLA_QK].set(wa_f).at[GLA_GATE_RANK:2 * GLA_GATE_RANK, GLA_QK:].set(wa_b)
    ba = jnp.concatenate([ba_f, ba_b]).reshape(1, 2 * GLA_QK)
    wq = wuq.reshape(MLA_Q_LORA, MLA_HEADS, MLA_NOPE + MLA_ROPE)
    nope = wq[:, :, :MLA_NOPE].reshape(MLA_Q_LORA, MLA_HEADS * MLA_NOPE)
    rope = wq[:, :, MLA_NOPE:]
    pad3 = lambda z: jnp.pad(z, ((0, 0), (0, 0), (0, LANES - MLA_ROPE))).reshape(MLA_Q_LORA, MLA_HEADS * LANES)
    wuq_x = jnp.concatenate([nope, pad3(rope), pad3(_swap_halves(rope))], axis=1).astype(BF16)
    wkv = wukv.reshape(MLA_KV_LORA, MLA_HEADS, MLA_NOPE + MLA_V)
    wukv_x = jnp.concatenate([wkv[:, :, :MLA_NOPE].reshape(MLA_KV_LORA, -1),
                              wkv[:, :, MLA_NOPE:].reshape(MLA_KV_LORA, -1)], axis=1).astype(BF16)
    return w_in, wa.astype(BF16), ba, wuq_x, wukv_x


def _router_weights(rw, rb):
    rw = _pad_cols(rw, LANES)
    hi = rw.astype(BF16)
    lo = (rw - hi.astype(F32)).astype(BF16)
    return hi, lo, _pad_cols(rb.reshape(1, -1), LANES)


def kernel(x, c, ctx, c_ctx, ada_w, ada_b, norm_g, router_w, router_b, moe_w1, moe_b1, moe_w2, moe_b2, ab_in_w,
           gla_wa_f, gla_ba_f, gla_wa_b, gla_ba_b, gla_onorm, mla_qnorm, mla_wuq, mla_kvnorm, mla_wukv, ab_out_w,
           na_qkv_w, na_rpb, na_out_w):
    b, s, d = x.shape
    assert b == 1 and d == D_MODEL and ctx.shape == (1, CTX_LEN, d) and s % (NA_QROWS * GRID_W) == 0
    rows = s // GRID_W
    xs = jnp.concatenate([ctx[0], x[0]], axis=0)

    cc = jnp.zeros((8, d), F32).at[0].set(c[0]).at[1].set(c_ctx)
    mod = _ada(cc, ada_w, ada_b)

    def vectors(l):
        parts = jnp.split(mod[l, 0:2][::-1], 6, axis=-1)
        sh1, sc1, g1, sh2, sc2, g2 = [p.reshape(2, 1, d) for p in parts]
        return ((1 + sc1) * norm_g[l, 0], sh1, g1, (1 + sc2) * norm_g[l, 2], sh2, g2)

    sc1, sh1, g1, sc2, sh2, g2 = vectors(0)
    cos_t, sin_t = _rope_tables(s)
    w_in, wa, ba, wuq_x, wukv_x = _proj0_weights(ab_in_w[0], gla_wa_f[0], gla_ba_f[0], gla_wa_b[0], gla_ba_b[0],
                                                 mla_wuq[0], mla_wukv[0])
    qg, kg, vg, r, gf, gb, qm, km, vm = _proj0(xs, sc1, sh1, cos_t, sin_t, w_in, wa, ba,
                                               mla_qnorm[0].reshape(1, -1), wuq_x,
                                               mla_kvnorm[0].reshape(1, -1), wukv_x)
    o_f, o_b = _gla(qg, kg, vg, gf, gb)
    om_l = _flash(qm[:, CTX_LEN:], km, vm, FLASH_TQ, FLASH_TK)
    om_c = _flash(qm[:, :CTX_LEN], km[:, :CTX_LEN], vm[:, :CTX_LEN], CTX_LEN, CTX_LEN)
    om = jnp.concatenate([om_c, om_l], axis=0)
    rw_hi, rw_lo, rb = _router_weights(router_w[0], router_b[0])
    xs, h2, logits = _attn_out(xs, (o_f, o_b, r, om, gla_onorm[0].reshape(1, -1)), ab_out_w[0].astype(BF16),
                               norm_g[0, 1].reshape(1, -1), g1, sc2, sh2, rw_hi, rw_lo, rb, gla=True, first_tile=0)
    y = _moe(h2, logits[:, :N_EXPERTS], *_moe_weights(moe_w1[0], moe_b1[0], moe_w2[0], moe_b2[0]))

    sc1n, sh1n, g1n, sc2n, sh2n, g2n = vectors(1)
    qkv_w = jnp.concatenate([na_qkv_w[0][:, :d] * (NA_DH ** -0.5), na_qkv_w[0][:, d:]], axis=1).astype(BF16)
    xs, q, k, v = _resid(xs, y, norm_g[0, 3].reshape(1, -1), g2, sc1n, sh1n, qkv_w)
    o_na = _na(q, k, v, _na_bias_tiles(na_rpb[0], rows), rows)
    rw_hi, rw_lo, rb = _router_weights(router_w[1], router_b[1])
    x_l, h2, logits = _attn_out(xs, (o_na,), na_out_w[0].astype(BF16), norm_g[1, 1].reshape(1, -1), g1n, sc2n, sh2n,
                                rw_hi, rw_lo, rb, gla=False, first_tile=1)
    y = _moe(h2, logits[:, :N_EXPERTS], *_moe_weights(moe_w1[1], moe_b1[1], moe_w2[1], moe_b2[1]))
    (x_l,) = _resid(x_l, y, norm_g[1, 3].reshape(1, -1), g2n, first_tile=1)
    return x_l.reshape(1, s, d)
```

```python
import functools

import numpy as np
import jax
import jax.numpy as jnp
from jax import lax
from jax.experimental import pallas as pl
from jax.experimental.pallas import tpu as pltpu

F32 = jnp.float32
BF16 = jnp.bfloat16

D_MODEL = 1024
DEPTH = 2
CTX_LEN = 256
GRID_W = 64
EPS = 1e-6
ROPE_THETA = 10000.0

GLA_HEADS = 4
GLA_DK = 64
GLA_DV = 128
GLA_GATE_RANK = 16
GLA_TAU = 16.0
GLA_QK = GLA_HEADS * GLA_DK
GLA_VW = GLA_HEADS * GLA_DV

MLA_HEADS = 4
MLA_Q_LORA = 256
MLA_KV_LORA = 128
MLA_NOPE = 128
MLA_ROPE = 64
MLA_V = 128
MLA_VW = MLA_HEADS * MLA_V
MLA_QK_PAD = 256

NA_HEADS = 16
NA_DH = D_MODEL // NA_HEADS
NA_KH = 8
NA_KW = 16
NA_QROWS = 4
NA_WROWS = NA_QROWS + NA_KH

N_EXPERTS = 32
TOP_K = 4
SWIGLU_LIMIT = 7.0
SWIGLU_ALPHA = 1.702

LANES = 128
ROW_TILE = 256
MOE_ROWS = 256
VMEM_LIMIT = 56 * 1024 * 1024
NEG_BIG = -1e30

GLA_CHUNK = 64
GLA_SUB = 16
FLASH_TQ = 512
FLASH_TK = 3328
FLASH_SUB = 256


def _cparams(sem):
    return pltpu.CompilerParams(dimension_semantics=sem, vmem_limit_bytes=VMEM_LIMIT)


def _dot(a, b):
    return jnp.dot(a, b, preferred_element_type=F32)


def _dot_nt(a, b):
    return lax.dot_general(a, b, (((1,), (1,)), ((), ())), preferred_element_type=F32)


def _rms(x, g):
    return x * lax.rsqrt(jnp.mean(x * x, axis=-1, keepdims=True) + EPS) * g


def _stream(i):
    return jnp.where(i == 0, 0, 1)


def _ada_kernel(c_ref, w_ref, b_ref, o_ref):
    c = c_ref[...]
    s = (c * (1.0 / (1.0 + jnp.exp(-c)))).astype(BF16)
    o_ref[0] = _dot(s, w_ref[0].astype(BF16)) + b_ref[0]


def _ada(cc, ada_w, ada_b):
    depth, d, n = ada_w.shape
    tn = 1536
    return pl.pallas_call(
        _ada_kernel,
        grid=(depth, n // tn),
        in_specs=[pl.BlockSpec((8, d), lambda l, j: (0, 0)),
                  pl.BlockSpec((1, d, tn), lambda l, j: (l, 0, j)),
                  pl.BlockSpec((1, 1, tn), lambda l, j: (l, 0, j))],
        out_specs=pl.BlockSpec((1, 8, tn), lambda l, j: (l, 0, j)),
        out_shape=jax.ShapeDtypeStruct((depth, 8, n), F32),
        compiler_params=_cparams(("arbitrary", "arbitrary")),
        name="ada",
    )(cc, ada_w, ada_b.reshape(depth, 1, n))


def _proj0_kernel(x_ref, sc_ref, sh_ref, cos_ref, sin_ref, win_ref, wa_ref, ba_ref, qn_ref, wuq_ref,
                  kvn_ref, wukv_ref,
                  qg_ref, kg_ref, vg_ref, r_ref, gf_ref, gb_ref, qm_ref, km_ref, vm_ref, *, mla_scale):
    x = x_ref[...]
    h = x * lax.rsqrt(jnp.mean(x * x, axis=-1, keepdims=True) + EPS) * sc_ref[0] + sh_ref[0]
    u = _dot(h.astype(BF16), win_ref[...])
    qg_ref[...] = u[:, 0:256]
    kg_ref[...] = u[:, 256:512]
    vg_ref[...] = u[:, 512:1024].astype(BF16)
    r_ref[...] = u[:, 1024:1536]
    cq = u[:, 1536:1792]
    ckv = u[:, 1792:1920]
    kr = u[:, 1920:2048]
    krs = u[:, 2048:2176]
    a = u[:, 2176:2304]

    ga = _dot(a.astype(BF16), wa_ref[...]) + ba_ref[...]
    la = (jnp.minimum(ga, 0.0) - jnp.log1p(jnp.exp(-jnp.abs(ga)))) * (1.0 / GLA_TAU)
    gf_ref[...] = la[:, :GLA_QK]
    gb_ref[...] = la[:, GLA_QK:]

    cos = cos_ref[...]
    sin = sin_ref[...]
    qm = _dot(_rms(cq, qn_ref[...]).astype(BF16), wuq_ref[...])
    kvm = _dot(_rms(ckv, kvn_ref[...]).astype(BF16), wukv_ref[...])
    kroped = (kr * cos + krs * sin).astype(BF16)
    for hd in range(MLA_HEADS):
        nope = qm[:, hd * 128:(hd + 1) * 128]
        rp = qm[:, 512 + hd * 128:512 + (hd + 1) * 128]
        rps = qm[:, 1024 + hd * 128:1024 + (hd + 1) * 128]
        roped = rp * cos + rps * sin
        qm_ref[hd] = (jnp.concatenate([nope, roped], axis=1) * mla_scale).astype(BF16)
        km_ref[hd] = jnp.concatenate([kvm[:, hd * 128:(hd + 1) * 128].astype(BF16), kroped], axis=1)
        v_h = kvm[:, 512 + hd * 128:512 + (hd + 1) * 128].astype(BF16)
        vm_ref[hd] = jnp.concatenate([v_h, jnp.ones_like(v_h)], axis=1)


def _proj0(xs, scale, shift, cos_t, sin_t, w_in, wa, ba, qn, wuq, kvn, wukv):
    t, d = xs.shape
    tm = ROW_TILE
    row = lambda i: (i, 0)
    const = lambda i: (0, 0)
    strm = lambda i: (_stream(i), 0, 0)
    hrow = lambda i: (0, i, 0)
    kern = functools.partial(_proj0_kernel, mla_scale=float((MLA_NOPE + MLA_ROPE) ** -0.5 * np.log2(np.e)))
    return pl.pallas_call(
        kern,
        grid=(t // tm,),
        in_specs=[pl.BlockSpec((tm, d), row),
                  pl.BlockSpec((1, 1, d), strm), pl.BlockSpec((1, 1, d), strm),
                  pl.BlockSpec((tm, LANES), row), pl.BlockSpec((tm, LANES), row),
                  pl.BlockSpec(w_in.shape, const), pl.BlockSpec(wa.shape, const), pl.BlockSpec(ba.shape, const),
                  pl.BlockSpec(qn.shape, const), pl.BlockSpec(wuq.shape, const),
                  pl.BlockSpec(kvn.shape, const), pl.BlockSpec(wukv.shape, const)],
        out_specs=[pl.BlockSpec((tm, GLA_QK), row), pl.BlockSpec((tm, GLA_QK), row),
                   pl.BlockSpec((tm, GLA_VW), row), pl.BlockSpec((tm, GLA_VW), row),
                   pl.BlockSpec((tm, GLA_QK), row), pl.BlockSpec((tm, GLA_QK), row),
                   pl.BlockSpec((MLA_HEADS, tm, MLA_QK_PAD), hrow),
                   pl.BlockSpec((MLA_HEADS, tm, MLA_QK_PAD), hrow),
                   pl.BlockSpec((MLA_HEADS, tm, 2 * MLA_V), hrow)],
        out_shape=[jax.ShapeDtypeStruct((t, GLA_QK), F32), jax.ShapeDtypeStruct((t, GLA_QK), F32),
                   jax.ShapeDtypeStruct((t, GLA_VW), BF16), jax.ShapeDtypeStruct((t, GLA_VW), F32),
                   jax.ShapeDtypeStruct((t, GLA_QK), F32), jax.ShapeDtypeStruct((t, GLA_QK), F32),
                   jax.ShapeDtypeStruct((MLA_HEADS, t, MLA_QK_PAD), BF16),
                   jax.ShapeDtypeStruct((MLA_HEADS, t, MLA_QK_PAD), BF16),
                   jax.ShapeDtypeStruct((MLA_HEADS, t, 2 * MLA_V), BF16)],
        compiler_params=_cparams(("arbitrary",)),
        name="proj0",
    )(xs, scale, shift, cos_t, sin_t, w_in, wa, ba, qn, wuq, kvn, wukv)


def _gla_tables():
    c, s = GLA_CHUNK, GLA_SUB
    t = np.arange(c)[:, None]
    u = np.arange(c)[None, :]
    bt, bu = t // s, u // s
    half = (c // s) // 2
    mats = [(bu == bt) & (u <= t), (bu == bt) & (u > t), bu < bt, bu > bt,
            (bu >= half) & (bu < bt), (bu > bt) & (bu < half)]
    m = np.concatenate([x.astype(np.float32) for x in mats], axis=0)
    sel = np.stack([(bt == bu) & (u <= t), bt == bu + 1, (bt >= half) & (bu < half)]).astype(np.float32)
    m_b = np.concatenate([x.astype(np.float32)[::-1, ::-1] for x in mats], axis=0)
    sel_b = sel[:, ::-1, ::-1]
    return np.stack([m, m_b]), np.stack([sel, sel_b])


def _gla_chunk(q, k, g, v_ref, vt_ref, j, m, sel, st_ref, o_ref):
    rows = pl.ds(j * GLA_CHUNK, GLA_CHUNK)
    c = GLA_CHUNK
    g_hi = g.astype(BF16)
    g_r1 = g - g_hi.astype(F32)
    g_mid = g_r1.astype(BF16)
    g_lo = (g_r1 - g_mid.astype(F32)).astype(BF16)
    cs = _dot(m, g_hi) + _dot(m, g_mid) + _dot(m, g_lo)
    own, rest, before, after = cs[0:c], cs[c:2 * c], cs[2 * c:3 * c], cs[3 * c:4 * c]
    far_q, far_k = cs[4 * c:5 * c], cs[5 * c:6 * c]
    qd = q * jnp.exp(own)
    kd = k * jnp.exp(-own)
    kh = k * jnp.exp(rest)
    q_in = qd * jnp.exp(before)
    k_out = kh * jnp.exp(after)
    q_far = qd * jnp.exp(far_q)
    k_far = kh * jnp.exp(far_k)
    decay = jnp.exp(own + rest + before + after)
    lane = lax.broadcasted_iota(jnp.int32, (c, LANES), 1)
    s_diag, s_adj, s_far = sel[0] > 0.5, sel[1] > 0.5, sel[2] > 0.5
    for p in range(GLA_HEADS // 2):
        lo, hi = p * LANES, (p + 1) * LANES
        kd_p = kd[:, lo:hi].astype(BF16)
        kh_p = kh[:, lo:hi].astype(BF16)
        kf_p = k_far[:, lo:hi].astype(BF16)
        st = st_ref[p]
        st_b = st.astype(BF16)
        upd = st * decay[0:1, lo:hi]
        for half in range(2):
            hd = 2 * p + half
            msk = (lane < GLA_DK) if half == 0 else (lane >= GLA_DK)
            pick = lambda z: jnp.where(msk, z[:, lo:hi], 0.0).astype(BF16)
            qd_h = pick(qd)
            a = jnp.where(s_diag, _dot_nt(qd_h, kd_p),
                          jnp.where(s_adj, _dot_nt(qd_h, kh_p),
                                    jnp.where(s_far, _dot_nt(pick(q_far), kf_p), 0.0)))
            v_h = v_ref[rows, hd * GLA_DV:(hd + 1) * GLA_DV]
            o_ref[rows, hd * GLA_DV:(hd + 1) * GLA_DV] = _dot(a.astype(BF16), v_h) + _dot_nt(pick(q_in), st_b)
            upd = upd + _dot(vt_ref[j, hd * GLA_DV:(hd + 1) * GLA_DV, :], pick(k_out))
        st_ref[p] = upd


def _gla_kernel(qf_ref, kf_ref, gf_ref, vf_ref, vtf_ref, qb_ref, kb_ref, gb_ref, vb_ref, vtb_ref, m_ref, sel_ref,
                of_ref, ob_ref, stf_ref, stb_ref):
    @pl.when(pl.program_id(0) == 0)
    def _():
        stf_ref[...] = jnp.zeros_like(stf_ref)
        stb_ref[...] = jnp.zeros_like(stb_ref)

    n = ROW_TILE // GLA_CHUNK
    for j in range(n):
        rows = pl.ds(j * GLA_CHUNK, GLA_CHUNK)
        _gla_chunk(qf_ref[rows, :], kf_ref[rows, :], gf_ref[rows, :], vf_ref, vtf_ref, j,
                   m_ref[0], sel_ref[0], stf_ref, of_ref)
        jb = n - 1 - j
        rows = pl.ds(jb * GLA_CHUNK, GLA_CHUNK)
        _gla_chunk(qb_ref[rows, :], kb_ref[rows, :], gb_ref[rows, :], vb_ref, vtb_ref, jb,
                   m_ref[1], sel_ref[1], stb_ref, ob_ref)


def _gla(q, k, v, gf, gb):
    t = q.shape[0]
    tm = ROW_TILE
    nb = t // tm
    n = tm // GLA_CHUNK
    m_np, sel_np = _gla_tables()
    m = jnp.asarray(m_np, BF16)
    sel = jnp.asarray(sel_np, F32)
    vt = jnp.transpose(v.reshape(t // GLA_CHUNK, GLA_CHUNK, GLA_VW), (0, 2, 1))
    fwd = lambda i: (i, 0)
    bwd = lambda i: (jnp.where(i == 0, 0, nb - i), 0)
    fwd_t = lambda i: (i, 0, 0)
    bwd_t = lambda i: (jnp.where(i == 0, 0, nb - i), 0, 0)
    c3 = lambda i: (0, 0, 0)
    c4 = lambda i: (0, 0, 0, 0)
    spec = lambda w, im: pl.BlockSpec((tm, w), im)
    return pl.pallas_call(
        _gla_kernel,
        grid=(nb,),
        in_specs=[spec(GLA_QK, fwd), spec(GLA_QK, fwd), spec(GLA_QK, fwd), spec(GLA_VW, fwd),
                  pl.BlockSpec((n, GLA_VW, GLA_CHUNK), fwd_t),
                  spec(GLA_QK, bwd), spec(GLA_QK, bwd), spec(GLA_QK, bwd), spec(GLA_VW, bwd),
                  pl.BlockSpec((n, GLA_VW, GLA_CHUNK), bwd_t),
                  pl.BlockSpec(m.shape, c3), pl.BlockSpec(sel.shape, c4)],
        out_specs=[spec(GLA_VW, fwd), spec(GLA_VW, bwd)],
        out_shape=[jax.ShapeDtypeStruct((t, GLA_VW), F32), jax.ShapeDtypeStruct((t, GLA_VW), F32)],
        scratch_shapes=[pltpu.VMEM((GLA_HEADS // 2, GLA_DV, LANES), F32),
                        pltpu.VMEM((GLA_HEADS // 2, GLA_DV, LANES), F32)],
        compiler_params=_cparams(("arbitrary",)),
        name="gla",
    )(q, k, gf, v, vt, q, k, gb, v, vt, m, sel)


def _flash_kernel(q_ref, k_ref, v_ref, o_ref, s_ref, mc_ref, m_ref, acc_ref, *, tk, n_chunks):
    m_ref[...] = jnp.full_like(m_ref, -jnp.inf)
    acc_ref[...] = jnp.zeros_like(acc_ref)
    q = q_ref[0]
    tq = q.shape[0]
    sub = min(tk, FLASH_SUB)
    dv = o_ref.shape[1]

    def scores(j, slot):
        off = pl.multiple_of(j * tk, tk)
        s = _dot_nt(q, k_ref[0, pl.ds(off, tk), :])
        s_ref[slot] = s
        mc_ref[slot] = jnp.broadcast_to(jnp.max(s, axis=-1, keepdims=True), (tq, LANES))

    def softmax_pv(j, slot):
        off = pl.multiple_of(j * tk, tk)
        m_old = m_ref[...]
        m_new = jnp.maximum(m_old, mc_ref[slot])
        alpha = jnp.exp2(m_old - m_new)
        m_ref[...] = m_new
        acc = jnp.concatenate([alpha, alpha], axis=1) * acc_ref[...]
        for c in range(tk // sub):
            ps = [jnp.exp2(s_ref[slot, :, c * sub + a * LANES:c * sub + (a + 1) * LANES] - m_new).astype(BF16)
                  for a in range(sub // LANES)]
            acc = acc + _dot(jnp.concatenate(ps, axis=1), v_ref[0, pl.ds(off + c * sub, sub), :])
        acc_ref[...] = acc

    scores(0, 0)

    def pair(i, carry):
        scores(2 * i + 1, 1)
        softmax_pv(2 * i, 0)
        scores(2 * i + 2, 0)
        softmax_pv(2 * i + 1, 1)
        return carry

    n_pairs = (n_chunks - 1) // 2
    lax.fori_loop(0, n_pairs, pair, 0)
    if n_chunks % 2 == 0:
        scores(n_chunks - 1, 1)
        softmax_pv(n_chunks - 2, 0)
        softmax_pv(n_chunks - 1, 1)
    else:
        softmax_pv(n_chunks - 1, 0)
    o_ref[...] = (acc_ref[:, :dv] * (1.0 / acc_ref[:, dv:])).astype(o_ref.dtype)


def _flash(q, k, v, tq, tk):
    h, s, dq = q.shape
    t, dv = k.shape[1], v.shape[2] // 2
    assert s % tq == 0 and t % tk == 0
    kern = functools.partial(_flash_kernel, tk=tk, n_chunks=t // tk)
    return pl.pallas_call(
        kern,
        grid=(h, s // tq),
        in_specs=[pl.BlockSpec((1, tq, dq), lambda hh, i: (hh, i, 0)),
                  pl.BlockSpec((1, t, dq), lambda hh, i: (hh, 0, 0)),
                  pl.BlockSpec((1, t, 2 * dv), lambda hh, i: (hh, 0, 0))],
        out_specs=pl.BlockSpec((tq, dv), lambda hh, i: (i, hh)),
        out_shape=jax.ShapeDtypeStruct((s, h * dv), BF16),
        scratch_shapes=[pltpu.VMEM((2, tq, tk), F32), pltpu.VMEM((2, tq, LANES), F32),
                        pltpu.VMEM((tq, LANES), F32), pltpu.VMEM((tq, 2 * dv), F32)],
        compiler_params=_cparams(("arbitrary", "arbitrary")),
        name="flash",
    )(q, k, v)


def _na_bias_tiles(rpb, rows):
    w = GRID_W
    qc = np.arange(w)[:, None]
    kc = np.arange(w)[None, :]
    c0 = np.clip(qc - NA_KW // 2, 0, w - NA_KW)
    col_ok = (kc >= c0) & (kc < c0 + NA_KW)
    dc = np.clip(kc - qc + NA_KW - 1, 0, 2 * NA_KW - 2)
    onehot = (np.arange(2 * NA_KW - 1)[:, None, None] == dc[None]).astype(np.float32)
    e1 = jnp.einsum('hrd,dqk->hrqk', rpb.astype(F32), jnp.asarray(onehot), precision=lax.Precision.HIGHEST)
    e1 = jnp.where(jnp.asarray(col_ok)[None, None], e1, NEG_BIG)
    e1 = jnp.concatenate([e1, jnp.full((NA_HEADS, 1, w, w), NEG_BIG, F32)], axis=1)
    idx = np.zeros((3, NA_QROWS, NA_WROWS), np.int32)
    for vi, (r_first, ks) in enumerate([(0, 0), (NA_QROWS, 0), (rows - NA_QROWS, rows - NA_WROWS)]):
        for a in range(NA_QROWS):
            qr = r_first + a
            r0 = min(max(qr - NA_KH // 2, 0), rows - NA_KH)
            for b in range(NA_WROWS):
                kr = ks + b
                idx[vi, a, b] = kr - qr + NA_KH - 1 if r0 <= kr < r0 + NA_KH else 2 * NA_KH - 1
    tiles = jnp.take(e1, jnp.asarray(idx.reshape(-1)), axis=1)
    tiles = tiles.reshape(NA_HEADS, 3, NA_QROWS, NA_WROWS, w, w)
    return jnp.transpose(tiles, (1, 0, 2, 4, 3, 5)).reshape(3, NA_HEADS, NA_QROWS * w, NA_WROWS * w)


def _na_kernel(q_ref, kv_k_ref, kv_v_ref, bias_ref, o_ref, sw_ref, sc_ref, m_ref, *, rows, n_groups):
    g = pl.program_id(1)
    nq = NA_QROWS * GRID_W
    nw = NA_WROWS * GRID_W
    lane = lax.broadcasted_iota(jnp.int32, (nq, LANES), 1)

    def window(grp):
        ks = jnp.clip(grp * NA_QROWS - NA_KH // 2, 0, rows - NA_WROWS)
        return pl.ds(pl.multiple_of(CTX_LEN + ks * GRID_W, GRID_W), nw)

    @pl.when(g == 0)
    def _():
        sw_ref[1] = jnp.zeros(sw_ref.shape[1:], F32)
        sc_ref[1] = jnp.zeros(sc_ref.shape[1:], F32)
        m_ref[1] = jnp.zeros(m_ref.shape[1:], F32)

    def step(slot, prev):
        q = q_ref[...]
        k_w = kv_k_ref[window(jnp.minimum(g, n_groups - 1)), :]
        k_c = kv_k_ref[0:CTX_LEN, :]
        for half in range(2):
            msk = (lane < NA_DH) if half == 0 else (lane >= NA_DH)
            q_h = jnp.where(msk, q, jnp.zeros_like(q))
            s_w = _dot_nt(q_h, k_w) + bias_ref[0, half]
            s_c = _dot_nt(q_h, k_c)
            sw_ref[slot, half] = s_w
            sc_ref[slot, half] = s_c
            m = jnp.maximum(jnp.max(s_w, axis=-1, keepdims=True), jnp.max(s_c, axis=-1, keepdims=True))
            m_ref[slot, half] = jnp.broadcast_to(m, (nq, LANES))
        v_w = kv_v_ref[window(jnp.maximum(g - 1, 0)), :]
        v_c = kv_v_ref[0:CTX_LEN, :]
        outs = []
        for half in range(2):
            m = m_ref[prev, half]
            p_w = jnp.exp(sw_ref[prev, half] - jnp.concatenate([m] * (nw // LANES), axis=1))
            p_c = jnp.exp(sc_ref[prev, half] - jnp.concatenate([m] * (CTX_LEN // LANES), axis=1))
            l = jnp.sum(p_w, axis=-1, keepdims=True) + jnp.sum(p_c, axis=-1, keepdims=True)
            o = _dot(p_w.astype(BF16), v_w) + _dot(p_c.astype(BF16), v_c)
            outs.append(o * (1.0 / l))
        o_ref[...] = jnp.where(lane < NA_DH, outs[0], outs[1]).astype(o_ref.dtype)

    @pl.when(g % 2 == 0)
    def _():
        step(0, 1)

    @pl.when(g % 2 == 1)
    def _():
        step(1, 0)


def _na(q, k, v, bias, rows):
    t, d = q.shape
    s = t - CTX_LEN
    nq = NA_QROWS * GRID_W
    nw = NA_WROWS * GRID_W
    assert nq == ROW_TILE and rows >= NA_WROWS and rows % NA_QROWS == 0
    n_groups = rows // NA_QROWS
    cur = lambda g: jnp.minimum(g, n_groups - 1)
    variant = lambda g: jnp.where(g == 0, 0, jnp.where(g == n_groups - 1, 2, 1))
    return pl.pallas_call(
        functools.partial(_na_kernel, rows=rows, n_groups=n_groups),
        grid=(NA_HEADS // 2, n_groups + 1),
        in_specs=[pl.BlockSpec((nq, LANES), lambda p, g: (cur(g) + 1, p)),
                  pl.BlockSpec((t, LANES), lambda p, g: (0, p)),
                  pl.BlockSpec((t, LANES), lambda p, g: (0, p)),
                  pl.BlockSpec((1, 2, nq, nw), lambda p, g: (variant(cur(g)), p, 0, 0))],
        out_specs=pl.BlockSpec((nq, LANES), lambda p, g: (jnp.maximum(g - 1, 0), p)),
        out_shape=jax.ShapeDtypeStruct((s, d), BF16),
        scratch_shapes=[pltpu.VMEM((2, 2, nq, nw), F32), pltpu.VMEM((2, 2, nq, CTX_LEN), F32),
                        pltpu.VMEM((2, 2, nq, LANES), F32)],
        compiler_params=_cparams(("arbitrary", "arbitrary")),
        name="na",
    )(q, k, v, bias)


def _attn_out_kernel(*refs, gla):
    if gla:
        (x_ref, of_ref, ob_ref, r_ref, om_ref, on_ref, w_ref, g1_ref, gate_ref, sc_ref, sh_ref,
         rwh_ref, rwl_ref, rb_ref, xo_ref, h_ref, lg_ref) = refs
        o = of_ref[...] + ob_ref[...]
        r = r_ref[...]
        parts = []
        for hd in range(GLA_HEADS):
            sl = slice(hd * GLA_DV, (hd + 1) * GLA_DV)
            rh = r[:, sl]
            parts.append(_rms(o[:, sl], on_ref[...]) * (rh * (1.0 / (1.0 + jnp.exp(-rh)))))
        mix_a = jnp.concatenate(parts, axis=1).astype(BF16)
        y = _dot(mix_a, w_ref[0:GLA_VW, :]) + _dot(om_ref[...], w_ref[GLA_VW:, :])
    else:
        (x_ref, om_ref, w_ref, g1_ref, gate_ref, sc_ref, sh_ref,
         rwh_ref, rwl_ref, rb_ref, xo_ref, h_ref, lg_ref) = refs
        y = _dot(om_ref[...], w_ref[...])
    x = x_ref[...] + _rms(y, g1_ref[...]) * gate_ref[0]
    xo_ref[...] = x
    h = x * lax.rsqrt(jnp.mean(x * x, axis=-1, keepdims=True) + EPS) * sc_ref[0] + sh_ref[0]
    h_ref[...] = h
    h_hi = h.astype(BF16)
    h_lo = (h - h_hi.astype(F32)).astype(BF16)
    lg_ref[...] = _dot(h_hi, rwh_ref[...]) + _dot(h_lo, rwh_ref[...]) + _dot(h_hi, rwl_ref[...]) + rb_ref[...]


def _attn_out(xs, mix_inputs, w_out, g1, gate, scale, shift, rw_hi, rw_lo, rb, *, gla, first_tile):
    d = xs.shape[1]
    tm = ROW_TILE
    t = mix_inputs[0].shape[0]
    row = lambda i: (i, 0)
    xrow = lambda i: (i + first_tile, 0)
    const = lambda i: (0, 0)
    strm = lambda i: (_stream(i + first_tile), 0, 0)
    if gla:
        of, ob, r, om, onorm = mix_inputs
        mix_specs = [pl.BlockSpec((tm, GLA_VW), row), pl.BlockSpec((tm, GLA_VW), row), pl.BlockSpec((tm, GLA_VW), row),
                     pl.BlockSpec((tm, MLA_VW), row), pl.BlockSpec(onorm.shape, const)]
    else:
        mix_specs = [pl.BlockSpec((tm, d), row)]
    return pl.pallas_call(
        functools.partial(_attn_out_kernel, gla=gla),
        grid=(t // tm,),
        in_specs=[pl.BlockSpec((tm, d), xrow)] + mix_specs + [
            pl.BlockSpec(w_out.shape, const), pl.BlockSpec(g1.shape, const),
            pl.BlockSpec((1, 1, d), strm), pl.BlockSpec((1, 1, d), strm), pl.BlockSpec((1, 1, d), strm),
            pl.BlockSpec(rw_hi.shape, const), pl.BlockSpec(rw_lo.shape, const), pl.BlockSpec(rb.shape, const)],
        out_specs=[pl.BlockSpec((tm, d), row), pl.BlockSpec((tm, d), row), pl.BlockSpec((tm, LANES), row)],
        out_shape=[jax.ShapeDtypeStruct((t, d), F32), jax.ShapeDtypeStruct((t, d), F32),
                   jax.ShapeDtypeStruct((t, LANES), F32)],
        compiler_params=_cparams(("arbitrary",)),
        name="attn_out",
    )(xs, *mix_inputs, w_out, g1, gate, scale, shift, rw_hi, rw_lo, rb)


def _moe_kernel(be_ref, tok_cur_ref, tok_nxt_ref, hp_ref, gate_ref, w1_ref, b1_ref, w2_ref, b2_ref, o_ref,
                xa_ref, xb_ref, sem_ref):
    i = pl.program_id(0)
    last = pl.num_programs(0) - 1
    bm = o_ref.shape[0]
    f = w2_ref.shape[2]

    def row_copy(tok_ref, r, buf, sem):
        return pltpu.make_async_copy(hp_ref.at[pl.ds(tok_ref[0, 0, r], 1), :], buf.at[pl.ds(r, 1), :], sem)

    def wait_block(buf, sem):
        pltpu.make_async_copy(hp_ref.at[pl.ds(0, bm), :], buf, sem).wait()

    @pl.when(i == 0)
    def _():
        for r in range(bm):
            row_copy(tok_cur_ref, r, xa_ref, sem_ref.at[0]).start()

    def step(cur, sem_cur, nxt, sem_nxt):
        wait_block(cur, sem_cur)
        x = cur[...].astype(BF16)
        n_grp = f // LANES
        per = bm // n_grp
        acts = []
        for c in range(n_grp):
            u = _dot(x, w1_ref[0, 0, :, 2 * c * LANES:(2 * c + 2) * LANES]) \
                + b1_ref[0, 0, :, 2 * c * LANES:(2 * c + 2) * LANES]
            glu = jnp.minimum(u[:, :LANES], SWIGLU_LIMIT)
            lin = jnp.clip(u[:, LANES:], -SWIGLU_LIMIT, SWIGLU_LIMIT)
            acts.append((glu * (1.0 / (1.0 + jnp.exp(-SWIGLU_ALPHA * glu))) * (lin + 1.0)).astype(BF16))
            for r in range(c * per, (c + 1) * per):
                row_copy(tok_nxt_ref, r, nxt, sem_nxt).start()
        y = _dot(jnp.concatenate(acts, axis=1), w2_ref[0, 0]) + b2_ref[0, 0]
        o_ref[...] = (y * gate_ref[...]).astype(o_ref.dtype)

        @pl.when(i == last)
        def _():
            wait_block(nxt, sem_nxt)

    @pl.when(i % 2 == 0)
    def _():
        step(xa_ref, sem_ref.at[0], xb_ref, sem_ref.at[1])

    @pl.when(i % 2 == 1)
    def _():
        step(xb_ref, sem_ref.at[1], xa_ref, sem_ref.at[0])


def _w1_prep_kernel(w_ref, p_ref, o_ref):
    w = w_ref[0, 0].astype(BF16)
    for c in range(w.shape[1] // (2 * LANES)):
        cols = slice(c * 2 * LANES, (c + 1) * 2 * LANES)
        o_ref[0, :, cols] = _dot(w[:, cols], p_ref[...]).astype(BF16)


def _w1_prep(w1_all, layer):
    _, e, d, f2 = w1_all.shape
    grp = 2 * LANES
    src = np.concatenate([np.arange(0, grp, 2), np.arange(1, grp, 2)])
    perm = np.zeros((grp, grp), np.float32)
    perm[src, np.arange(grp)] = 1.0
    tn = 1024
    return pl.pallas_call(
        _w1_prep_kernel,
        grid=(e, f2 // tn),
        in_specs=[pl.BlockSpec((1, 1, d, tn), lambda i, j: (layer, i, 0, j)),
                  pl.BlockSpec((grp, grp), lambda i, j: (0, 0))],
        out_specs=pl.BlockSpec((1, d, tn), lambda i, j: (i, 0, j)),
        out_shape=jax.ShapeDtypeStruct((e, d, f2), BF16),
        compiler_params=_cparams(("arbitrary", "arbitrary")),
        name="w1_prep",
    )(w1_all, jnp.asarray(perm, BF16))


def _moe_experts(block_e, slot_tok, hp, slot_gate, w1, b1, w2, b2, layer):
    n_blocks, _, bm = slot_tok.shape
    d = hp.shape[1]
    f = w2.shape[2]
    nxt = lambda i, be: (jnp.minimum(i + 1, n_blocks - 1), 0, 0)
    grid_spec = pltpu.PrefetchScalarGridSpec(
        num_scalar_prefetch=1,
        grid=(n_blocks,),
        in_specs=[pl.BlockSpec((1, 1, bm), lambda i, be: (i, 0, 0), memory_space=pltpu.SMEM),
                  pl.BlockSpec((1, 1, bm), nxt, memory_space=pltpu.SMEM),
                  pl.BlockSpec(memory_space=pl.ANY),
                  pl.BlockSpec((bm, 1), lambda i, be: (i, 0)),
                  pl.BlockSpec((1, 1, d, 2 * f), lambda i, be: (0, be[i], 0, 0)),
                  pl.BlockSpec((1, 1, 1, 2 * f), lambda i, be: (layer, be[i], 0, 0)),
                  pl.BlockSpec((1, 1, f, d), lambda i, be: (layer, be[i], 0, 0)),
                  pl.BlockSpec((1, 1, 1, d), lambda i, be: (layer, be[i], 0, 0))],
        out_specs=pl.BlockSpec((bm, d), lambda i, be: (i, 0)),
        scratch_shapes=[pltpu.VMEM((bm, d), F32), pltpu.VMEM((bm, d), F32), pltpu.SemaphoreType.DMA((2,))],
    )
    return pl.pallas_call(
        _moe_kernel,
        grid_spec=grid_spec,
        out_shape=jax.ShapeDtypeStruct((n_blocks * bm, d), BF16),
        compiler_params=_cparams(("arbitrary",)),
        name="moe_experts",
    )(block_e, slot_tok, slot_tok, hp, slot_gate, w1[None], b1, w2, b2)


def _moe(hp, logits, w1, b1, w2, b2, layer):
    t = hp.shape[0]
    bm = MOE_ROWS
    top_val, top_idx = lax.top_k(logits, TOP_K)
    gates = jax.nn.softmax(top_val, axis=-1)
    n_assign = t * TOP_K
    n_blocks = -(-n_assign // bm) + N_EXPERTS
    experts = jnp.arange(N_EXPERTS, dtype=jnp.int32)
    hit = jnp.any(top_idx[:, :, None] == experts, axis=1).astype(jnp.int32)
    counts = jnp.sum(hit, axis=0)
    before = jnp.cumsum(hit, axis=0) - hit
    padded = ((counts + bm - 1) // bm) * bm
    pad_end = jnp.cumsum(padded)
    pad_start = pad_end - padded
    start = jnp.cumsum(counts) - counts
    slot_of = jnp.take_along_axis(before + pad_start[None, :], top_idx, axis=1)
    block_first = jnp.arange(n_blocks, dtype=jnp.int32) * bm
    block_e = jnp.minimum(jnp.sum(block_first[:, None] >= pad_end[None, :], axis=1), N_EXPERTS - 1).astype(jnp.int32)
    order = jnp.argsort(top_idx.reshape(-1))
    local = (block_first - pad_start[block_e])[:, None] + jnp.arange(bm, dtype=jnp.int32)[None, :]
    valid = local < counts[block_e][:, None]
    assign = order[jnp.clip(start[block_e][:, None] + local, 0, n_assign - 1)]
    slot_tok = jnp.where(valid, assign // TOP_K, 0).astype(jnp.int32)
    slot_gate = jnp.where(valid, gates.reshape(-1)[assign], 0.0)
    yb = _moe_experts(block_e, slot_tok.reshape(n_blocks, 1, bm), hp, slot_gate.reshape(n_blocks * bm, 1),
                      w1, b1, w2, b2, layer)
    return yb[slot_of.T.reshape(-1)].reshape(TOP_K, t, yb.shape[1])


def _moe_weights(moe_w1, moe_b1, moe_w2, moe_b2):
    depth, e, d, f2 = moe_w1.shape
    b1p = jnp.transpose(moe_b1.reshape(depth, e, f2 // (2 * LANES), LANES, 2), (0, 1, 2, 4, 3))
    return ([_w1_prep(moe_w1, l) for l in range(depth)], b1p.reshape(depth, e, 1, f2), moe_w2.astype(BF16),
            moe_b2.reshape(depth, e, 1, moe_w2.shape[3]))


def _resid_kernel(*refs, project):
    if project:
        x_ref, y_ref, g3_ref, gate_ref, sc_ref, sh_ref, w_ref, xo_ref, q_ref, k_ref, v_ref = refs
    else:
        x_ref, y_ref, g3_ref, gate_ref, xo_ref = refs
    y = y_ref[0].astype(F32)
    for j in range(1, TOP_K):
        y = y + y_ref[j].astype(F32)
    x = x_ref[...] + _rms(y, g3_ref[...]) * gate_ref[0]
    xo_ref[...] = x
    if project:
        d = x.shape[1]
        h = x * lax.rsqrt(jnp.mean(x * x, axis=-1, keepdims=True) + EPS) * sc_ref[0] + sh_ref[0]
        u = _dot(h.astype(BF16), w_ref[...])
        q_ref[...] = u[:, 0:d].astype(BF16)
        k_ref[...] = u[:, d:2 * d].astype(BF16)
        v_ref[...] = u[:, 2 * d:3 * d].astype(BF16)


def _resid(xs, y, g3, gate, scale=None, shift=None, w=None, *, first_tile=0):
    t, d = xs.shape
    tm = ROW_TILE
    project = w is not None
    row = lambda i: (i, 0)
    const = lambda i: (0, 0)
    strm = lambda i: (_stream(i + first_tile), 0, 0)
    vec = pl.BlockSpec((1, 1, d), strm)
    in_specs = [pl.BlockSpec((tm, d), row), pl.BlockSpec((TOP_K, tm, d), lambda i: (0, i, 0)),
                pl.BlockSpec(g3.shape, const), vec]
    out_specs = [pl.BlockSpec((tm, d), row)]
    out_shape = [jax.ShapeDtypeStruct((t, d), F32)]
    args = [xs, y, g3, gate]
    if project:
        in_specs += [vec, vec, pl.BlockSpec(w.shape, const)]
        out_specs += [pl.BlockSpec((tm, d), row)] * 3
        out_shape += [jax.ShapeDtypeStruct((t, d), BF16)] * 3
        args += [scale, shift, w]
    return pl.pallas_call(
        functools.partial(_resid_kernel, project=project),
        grid=(t // tm,),
        in_specs=in_specs, out_specs=out_specs, out_shape=out_shape,
        compiler_params=_cparams(("arbitrary",)),
        name="resid",
    )(*args)


def _rope_tables(s):
    tok = jnp.arange(s)
    row = (tok // GRID_W).astype(F32)
    col = (tok % GRID_W).astype(F32)
    n_freq = MLA_ROPE // 4
    inv = ROPE_THETA ** (-jnp.arange(n_freq, dtype=F32) / n_freq)
    ang = jnp.concatenate([row[:, None] * inv, col[:, None] * inv], axis=-1)
    cos, sin = jnp.cos(ang), jnp.sin(ang)
    cos_t = jnp.concatenate([cos, cos, cos, cos], axis=-1)
    sin_t = jnp.concatenate([-sin, sin, -sin, sin], axis=-1)
    cos_t = jnp.concatenate([jnp.ones((CTX_LEN, LANES), F32), cos_t], axis=0)
    sin_t = jnp.concatenate([jnp.zeros((CTX_LEN, LANES), F32), sin_t], axis=0)
    return cos_t, sin_t


def _pad_cols(w, n):
    return jnp.pad(w, ((0, 0), (0, n - w.shape[1])))


def _swap_halves(w):
    half = w.shape[-1] // 2
    return jnp.concatenate([w[..., half:], w[..., :half]], axis=-1)


def _proj0_weights(in_w, wa_f, ba_f, wa_b, ba_b, wuq, wukv):
    o = np.cumsum((0, GLA_QK, GLA_QK, GLA_VW, GLA_VW, GLA_GATE_RANK, GLA_GATE_RANK, MLA_Q_LORA, MLA_KV_LORA,
                   MLA_ROPE))
    q, k, v, r, a_f, a_b, cq, ckv, kr = [in_w[:, o[i]:o[i + 1]] for i in range(9)]
    w_in = jnp.concatenate([q * (GLA_DK ** -0.5), k, v, r, cq, ckv, _pad_cols(kr, LANES),
                            _pad_cols(_swap_halves(kr), LANES),
                            _pad_cols(jnp.concatenate([a_f, a_b], axis=1), LANES)], axis=1).astype(BF16)
    wa = jnp.zeros((LANES, 2 * GLA_QK), F32)
    wa = wa.at[0:GLA_GATE_RANK, 0:GLA_QK].set(wa_f).at[GLA_GATE_RANK:2 * GLA_GATE_RANK, GLA_QK:].set(wa_b)
    ba = jnp.concatenate([ba_f, ba_b]).reshape(1, 2 * GLA_QK)
    wq = wuq.reshape(MLA_Q_LORA, MLA_HEADS, MLA_NOPE + MLA_ROPE)
    nope = wq[:, :, :MLA_NOPE].reshape(MLA_Q_LORA, MLA_HEADS * MLA_NOPE)
    rope = wq[:, :, MLA_NOPE:]
    pad3 = lambda z: jnp.pad(z, ((0, 0), (0, 0), (0, LANES - MLA_ROPE))).reshape(MLA_Q_LORA, MLA_HEADS * LANES)
    wuq_x = jnp.concatenate([nope, pad3(rope), pad3(_swap_halves(rope))], axis=1).astype(BF16)
    wkv = wukv.reshape(MLA_KV_LORA, MLA_HEADS, MLA_NOPE + MLA_V)
    wukv_x = jnp.concatenate([wkv[:, :, :MLA_NOPE].reshape(MLA_KV_LORA, -1),
                              wkv[:, :, MLA_NOPE:].reshape(MLA_KV_LORA, -1)], axis=1).astype(BF16)
    return w_in, wa.astype(BF16), ba, wuq_x, wukv_x


def _router_weights(rw, rb):
    rw = _pad_cols(rw, LANES)
    hi = rw.astype(BF16)
    lo = (rw - hi.astype(F32)).astype(BF16)
    return hi, lo, _pad_cols(rb.reshape(1, -1), LANES)


def kernel(x, c, ctx, c_ctx, ada_w, ada_b, norm_g, router_w, router_b, moe_w1, moe_b1, moe_w2, moe_b2, ab_in_w,
           gla_wa_f, gla_ba_f, gla_wa_b, gla_ba_b, gla_onorm, mla_qnorm, mla_wuq, mla_kvnorm, mla_wukv, ab_out_w,
           na_qkv_w, na_rpb, na_out_w):
    b, s, d = x.shape
    assert b == 1 and d == D_MODEL and ctx.shape == (1, CTX_LEN, d) and s % (NA_QROWS * GRID_W) == 0
    rows = s // GRID_W
    xs = jnp.concatenate([ctx[0], x[0]], axis=0)

    cc = jnp.zeros((8, d), F32).at[0].set(c[0]).at[1].set(c_ctx)
    mod = _ada(cc, ada_w, ada_b)

    def vectors(l):
        parts = jnp.split(mod[l, 0:2][::-1], 6, axis=-1)
        sh1, sc1, g1, sh2, sc2, g2 = [p.reshape(2, 1, d) for p in parts]
        return ((1 + sc1) * norm_g[l, 0], sh1, g1, (1 + sc2) * norm_g[l, 2], sh2, g2)

    sc1, sh1, g1, sc2, sh2, g2 = vectors(0)
    cos_t, sin_t = _rope_tables(s)
    w_in, wa, ba, wuq_x, wukv_x = _proj0_weights(ab_in_w[0], gla_wa_f[0], gla_ba_f[0], gla_wa_b[0], gla_ba_b[0],
                                                 mla_wuq[0], mla_wukv[0])
    qg, kg, vg, r, gf, gb, qm, km, vm = _proj0(xs, sc1, sh1, cos_t, sin_t, w_in, wa, ba,
                                               mla_qnorm[0].reshape(1, -1), wuq_x,
                                               mla_kvnorm[0].reshape(1, -1), wukv_x)
    o_f, o_b = _gla(qg, kg, vg, gf, gb)
    om_l = _flash(qm[:, CTX_LEN:], km, vm, FLASH_TQ, FLASH_TK)
    om_c = _flash(qm[:, :CTX_LEN], km[:, :CTX_LEN], vm[:, :CTX_LEN], CTX_LEN, CTX_LEN)
    om = jnp.concatenate([om_c, om_l], axis=0)
    rw_hi, rw_lo, rb = _router_weights(router_w[0], router_b[0])
    xs, h2, logits = _attn_out(xs, (o_f, o_b, r, om, gla_onorm[0].reshape(1, -1)), ab_out_w[0].astype(BF16),
                               norm_g[0, 1].reshape(1, -1), g1, sc2, sh2, rw_hi, rw_lo, rb, gla=True, first_tile=0)
    w1p, b1p, w2b, b2r = _moe_weights(moe_w1, moe_b1, moe_w2, moe_b2)
    y = _moe(h2, logits[:, :N_EXPERTS], w1p[0], b1p, w2b, b2r, 0)

    sc1n, sh1n, g1n, sc2n, sh2n, g2n = vectors(1)
    qkv_w = jnp.concatenate([na_qkv_w[0][:, :d] * (NA_DH ** -0.5), na_qkv_w[0][:, d:]], axis=1).astype(BF16)
    xs, q, k, v = _resid(xs, y, norm_g[0, 3].reshape(1, -1), g2, sc1n, sh1n, qkv_w)
    o_na = _na(q, k, v, _na_bias_tiles(na_rpb[0], rows), rows)
    rw_hi, rw_lo, rb = _router_weights(router_w[1], router_b[1])
    x_l, h2, logits = _attn_out(xs, (o_na,), na_out_w[0].astype(BF16), norm_g[1, 1].reshape(1, -1), g1n, sc2n, sh2n,
                                rw_hi, rw_lo, rb, gla=False, first_tile=1)
    y = _moe(h2, logits[:, :N_EXPERTS], w1p[1], b1p, w2b, b2r, 1)
    (x_l,) = _resid(x_l, y, norm_g[1, 3].reshape(1, -1), g2n, first_tile=1)
    return x_l.reshape(1, s, d)
```

```python
import functools

import numpy as np
import jax
import jax.numpy as jnp
from jax import lax
from jax.experimental import pallas as pl
from jax.experimental.pallas import tpu as pltpu

F32 = jnp.float32
BF16 = jnp.bfloat16

D_MODEL = 1024
DEPTH = 2
CTX_LEN = 256
GRID_W = 64
EPS = 1e-6
ROPE_THETA = 10000.0

GLA_HEADS = 4
GLA_DK = 64
GLA_DV = 128
GLA_GATE_RANK = 16
GLA_TAU = 16.0
GLA_QK = GLA_HEADS * GLA_DK
GLA_VW = GLA_HEADS * GLA_DV

MLA_HEADS = 4
MLA_Q_LORA = 256
MLA_KV_LORA = 128
MLA_NOPE = 128
MLA_ROPE = 64
MLA_V = 128
MLA_VW = MLA_HEADS * MLA_V
MLA_QK_PAD = 256

NA_HEADS = 16
NA_DH = D_MODEL // NA_HEADS
NA_KH = 8
NA_KW = 16
NA_QROWS = 4
NA_WROWS = NA_QROWS + NA_KH

N_EXPERTS = 32
TOP_K = 4
SWIGLU_LIMIT = 7.0
SWIGLU_ALPHA = 1.702

LANES = 128
ROW_TILE = 256
MOE_ROWS = 256
VMEM_LIMIT = 56 * 1024 * 1024
NEG_BIG = -1e30

GLA_CHUNK = 64
GLA_SUB = 16
FLASH_TQ = 512
FLASH_TK = 3328
FLASH_SUB = 256


def _cparams(sem):
    return pltpu.CompilerParams(dimension_semantics=sem, vmem_limit_bytes=VMEM_LIMIT)


def _dot(a, b):
    return jnp.dot(a, b, preferred_element_type=F32)


def _dot_nt(a, b):
    return lax.dot_general(a, b, (((1,), (1,)), ((), ())), preferred_element_type=F32)


def _rms(x, g):
    return x * lax.rsqrt(jnp.mean(x * x, axis=-1, keepdims=True) + EPS) * g


def _stream(i):
    return jnp.where(i == 0, 0, 1)


def _ada_kernel(c_ref, w_ref, b_ref, o_ref):
    c = c_ref[...]
    s = (c * (1.0 / (1.0 + jnp.exp(-c)))).astype(BF16)
    o_ref[0] = _dot(s, w_ref[0].astype(BF16)) + b_ref[0]


def _ada(cc, ada_w, ada_b):
    depth, d, n = ada_w.shape
    tn = 1536
    return pl.pallas_call(
        _ada_kernel,
        grid=(depth, n // tn),
        in_specs=[pl.BlockSpec((8, d), lambda l, j: (0, 0)),
                  pl.BlockSpec((1, d, tn), lambda l, j: (l, 0, j)),
                  pl.BlockSpec((1, 1, tn), lambda l, j: (l, 0, j))],
        out_specs=pl.BlockSpec((1, 8, tn), lambda l, j: (l, 0, j)),
        out_shape=jax.ShapeDtypeStruct((depth, 8, n), F32),
        compiler_params=_cparams(("arbitrary", "arbitrary")),
        name="ada",
    )(cc, ada_w, ada_b.reshape(depth, 1, n))


def _proj0_kernel(x_ref, sc_ref, sh_ref, cos_ref, sin_ref, win_ref, wa_ref, ba_ref, qn_ref, wuq_ref,
                  kvn_ref, wukv_ref,
                  qg_ref, kg_ref, vg_ref, r_ref, gf_ref, gb_ref, qm_ref, km_ref, vm_ref, *, mla_scale):
    x = x_ref[...]
    h = x * lax.rsqrt(jnp.mean(x * x, axis=-1, keepdims=True) + EPS) * sc_ref[0] + sh_ref[0]
    u = _dot(h.astype(BF16), win_ref[...])
    qg_ref[...] = u[:, 0:256]
    kg_ref[...] = u[:, 256:512]
    vg_ref[...] = u[:, 512:1024].astype(BF16)
    r_ref[...] = u[:, 1024:1536]
    cq = u[:, 1536:1792]
    ckv = u[:, 1792:1920]
    kr = u[:, 1920:2048]
    krs = u[:, 2048:2176]
    a = u[:, 2176:2304]

    ga = _dot(a.astype(BF16), wa_ref[...]) + ba_ref[...]
    la = (jnp.minimum(ga, 0.0) - jnp.log1p(jnp.exp(-jnp.abs(ga)))) * (1.0 / GLA_TAU)
    gf_ref[...] = la[:, :GLA_QK]
    gb_ref[...] = la[:, GLA_QK:]

    cos = cos_ref[...]
    sin = sin_ref[...]
    qm = _dot(_rms(cq, qn_ref[...]).astype(BF16), wuq_ref[...])
    kvm = _dot(_rms(ckv, kvn_ref[...]).astype(BF16), wukv_ref[...])
    kroped = (kr * cos + krs * sin).astype(BF16)
    for hd in range(MLA_HEADS):
        nope = qm[:, hd * 128:(hd + 1) * 128]
        rp = qm[:, 512 + hd * 128:512 + (hd + 1) * 128]
        rps = qm[:, 1024 + hd * 128:1024 + (hd + 1) * 128]
        roped = rp * cos + rps * sin
        qm_ref[hd] = (jnp.concatenate([nope, roped], axis=1) * mla_scale).astype(BF16)
        km_ref[hd] = jnp.concatenate([kvm[:, hd * 128:(hd + 1) * 128].astype(BF16), kroped], axis=1)
        v_h = kvm[:, 512 + hd * 128:512 + (hd + 1) * 128].astype(BF16)
        vm_ref[hd] = jnp.concatenate([v_h, jnp.ones_like(v_h)], axis=1)


def _proj0(xs, scale, shift, cos_t, sin_t, w_in, wa, ba, qn, wuq, kvn, wukv):
    t, d = xs.shape
    tm = ROW_TILE
    row = lambda i: (i, 0)
    const = lambda i: (0, 0)
    strm = lambda i: (_stream(i), 0, 0)
    hrow = lambda i: (0, i, 0)
    kern = functools.partial(_proj0_kernel, mla_scale=float((MLA_NOPE + MLA_ROPE) ** -0.5 * np.log2(np.e)))
    return pl.pallas_call(
        kern,
        grid=(t // tm,),
        in_specs=[pl.BlockSpec((tm, d), row),
                  pl.BlockSpec((1, 1, d), strm), pl.BlockSpec((1, 1, d), strm),
                  pl.BlockSpec((tm, LANES), row), pl.BlockSpec((tm, LANES), row),
                  pl.BlockSpec(w_in.shape, const), pl.BlockSpec(wa.shape, const), pl.BlockSpec(ba.shape, const),
                  pl.BlockSpec(qn.shape, const), pl.BlockSpec(wuq.shape, const),
                  pl.BlockSpec(kvn.shape, const), pl.BlockSpec(wukv.shape, const)],
        out_specs=[pl.BlockSpec((tm, GLA_QK), row), pl.BlockSpec((tm, GLA_QK), row),
                   pl.BlockSpec((tm, GLA_VW), row), pl.BlockSpec((tm, GLA_VW), row),
                   pl.BlockSpec((tm, GLA_QK), row), pl.BlockSpec((tm, GLA_QK), row),
                   pl.BlockSpec((MLA_HEADS, tm, MLA_QK_PAD), hrow),
                   pl.BlockSpec((MLA_HEADS, tm, MLA_QK_PAD), hrow),
                   pl.BlockSpec((MLA_HEADS, tm, 2 * MLA_V), hrow)],
        out_shape=[jax.ShapeDtypeStruct((t, GLA_QK), F32), jax.ShapeDtypeStruct((t, GLA_QK), F32),
                   jax.ShapeDtypeStruct((t, GLA_VW), BF16), jax.ShapeDtypeStruct((t, GLA_VW), F32),
                   jax.ShapeDtypeStruct((t, GLA_QK), F32), jax.ShapeDtypeStruct((t, GLA_QK), F32),
                   jax.ShapeDtypeStruct((MLA_HEADS, t, MLA_QK_PAD), BF16),
                   jax.ShapeDtypeStruct((MLA_HEADS, t, MLA_QK_PAD), BF16),
                   jax.ShapeDtypeStruct((MLA_HEADS, t, 2 * MLA_V), BF16)],
        compiler_params=_cparams(("arbitrary",)),
        name="proj0",
    )(xs, scale, shift, cos_t, sin_t, w_in, wa, ba, qn, wuq, kvn, wukv)


def _gla_tables():
    c, s = GLA_CHUNK, GLA_SUB
    t = np.arange(c)[:, None]
    u = np.arange(c)[None, :]
    bt, bu = t // s, u // s
    half = (c // s) // 2
    mats = [(bu == bt) & (u <= t), (bu == bt) & (u > t), bu < bt, bu > bt,
            (bu >= half) & (bu < bt), (bu > bt) & (bu < half)]
    m = np.concatenate([x.astype(np.float32) for x in mats], axis=0)
    sel = np.stack([(bt == bu) & (u <= t), bt == bu + 1, (bt >= half) & (bu < half)]).astype(np.float32)
    m_b = np.concatenate([x.astype(np.float32)[::-1, ::-1] for x in mats], axis=0)
    sel_b = sel[:, ::-1, ::-1]
    return np.stack([m, m_b]), np.stack([sel, sel_b])


def _gla_chunk(q, k, g, v_ref, vt_ref, j, m, sel, st_ref, o_ref):
    rows = pl.ds(j * GLA_CHUNK, GLA_CHUNK)
    c = GLA_CHUNK
    g_hi = g.astype(BF16)
    g_r1 = g - g_hi.astype(F32)
    g_mid = g_r1.astype(BF16)
    g_lo = (g_r1 - g_mid.astype(F32)).astype(BF16)
    cs = _dot(m, g_hi) + _dot(m, g_mid) + _dot(m, g_lo)
    own, rest, before, after = cs[0:c], cs[c:2 * c], cs[2 * c:3 * c], cs[3 * c:4 * c]
    far_q, far_k = cs[4 * c:5 * c], cs[5 * c:6 * c]
    qd = q * jnp.exp(own)
    kd = k * jnp.exp(-own)
    kh = k * jnp.exp(rest)
    q_in = qd * jnp.exp(before)
    k_out = kh * jnp.exp(after)
    q_far = qd * jnp.exp(far_q)
    k_far = kh * jnp.exp(far_k)
    decay = jnp.exp(own + rest + before + after)
    lane = lax.broadcasted_iota(jnp.int32, (c, LANES), 1)
    s_diag, s_adj, s_far = sel[0] > 0.5, sel[1] > 0.5, sel[2] > 0.5
    for p in range(GLA_HEADS // 2):
        lo, hi = p * LANES, (p + 1) * LANES
        kd_p = kd[:, lo:hi].astype(BF16)
        kh_p = kh[:, lo:hi].astype(BF16)
        kf_p = k_far[:, lo:hi].astype(BF16)
        st = st_ref[p]
        st_b = st.astype(BF16)
        upd = st * decay[0:1, lo:hi]
        for half in range(2):
            hd = 2 * p + half
            msk = (lane < GLA_DK) if half == 0 else (lane >= GLA_DK)
            pick = lambda z: jnp.where(msk, z[:, lo:hi], 0.0).astype(BF16)
            qd_h = pick(qd)
            a = jnp.where(s_diag, _dot_nt(qd_h, kd_p),
                          jnp.where(s_adj, _dot_nt(qd_h, kh_p),
                                    jnp.where(s_far, _dot_nt(pick(q_far), kf_p), 0.0)))
            v_h = v_ref[rows, hd * GLA_DV:(hd + 1) * GLA_DV]
            o_ref[rows, hd * GLA_DV:(hd + 1) * GLA_DV] = _dot(a.astype(BF16), v_h) + _dot_nt(pick(q_in), st_b)
            upd = upd + _dot(vt_ref[j, hd * GLA_DV:(hd + 1) * GLA_DV, :], pick(k_out))
        st_ref[p] = upd


def _gla_kernel(qf_ref, kf_ref, gf_ref, vf_ref, vtf_ref, qb_ref, kb_ref, gb_ref, vb_ref, vtb_ref, m_ref, sel_ref,
                of_ref, ob_ref, stf_ref, stb_ref):
    @pl.when(pl.program_id(0) == 0)
    def _():
        stf_ref[...] = jnp.zeros_like(stf_ref)
        stb_ref[...] = jnp.zeros_like(stb_ref)

    n = ROW_TILE // GLA_CHUNK
    for j in range(n):
        rows = pl.ds(j * GLA_CHUNK, GLA_CHUNK)
        _gla_chunk(qf_ref[rows, :], kf_ref[rows, :], gf_ref[rows, :], vf_ref, vtf_ref, j,
                   m_ref[0], sel_ref[0], stf_ref, of_ref)
        jb = n - 1 - j
        rows = pl.ds(jb * GLA_CHUNK, GLA_CHUNK)
        _gla_chunk(qb_ref[rows, :], kb_ref[rows, :], gb_ref[rows, :], vb_ref, vtb_ref, jb,
                   m_ref[1], sel_ref[1], stb_ref, ob_ref)


def _gla(q, k, v, gf, gb):
    t = q.shape[0]
    tm = ROW_TILE
    nb = t // tm
    n = tm // GLA_CHUNK
    m_np, sel_np = _gla_tables()
    m = jnp.asarray(m_np, BF16)
    sel = jnp.asarray(sel_np, F32)
    vt = jnp.transpose(v.reshape(t // GLA_CHUNK, GLA_CHUNK, GLA_VW), (0, 2, 1))
    fwd = lambda i: (i, 0)
    bwd = lambda i: (jnp.where(i == 0, 0, nb - i), 0)
    fwd_t = lambda i: (i, 0, 0)
    bwd_t = lambda i: (jnp.where(i == 0, 0, nb - i), 0, 0)
    c3 = lambda i: (0, 0, 0)
    c4 = lambda i: (0, 0, 0, 0)
    spec = lambda w, im: pl.BlockSpec((tm, w), im)
    return pl.pallas_call(
        _gla_kernel,
        grid=(nb,),
        in_specs=[spec(GLA_QK, fwd), spec(GLA_QK, fwd), spec(GLA_QK, fwd), spec(GLA_VW, fwd),
                  pl.BlockSpec((n, GLA_VW, GLA_CHUNK), fwd_t),
                  spec(GLA_QK, bwd), spec(GLA_QK, bwd), spec(GLA_QK, bwd), spec(GLA_VW, bwd),
                  pl.BlockSpec((n, GLA_VW, GLA_CHUNK), bwd_t),
                  pl.BlockSpec(m.shape, c3), pl.BlockSpec(sel.shape, c4)],
        out_specs=[spec(GLA_VW, fwd), spec(GLA_VW, bwd)],
        out_shape=[jax.ShapeDtypeStruct((t, GLA_VW), F32), jax.ShapeDtypeStruct((t, GLA_VW), F32)],
        scratch_shapes=[pltpu.VMEM((GLA_HEADS // 2, GLA_DV, LANES), F32),
                        pltpu.VMEM((GLA_HEADS // 2, GLA_DV, LANES), F32)],
        compiler_params=_cparams(("arbitrary",)),
        name="gla",
    )(q, k, gf, v, vt, q, k, gb, v, vt, m, sel)


def _flash_kernel(q_ref, k_ref, v_ref, o_ref, s_ref, mc_ref, m_ref, acc_ref, *, tk, n_chunks):
    m_ref[...] = jnp.full_like(m_ref, -jnp.inf)
    acc_ref[...] = jnp.zeros_like(acc_ref)
    q = q_ref[0]
    tq = q.shape[0]
    sub = min(tk, FLASH_SUB)
    dv = o_ref.shape[1]

    def scores(j, slot):
        off = pl.multiple_of(j * tk, tk)
        s = _dot_nt(q, k_ref[0, pl.ds(off, tk), :])
        s_ref[slot] = s
        mc_ref[slot] = jnp.broadcast_to(jnp.max(s, axis=-1, keepdims=True), (tq, LANES))

    def softmax_pv(j, slot):
        off = pl.multiple_of(j * tk, tk)
        m_old = m_ref[...]
        m_new = jnp.maximum(m_old, mc_ref[slot])
        alpha = jnp.exp2(m_old - m_new)
        m_ref[...] = m_new
        acc = jnp.concatenate([alpha, alpha], axis=1) * acc_ref[...]
        for c in range(tk // sub):
            ps = [jnp.exp2(s_ref[slot, :, c * sub + a * LANES:c * sub + (a + 1) * LANES] - m_new).astype(BF16)
                  for a in range(sub // LANES)]
            acc = acc + _dot(jnp.concatenate(ps, axis=1), v_ref[0, pl.ds(off + c * sub, sub), :])
        acc_ref[...] = acc

    scores(0, 0)

    def pair(i, carry):
        scores(2 * i + 1, 1)
        softmax_pv(2 * i, 0)
        scores(2 * i + 2, 0)
        softmax_pv(2 * i + 1, 1)
        return carry

    n_pairs = (n_chunks - 1) // 2
    lax.fori_loop(0, n_pairs, pair, 0)
    if n_chunks % 2 == 0:
        scores(n_chunks - 1, 1)
        softmax_pv(n_chunks - 2, 0)
        softmax_pv(n_chunks - 1, 1)
    else:
        softmax_pv(n_chunks - 1, 0)
    o_ref[...] = (acc_ref[:, :dv] * (1.0 / acc_ref[:, dv:])).astype(o_ref.dtype)


def _flash(q, k, v, tq, tk):
    h, s, dq = q.shape
    t, dv = k.shape[1], v.shape[2] // 2
    assert s % tq == 0 and t % tk == 0
    kern = functools.partial(_flash_kernel, tk=tk, n_chunks=t // tk)
    return pl.pallas_call(
        kern,
        grid=(h, s // tq),
        in_specs=[pl.BlockSpec((1, tq, dq), lambda hh, i: (hh, i, 0)),
                  pl.BlockSpec((1, t, dq), lambda hh, i: (hh, 0, 0)),
                  pl.BlockSpec((1, t, 2 * dv), lambda hh, i: (hh, 0, 0))],
        out_specs=pl.BlockSpec((tq, dv), lambda hh, i: (i, hh)),
        out_shape=jax.ShapeDtypeStruct((s, h * dv), BF16),
        scratch_shapes=[pltpu.VMEM((2, tq, tk), F32), pltpu.VMEM((2, tq, LANES), F32),
                        pltpu.VMEM((tq, LANES), F32), pltpu.VMEM((tq, 2 * dv), F32)],
        compiler_params=_cparams(("arbitrary", "arbitrary")),
        name="flash",
    )(q, k, v)


def _na_bias_tiles(rpb, rows):
    w = GRID_W
    qc = np.arange(w)[:, None]
    kc = np.arange(w)[None, :]
    c0 = np.clip(qc - NA_KW // 2, 0, w - NA_KW)
    col_ok = (kc >= c0) & (kc < c0 + NA_KW)
    dc = np.clip(kc - qc + NA_KW - 1, 0, 2 * NA_KW - 2)
    onehot = (np.arange(2 * NA_KW - 1)[:, None, None] == dc[None]).astype(np.float32)
    e1 = jnp.einsum('hrd,dqk->hrqk', rpb.astype(F32), jnp.asarray(onehot), precision=lax.Precision.HIGHEST)
    e1 = jnp.where(jnp.asarray(col_ok)[None, None], e1, NEG_BIG)
    e1 = jnp.concatenate([e1, jnp.full((NA_HEADS, 1, w, w), NEG_BIG, F32)], axis=1)
    idx = np.zeros((3, NA_QROWS, NA_WROWS), np.int32)
    for vi, (r_first, ks) in enumerate([(0, 0), (NA_QROWS, 0), (rows - NA_QROWS, rows - NA_WROWS)]):
        for a in range(NA_QROWS):
            qr = r_first + a
            r0 = min(max(qr - NA_KH // 2, 0), rows - NA_KH)
            for b in range(NA_WROWS):
                kr = ks + b
                idx[vi, a, b] = kr - qr + NA_KH - 1 if r0 <= kr < r0 + NA_KH else 2 * NA_KH - 1
    tiles = jnp.take(e1, jnp.asarray(idx.reshape(-1)), axis=1)
    tiles = tiles.reshape(NA_HEADS, 3, NA_QROWS, NA_WROWS, w, w)
    return jnp.transpose(tiles, (1, 0, 2, 4, 3, 5)).reshape(3, NA_HEADS, NA_QROWS * w, NA_WROWS * w)


def _na_kernel(q_ref, kv_k_ref, kv_v_ref, bias_ref, o_ref, sw_ref, sc_ref, m_ref, *, rows, n_groups):
    g = pl.program_id(1)
    nq = NA_QROWS * GRID_W
    nw = NA_WROWS * GRID_W
    lane = lax.broadcasted_iota(jnp.int32, (nq, LANES), 1)

    def window(grp):
        ks = jnp.clip(grp * NA_QROWS - NA_KH // 2, 0, rows - NA_WROWS)
        return pl.ds(pl.multiple_of(CTX_LEN + ks * GRID_W, GRID_W), nw)

    @pl.when(g == 0)
    def _():
        sw_ref[1] = jnp.zeros(sw_ref.shape[1:], F32)
        sc_ref[1] = jnp.zeros(sc_ref.shape[1:], F32)
        m_ref[1] = jnp.zeros(m_ref.shape[1:], F32)

    def step(slot, prev):
        q = q_ref[...]
        k_w = kv_k_ref[window(jnp.minimum(g, n_groups - 1)), :]
        k_c = kv_k_ref[0:CTX_LEN, :]
        for half in range(2):
            msk = (lane < NA_DH) if half == 0 else (lane >= NA_DH)
            q_h = jnp.where(msk, q, jnp.zeros_like(q))
            s_w = _dot_nt(q_h, k_w) + bias_ref[0, half]
            s_c = _dot_nt(q_h, k_c)
            sw_ref[slot, half] = s_w
            sc_ref[slot, half] = s_c
            m = jnp.maximum(jnp.max(s_w, axis=-1, keepdims=True), jnp.max(s_c, axis=-1, keepdims=True))
            m_ref[slot, half] = jnp.broadcast_to(m, (nq, LANES))
        v_w = kv_v_ref[window(jnp.maximum(g - 1, 0)), :]
        v_c = kv_v_ref[0:CTX_LEN, :]
        outs = []
        for half in range(2):
            m = m_ref[prev, half]
            p_w = jnp.exp(sw_ref[prev, half] - jnp.concatenate([m] * (nw // LANES), axis=1))
            p_c = jnp.exp(sc_ref[prev, half] - jnp.concatenate([m] * (CTX_LEN // LANES), axis=1))
            l = jnp.sum(p_w, axis=-1, keepdims=True) + jnp.sum(p_c, axis=-1, keepdims=True)
            o = _dot(p_w.astype(BF16), v_w) + _dot(p_c.astype(BF16), v_c)
            outs.append(o * (1.0 / l))
        o_ref[...] = jnp.where(lane < NA_DH, outs[0], outs[1]).astype(o_ref.dtype)

    @pl.when(g % 2 == 0)
    def _():
        step(0, 1)

    @pl.when(g % 2 == 1)
    def _():
        step(1, 0)


def _na(q, k, v, bias, rows):
    t, d = q.shape
    s = t - CTX_LEN
    nq = NA_QROWS * GRID_W
    nw = NA_WROWS * GRID_W
    assert nq == ROW_TILE and rows >= NA_WROWS and rows % NA_QROWS == 0
    n_groups = rows // NA_QROWS
    cur = lambda g: jnp.minimum(g, n_groups - 1)
    variant = lambda g: jnp.where(g == 0, 0, jnp.where(g == n_groups - 1, 2, 1))
    return pl.pallas_call(
        functools.partial(_na_kernel, rows=rows, n_groups=n_groups),
        grid=(NA_HEADS // 2, n_groups + 1),
        in_specs=[pl.BlockSpec((nq, LANES), lambda p, g: (cur(g) + 1, p)),
                  pl.BlockSpec((t, LANES), lambda p, g: (0, p)),
                  pl.BlockSpec((t, LANES), lambda p, g: (0, p)),
                  pl.BlockSpec((1, 2, nq, nw), lambda p, g: (variant(cur(g)), p, 0, 0))],
        out_specs=pl.BlockSpec((nq, LANES), lambda p, g: (jnp.maximum(g - 1, 0), p)),
        out_shape=jax.ShapeDtypeStruct((s, d), BF16),
        scratch_shapes=[pltpu.VMEM((2, 2, nq, nw), F32), pltpu.VMEM((2, 2, nq, CTX_LEN), F32),
                        pltpu.VMEM((2, 2, nq, LANES), F32)],
        compiler_params=_cparams(("arbitrary", "arbitrary")),
        name="na",
    )(q, k, v, bias)


def _attn_out_kernel(*refs, gla):
    if gla:
        (x_ref, of_ref, ob_ref, r_ref, om_ref, on_ref, w_ref, g1_ref, gate_ref, sc_ref, sh_ref,
         rwh_ref, rwl_ref, rb_ref, xo_ref, h_ref, lg_ref) = refs
        o = of_ref[...] + ob_ref[...]
        r = r_ref[...]
        parts = []
        for hd in range(GLA_HEADS):
            sl = slice(hd * GLA_DV, (hd + 1) * GLA_DV)
            rh = r[:, sl]
            parts.append(_rms(o[:, sl], on_ref[...]) * (rh * (1.0 / (1.0 + jnp.exp(-rh)))))
        mix_a = jnp.concatenate(parts, axis=1).astype(BF16)
        y = _dot(mix_a, w_ref[0:GLA_VW, :]) + _dot(om_ref[...], w_ref[GLA_VW:, :])
    else:
        (x_ref, om_ref, w_ref, g1_ref, gate_ref, sc_ref, sh_ref,
         rwh_ref, rwl_ref, rb_ref, xo_ref, h_ref, lg_ref) = refs
        y = _dot(om_ref[...], w_ref[...])
    x = x_ref[...] + _rms(y, g1_ref[...]) * gate_ref[0]
    xo_ref[...] = x
    h = x * lax.rsqrt(jnp.mean(x * x, axis=-1, keepdims=True) + EPS) * sc_ref[0] + sh_ref[0]
    for c in range(h.shape[1] // LANES):
        h_ref[:, c, :] = h[:, c * LANES:(c + 1) * LANES]
    h_hi = h.astype(BF16)
    h_lo = (h - h_hi.astype(F32)).astype(BF16)
    lg_ref[...] = _dot(h_hi, rwh_ref[...]) + _dot(h_lo, rwh_ref[...]) + _dot(h_hi, rwl_ref[...]) + rb_ref[...]


def _attn_out(xs, mix_inputs, w_out, g1, gate, scale, shift, rw_hi, rw_lo, rb, *, gla, first_tile):
    d = xs.shape[1]
    tm = ROW_TILE
    t = mix_inputs[0].shape[0]
    row = lambda i: (i, 0)
    xrow = lambda i: (i + first_tile, 0)
    const = lambda i: (0, 0)
    strm = lambda i: (_stream(i + first_tile), 0, 0)
    if gla:
        of, ob, r, om, onorm = mix_inputs
        mix_specs = [pl.BlockSpec((tm, GLA_VW), row), pl.BlockSpec((tm, GLA_VW), row), pl.BlockSpec((tm, GLA_VW), row),
                     pl.BlockSpec((tm, MLA_VW), row), pl.BlockSpec(onorm.shape, const)]
    else:
        mix_specs = [pl.BlockSpec((tm, d), row)]
    return pl.pallas_call(
        functools.partial(_attn_out_kernel, gla=gla),
        grid=(t // tm,),
        in_specs=[pl.BlockSpec((tm, d), xrow)] + mix_specs + [
            pl.BlockSpec(w_out.shape, const), pl.BlockSpec(g1.shape, const),
            pl.BlockSpec((1, 1, d), strm), pl.BlockSpec((1, 1, d), strm), pl.BlockSpec((1, 1, d), strm),
            pl.BlockSpec(rw_hi.shape, const), pl.BlockSpec(rw_lo.shape, const), pl.BlockSpec(rb.shape, const)],
        out_specs=[pl.BlockSpec((tm, d), row), pl.BlockSpec((tm, d // LANES, LANES), lambda i: (i, 0, 0)),
                   pl.BlockSpec((tm, LANES), row)],
        out_shape=[jax.ShapeDtypeStruct((t, d), F32), jax.ShapeDtypeStruct((t, d // LANES, LANES), F32),
                   jax.ShapeDtypeStruct((t, LANES), F32)],
        compiler_params=_cparams(("arbitrary",)),
        name="attn_out",
    )(xs, *mix_inputs, w_out, g1, gate, scale, shift, rw_hi, rw_lo, rb)


def _moe_kernel(be_ref, tok_cur_ref, tok_nxt_ref, hp_ref, gate_ref, w1_ref, p_ref, b1_ref, w2_ref, b2_ref, o_ref,
                xa_ref, xb_ref, w1b_ref, w2b_ref, sem_ref):
    i = pl.program_id(0)
    last = pl.num_programs(0) - 1
    bm = o_ref.shape[0]
    f = w2_ref.shape[2]

    @pl.when((i == 0) | (be_ref[i] != be_ref[jnp.maximum(i - 1, 0)]))
    def _():
        for c in range(2 * f // (2 * LANES)):
            cols = slice(c * 2 * LANES, (c + 1) * 2 * LANES)
            w1b_ref[:, cols] = _dot(w1_ref[0, 0, :, cols].astype(BF16), p_ref[...]).astype(BF16)
        w2b_ref[...] = w2_ref[0, 0].astype(BF16)

    def row_copy(tok_ref, r, buf, sem):
        return pltpu.make_async_copy(hp_ref.at[pl.ds(tok_ref[0, 0, r], 1)], buf.at[pl.ds(r, 1)], sem)

    def wait_block(buf, sem):
        pltpu.make_async_copy(hp_ref.at[pl.ds(0, bm)], buf, sem).wait()

    @pl.when(i == 0)
    def _():
        for r in range(bm):
            row_copy(tok_cur_ref, r, xa_ref, sem_ref.at[0]).start()

    def step(cur, sem_cur, nxt, sem_nxt):
        wait_block(cur, sem_cur)
        x = jnp.concatenate([cur[:, c, :] for c in range(cur.shape[1])], axis=1).astype(BF16)
        n_grp = f // LANES
        per = bm // n_grp
        acts = []
        for c in range(n_grp):
            u = _dot(x, w1b_ref[:, 2 * c * LANES:(2 * c + 2) * LANES]) \
                + b1_ref[0, 0, :, 2 * c * LANES:(2 * c + 2) * LANES]
            glu = jnp.minimum(u[:, :LANES], SWIGLU_LIMIT)
            lin = jnp.clip(u[:, LANES:], -SWIGLU_LIMIT, SWIGLU_LIMIT)
            acts.append((glu * (1.0 / (1.0 + jnp.exp(-SWIGLU_ALPHA * glu))) * (lin + 1.0)).astype(BF16))
            for r in range(c * per, (c + 1) * per):
                row_copy(tok_nxt_ref, r, nxt, sem_nxt).start()
        y = _dot(jnp.concatenate(acts, axis=1), w2b_ref[...]) + b2_ref[0, 0]
        o_ref[...] = (y * gate_ref[...]).astype(o_ref.dtype)

        @pl.when(i == last)
        def _():
            wait_block(nxt, sem_nxt)

    @pl.when(i % 2 == 0)
    def _():
        step(xa_ref, sem_ref.at[0], xb_ref, sem_ref.at[1])

    @pl.when(i % 2 == 1)
    def _():
        step(xb_ref, sem_ref.at[1], xa_ref, sem_ref.at[0])


def _regroup_matrix():
    grp = 2 * LANES
    src = np.concatenate([np.arange(0, grp, 2), np.arange(1, grp, 2)])
    perm = np.zeros((grp, grp), np.float32)
    perm[src, np.arange(grp)] = 1.0
    return jnp.asarray(perm, BF16)


def _moe_experts(block_e, slot_tok, hp, slot_gate, w1, b1, w2, b2, layer):
    n_blocks, _, bm = slot_tok.shape
    row_shape = hp.shape[1:]
    d = row_shape[0] * row_shape[1]
    f = w2.shape[2]
    perm = _regroup_matrix()
    nxt = lambda i, be: (jnp.minimum(i + 1, n_blocks - 1), 0, 0)
    grid_spec = pltpu.PrefetchScalarGridSpec(
        num_scalar_prefetch=1,
        grid=(n_blocks,),
        in_specs=[pl.BlockSpec((1, 1, bm), lambda i, be: (i, 0, 0), memory_space=pltpu.SMEM),
                  pl.BlockSpec((1, 1, bm), nxt, memory_space=pltpu.SMEM),
                  pl.BlockSpec(memory_space=pl.ANY),
                  pl.BlockSpec((bm, 1), lambda i, be: (i, 0)),
                  pl.BlockSpec((1, 1, d, 2 * f), lambda i, be: (layer, be[i], 0, 0)),
                  pl.BlockSpec(perm.shape, lambda i, be: (0, 0)),
                  pl.BlockSpec((1, 1, 1, 2 * f), lambda i, be: (layer, be[i], 0, 0)),
                  pl.BlockSpec((1, 1, f, d), lambda i, be: (layer, be[i], 0, 0)),
                  pl.BlockSpec((1, 1, 1, d), lambda i, be: (layer, be[i], 0, 0))],
        out_specs=pl.BlockSpec((bm, d), lambda i, be: (i, 0)),
        scratch_shapes=[pltpu.VMEM((bm,) + row_shape, F32), pltpu.VMEM((bm,) + row_shape, F32),
                        pltpu.VMEM((d, 2 * f), BF16), pltpu.VMEM((f, d), BF16), pltpu.SemaphoreType.DMA((2,))],
    )
    return pl.pallas_call(
        _moe_kernel,
        grid_spec=grid_spec,
        out_shape=jax.ShapeDtypeStruct((n_blocks * bm, d), BF16),
        compiler_params=_cparams(("arbitrary",)),
        name="moe_experts",
    )(block_e, slot_tok, slot_tok, hp, slot_gate, w1, perm, b1, w2, b2)


def _moe(hp, logits, w1, b1, w2, b2, layer):
    t = hp.shape[0]
    bm = MOE_ROWS
    top_val, top_idx = lax.top_k(logits, TOP_K)
    gates = jax.nn.softmax(top_val, axis=-1)
    n_assign = t * TOP_K
    n_blocks = -(-n_assign // bm) + N_EXPERTS
    experts = jnp.arange(N_EXPERTS, dtype=jnp.int32)
    hit = jnp.any(top_idx[:, :, None] == experts, axis=1).astype(jnp.int32)
    counts = jnp.sum(hit, axis=0)
    before = jnp.cumsum(hit, axis=0) - hit
    padded = ((counts + bm - 1) // bm) * bm
    pad_end = jnp.cumsum(padded)
    pad_start = pad_end - padded
    start = jnp.cumsum(counts) - counts
    slot_of = jnp.take_along_axis(before + pad_start[None, :], top_idx, axis=1)
    block_first = jnp.arange(n_blocks, dtype=jnp.int32) * bm
    block_e = jnp.minimum(jnp.sum(block_first[:, None] >= pad_end[None, :], axis=1), N_EXPERTS - 1).astype(jnp.int32)
    order = jnp.argsort(top_idx.reshape(-1))
    local = (block_first - pad_start[block_e])[:, None] + jnp.arange(bm, dtype=jnp.int32)[None, :]
    valid = local < counts[block_e][:, None]
    assign = order[jnp.clip(start[block_e][:, None] + local, 0, n_assign - 1)]
    slot_tok = jnp.where(valid, assign // TOP_K, 0).astype(jnp.int32)
    slot_gate = jnp.where(valid, gates.reshape(-1)[assign], 0.0)
    yb = _moe_experts(block_e, slot_tok.reshape(n_blocks, 1, bm), hp, slot_gate.reshape(n_blocks * bm, 1),
                      w1, b1, w2, b2, layer)
    return yb[slot_of.T.reshape(-1)].reshape(TOP_K, t, yb.shape[1])


def _moe_biases(moe_b1, moe_b2):
    depth, e, f2 = moe_b1.shape
    b1p = jnp.transpose(moe_b1.reshape(depth, e, f2 // (2 * LANES), LANES, 2), (0, 1, 2, 4, 3))
    return b1p.reshape(depth, e, 1, f2), moe_b2.reshape(depth, e, 1, moe_b2.shape[2])


def _resid_kernel(*refs, project):
    if project:
        x_ref, y_ref, g3_ref, gate_ref, sc_ref, sh_ref, w_ref, xo_ref, q_ref, k_ref, v_ref = refs
    else:
        x_ref, y_ref, g3_ref, gate_ref, xo_ref = refs
    y = y_ref[0].astype(F32)
    for j in range(1, TOP_K):
        y = y + y_ref[j].astype(F32)
    x = x_ref[...] + _rms(y, g3_ref[...]) * gate_ref[0]
    xo_ref[...] = x
    if project:
        d = x.shape[1]
        h = x * lax.rsqrt(jnp.mean(x * x, axis=-1, keepdims=True) + EPS) * sc_ref[0] + sh_ref[0]
        u = _dot(h.astype(BF16), w_ref[...])
        q_ref[...] = u[:, 0:d].astype(BF16)
        k_ref[...] = u[:, d:2 * d].astype(BF16)
        v_ref[...] = u[:, 2 * d:3 * d].astype(BF16)


def _resid(xs, y, g3, gate, scale=None, shift=None, w=None, *, first_tile=0):
    t, d = xs.shape
    tm = ROW_TILE
    project = w is not None
    row = lambda i: (i, 0)
    const = lambda i: (0, 0)
    strm = lambda i: (_stream(i + first_tile), 0, 0)
    vec = pl.BlockSpec((1, 1, d), strm)
    in_specs = [pl.BlockSpec((tm, d), row), pl.BlockSpec((TOP_K, tm, d), lambda i: (0, i, 0)),
                pl.BlockSpec(g3.shape, const), vec]
    out_specs = [pl.BlockSpec((tm, d), row)]
    out_shape = [jax.ShapeDtypeStruct((t, d), F32)]
    args = [xs, y, g3, gate]
    if project:
        in_specs += [vec, vec, pl.BlockSpec(w.shape, const)]
        out_specs += [pl.BlockSpec((tm, d), row)] * 3
        out_shape += [jax.ShapeDtypeStruct((t, d), BF16)] * 3
        args += [scale, shift, w]
    return pl.pallas_call(
        functools.partial(_resid_kernel, project=project),
        grid=(t // tm,),
        in_specs=in_specs, out_specs=out_specs, out_shape=out_shape,
        compiler_params=_cparams(("arbitrary",)),
        name="resid",
    )(*args)


def _rope_tables(s):
    tok = jnp.arange(s)
    row = (tok // GRID_W).astype(F32)
    col = (tok % GRID_W).astype(F32)
    n_freq = MLA_ROPE // 4
    inv = ROPE_THETA ** (-jnp.arange(n_freq, dtype=F32) / n_freq)
    ang = jnp.concatenate([row[:, None] * inv, col[:, None] * inv], axis=-1)
    cos, sin = jnp.cos(ang), jnp.sin(ang)
    cos_t = jnp.concatenate([cos, cos, cos, cos], axis=-1)
    sin_t = jnp.concatenate([-sin, sin, -sin, sin], axis=-1)
    cos_t = jnp.concatenate([jnp.ones((CTX_LEN, LANES), F32), cos_t], axis=0)
    sin_t = jnp.concatenate([jnp.zeros((CTX_LEN, LANES), F32), sin_t], axis=0)
    return cos_t, sin_t


def _pad_cols(w, n):
    return jnp.pad(w, ((0, 0), (0, n - w.shape[1])))


def _swap_halves(w):
    half = w.shape[-1] // 2
    return jnp.concatenate([w[..., half:], w[..., :half]], axis=-1)


def _proj0_weights(in_w, wa_f, ba_f, wa_b, ba_b, wuq, wukv):
    o = np.cumsum((0, GLA_QK, GLA_QK, GLA_VW, GLA_VW, GLA_GATE_RANK, GLA_GATE_RANK, MLA_Q_LORA, MLA_KV_LORA,
                   MLA_ROPE))
    q, k, v, r, a_f, a_b, cq, ckv, kr = [in_w[:, o[i]:o[i + 1]] for i in range(9)]
    w_in = jnp.concatenate([q * (GLA_DK ** -0.5), k, v, r, cq, ckv, _pad_cols(kr, LANES),
                            _pad_cols(_swap_halves(kr), LANES),
                            _pad_cols(jnp.concatenate([a_f, a_b], axis=1), LANES)], axis=1).astype(BF16)
    wa = jnp.zeros((LANES, 2 * GLA_QK), F32)
    wa = wa.at[0:GLA_GATE_RANK, 0:GLA_QK].set(wa_f).at[GLA_GATE_RANK:2 * GLA_GATE_RANK, GLA_QK:].set(wa_b)
    ba = jnp.concatenate([ba_f, ba_b]).reshape(1, 2 * GLA_QK)
    wq = wuq.reshape(MLA_Q_LORA, MLA_HEADS, MLA_NOPE + MLA_ROPE)
    nope = wq[:, :, :MLA_NOPE].reshape(MLA_Q_LORA, MLA_HEADS * MLA_NOPE)
    rope = wq[:, :, MLA_NOPE:]
    pad3 = lambda z: jnp.pad(z, ((0, 0), (0, 0), (0, LANES - MLA_ROPE))).reshape(MLA_Q_LORA, MLA_HEADS * LANES)
    wuq_x = jnp.concatenate([nope, pad3(rope), pad3(_swap_halves(rope))], axis=1).astype(BF16)
    wkv = wukv.reshape(MLA_KV_LORA, MLA_HEADS, MLA_NOPE + MLA_V)
    wukv_x = jnp.concatenate([wkv[:, :, :MLA_NOPE].reshape(MLA_KV_LORA, -1),
                              wkv[:, :, MLA_NOPE:].reshape(MLA_KV_LORA, -1)], axis=1).astype(BF16)
    return w_in, wa.astype(BF16), ba, wuq_x, wukv_x


def _router_weights(rw, rb):
    rw = _pad_cols(rw, LANES)
    hi = rw.astype(BF16)
    lo = (rw - hi.astype(F32)).astype(BF16)
    return hi, lo, _pad_cols(rb.reshape(1, -1), LANES)


def kernel(x, c, ctx, c_ctx, ada_w, ada_b, norm_g, router_w, router_b, moe_w1, moe_b1, moe_w2, moe_b2, ab_in_w,
           gla_wa_f, gla_ba_f, gla_wa_b, gla_ba_b, gla_onorm, mla_qnorm, mla_wuq, mla_kvnorm, mla_wukv, ab_out_w,
           na_qkv_w, na_rpb, na_out_w):
    b, s, d = x.shape
    assert b == 1 and d == D_MODEL and ctx.shape == (1, CTX_LEN, d) and s % (NA_QROWS * GRID_W) == 0
    rows = s // GRID_W
    xs = jnp.concatenate([ctx[0], x[0]], axis=0)

    cc = jnp.zeros((8, d), F32).at[0].set(c[0]).at[1].set(c_ctx)
    mod = _ada(cc, ada_w, ada_b)

    def vectors(l):
        parts = jnp.split(mod[l, 0:2][::-1], 6, axis=-1)
        sh1, sc1, g1, sh2, sc2, g2 = [p.reshape(2, 1, d) for p in parts]
        return ((1 + sc1) * norm_g[l, 0], sh1, g1, (1 + sc2) * norm_g[l, 2], sh2, g2)

    sc1, sh1, g1, sc2, sh2, g2 = vectors(0)
    cos_t, sin_t = _rope_tables(s)
    w_in, wa, ba, wuq_x, wukv_x = _proj0_weights(ab_in_w[0], gla_wa_f[0], gla_ba_f[0], gla_wa_b[0], gla_ba_b[0],
                                                 mla_wuq[0], mla_wukv[0])
    qg, kg, vg, r, gf, gb, qm, km, vm = _proj0(xs, sc1, sh1, cos_t, sin_t, w_in, wa, ba,
                                               mla_qnorm[0].reshape(1, -1), wuq_x,
                                               mla_kvnorm[0].reshape(1, -1), wukv_x)
    o_f, o_b = _gla(qg, kg, vg, gf, gb)
    om_l = _flash(qm[:, CTX_LEN:], km, vm, FLASH_TQ, FLASH_TK)
    om_c = _flash(qm[:, :CTX_LEN], km[:, :CTX_LEN], vm[:, :CTX_LEN], CTX_LEN, CTX_LEN)
    om = jnp.concatenate([om_c, om_l], axis=0)
    rw_hi, rw_lo, rb = _router_weights(router_w[0], router_b[0])
    xs, h2, logits = _attn_out(xs, (o_f, o_b, r, om, gla_onorm[0].reshape(1, -1)), ab_out_w[0].astype(BF16),
                               norm_g[0, 1].reshape(1, -1), g1, sc2, sh2, rw_hi, rw_lo, rb, gla=True, first_tile=0)
    b1p, b2r = _moe_biases(moe_b1, moe_b2)
    y = _moe(h2, logits[:, :N_EXPERTS], moe_w1, b1p, moe_w2, b2r, 0)

    sc1n, sh1n, g1n, sc2n, sh2n, g2n = vectors(1)
    qkv_w = jnp.concatenate([na_qkv_w[0][:, :d] * (NA_DH ** -0.5), na_qkv_w[0][:, d:]], axis=1).astype(BF16)
    xs, q, k, v = _resid(xs, y, norm_g[0, 3].reshape(1, -1), g2, sc1n, sh1n, qkv_w)
    o_na = _na(q, k, v, _na_bias_tiles(na_rpb[0], rows), rows)
    rw_hi, rw_lo, rb = _router_weights(router_w[1], router_b[1])
    x_l, h2, logits = _attn_out(xs, (o_na,), na_out_w[0].astype(BF16), norm_g[1, 1].reshape(1, -1), g1n, sc2n, sh2n,
                                rw_hi, rw_lo, rb, gla=False, first_tile=1)
    y = _moe(h2, logits[:, :N_EXPERTS], moe_w1, b1p, moe_w2, b2r, 1)
    (x_l,) = _resid(x_l, y, norm_g[1, 3].reshape(1, -1), g2n, first_tile=1)
    return x_l.reshape(1, s, d)
```

```python
import functools

import numpy as np
import jax
import jax.numpy as jnp
from jax import lax
from jax.experimental import pallas as pl
from jax.experimental.pallas import tpu as pltpu

F32 = jnp.float32
BF16 = jnp.bfloat16

D_MODEL = 1024
DEPTH = 2
CTX_LEN = 256
GRID_W = 64
EPS = 1e-6
ROPE_THETA = 10000.0

GLA_HEADS = 4
GLA_DK = 64
GLA_DV = 128
GLA_GATE_RANK = 16
GLA_TAU = 16.0
GLA_QK = GLA_HEADS * GLA_DK
GLA_VW = GLA_HEADS * GLA_DV

MLA_HEADS = 4
MLA_Q_LORA = 256
MLA_KV_LORA = 128
MLA_NOPE = 128
MLA_ROPE = 64
MLA_V = 128
MLA_VW = MLA_HEADS * MLA_V
MLA_QK_PAD = 256

NA_HEADS = 16
NA_DH = D_MODEL // NA_HEADS
NA_KH = 8
NA_KW = 16
NA_QROWS = 4
NA_WROWS = NA_QROWS + NA_KH

N_EXPERTS = 32
TOP_K = 4
SWIGLU_LIMIT = 7.0
SWIGLU_ALPHA = 1.702

LANES = 128
ROW_TILE = 256
MOE_ROWS = 256
VMEM_LIMIT = 56 * 1024 * 1024
NEG_BIG = -1e30

GLA_CHUNK = 64
GLA_SUB = 16
FLASH_TQ = 512
FLASH_TK = 3328
FLASH_SUB = 256


def _cparams(sem):
    return pltpu.CompilerParams(dimension_semantics=sem, vmem_limit_bytes=VMEM_LIMIT)


def _dot(a, b):
    return jnp.dot(a, b, preferred_element_type=F32)


def _dot_nt(a, b):
    return lax.dot_general(a, b, (((1,), (1,)), ((), ())), preferred_element_type=F32)


def _rms(x, g):
    return x * lax.rsqrt(jnp.mean(x * x, axis=-1, keepdims=True) + EPS) * g


def _stream(i):
    return jnp.where(i == 0, 0, 1)


def _ada_kernel(c_ref, w_ref, b_ref, o_ref):
    c = c_ref[...]
    s = (c * (1.0 / (1.0 + jnp.exp(-c)))).astype(BF16)
    o_ref[0] = _dot(s, w_ref[0].astype(BF16)) + b_ref[0]


def _ada(cc, ada_w, ada_b):
    depth, d, n = ada_w.shape
    tn = 1536
    return pl.pallas_call(
        _ada_kernel,
        grid=(depth, n // tn),
        in_specs=[pl.BlockSpec((8, d), lambda l, j: (0, 0)),
                  pl.BlockSpec((1, d, tn), lambda l, j: (l, 0, j)),
                  pl.BlockSpec((1, 1, tn), lambda l, j: (l, 0, j))],
        out_specs=pl.BlockSpec((1, 8, tn), lambda l, j: (l, 0, j)),
        out_shape=jax.ShapeDtypeStruct((depth, 8, n), F32),
        compiler_params=_cparams(("arbitrary", "arbitrary")),
        name="ada",
    )(cc, ada_w, ada_b.reshape(depth, 1, n))


def _proj0_kernel(x_ref, sc_ref, sh_ref, cos_ref, sin_ref, win_ref, wa_ref, ba_ref, qn_ref, wuq_ref,
                  kvn_ref, wukv_ref,
                  qg_ref, kg_ref, vg_ref, r_ref, gf_ref, gb_ref, qm_ref, km_ref, vm_ref, *, mla_scale):
    x = x_ref[...]
    h = x * lax.rsqrt(jnp.mean(x * x, axis=-1, keepdims=True) + EPS) * sc_ref[0] + sh_ref[0]
    u = _dot(h.astype(BF16), win_ref[...])
    qg_ref[...] = u[:, 0:256]
    kg_ref[...] = u[:, 256:512]
    vg_ref[...] = u[:, 512:1024].astype(BF16)
    r_ref[...] = u[:, 1024:1536]
    cq = u[:, 1536:1792]
    ckv = u[:, 1792:1920]
    kr = u[:, 1920:2048]
    krs = u[:, 2048:2176]
    a = u[:, 2176:2304]

    ga = _dot(a.astype(BF16), wa_ref[...]) + ba_ref[...]
    la = (jnp.minimum(ga, 0.0) - jnp.log1p(jnp.exp(-jnp.abs(ga)))) * (1.0 / GLA_TAU)
    gf_ref[...] = la[:, :GLA_QK]
    gb_ref[...] = la[:, GLA_QK:]

    cos = cos_ref[...]
    sin = sin_ref[...]
    qm = _dot(_rms(cq, qn_ref[...]).astype(BF16), wuq_ref[...])
    kvm = _dot(_rms(ckv, kvn_ref[...]).astype(BF16), wukv_ref[...])
    kroped = (kr * cos + krs * sin).astype(BF16)
    for hd in range(MLA_HEADS):
        nope = qm[:, hd * 128:(hd + 1) * 128]
        rp = qm[:, 512 + hd * 128:512 + (hd + 1) * 128]
        rps = qm[:, 1024 + hd * 128:1024 + (hd + 1) * 128]
        roped = rp * cos + rps * sin
        qm_ref[hd] = (jnp.concatenate([nope, roped], axis=1) * mla_scale).astype(BF16)
        km_ref[hd] = jnp.concatenate([kvm[:, hd * 128:(hd + 1) * 128].astype(BF16), kroped], axis=1)
        v_h = kvm[:, 512 + hd * 128:512 + (hd + 1) * 128].astype(BF16)
        vm_ref[hd] = jnp.concatenate([v_h, jnp.ones_like(v_h)], axis=1)


def _proj0(xs, scale, shift, cos_t, sin_t, w_in, wa, ba, qn, wuq, kvn, wukv):
    t, d = xs.shape
    tm = ROW_TILE
    row = lambda i: (i, 0)
    const = lambda i: (0, 0)
    strm = lambda i: (_stream(i), 0, 0)
    hrow = lambda i: (0, i, 0)
    kern = functools.partial(_proj0_kernel, mla_scale=float((MLA_NOPE + MLA_ROPE) ** -0.5 * np.log2(np.e)))
    return pl.pallas_call(
        kern,
        grid=(t // tm,),
        in_specs=[pl.BlockSpec((tm, d), row),
                  pl.BlockSpec((1, 1, d), strm), pl.BlockSpec((1, 1, d), strm),
                  pl.BlockSpec((tm, LANES), row), pl.BlockSpec((tm, LANES), row),
                  pl.BlockSpec(w_in.shape, const), pl.BlockSpec(wa.shape, const), pl.BlockSpec(ba.shape, const),
                  pl.BlockSpec(qn.shape, const), pl.BlockSpec(wuq.shape, const),
                  pl.BlockSpec(kvn.shape, const), pl.BlockSpec(wukv.shape, const)],
        out_specs=[pl.BlockSpec((tm, GLA_QK), row), pl.BlockSpec((tm, GLA_QK), row),
                   pl.BlockSpec((tm, GLA_VW), row), pl.BlockSpec((tm, GLA_VW), row),
                   pl.BlockSpec((tm, GLA_QK), row), pl.BlockSpec((tm, GLA_QK), row),
                   pl.BlockSpec((MLA_HEADS, tm, MLA_QK_PAD), hrow),
                   pl.BlockSpec((MLA_HEADS, tm, MLA_QK_PAD), hrow),
                   pl.BlockSpec((MLA_HEADS, tm, 2 * MLA_V), hrow)],
        out_shape=[jax.ShapeDtypeStruct((t, GLA_QK), F32), jax.ShapeDtypeStruct((t, GLA_QK), F32),
                   jax.ShapeDtypeStruct((t, GLA_VW), BF16), jax.ShapeDtypeStruct((t, GLA_VW), F32),
                   jax.ShapeDtypeStruct((t, GLA_QK), F32), jax.ShapeDtypeStruct((t, GLA_QK), F32),
                   jax.ShapeDtypeStruct((MLA_HEADS, t, MLA_QK_PAD), BF16),
                   jax.ShapeDtypeStruct((MLA_HEADS, t, MLA_QK_PAD), BF16),
                   jax.ShapeDtypeStruct((MLA_HEADS, t, 2 * MLA_V), BF16)],
        compiler_params=_cparams(("arbitrary",)),
        name="proj0",
    )(xs, scale, shift, cos_t, sin_t, w_in, wa, ba, qn, wuq, kvn, wukv)


def _gla_tables():
    c, s = GLA_CHUNK, GLA_SUB
    t = np.arange(c)[:, None]
    u = np.arange(c)[None, :]
    bt, bu = t // s, u // s
    half = (c // s) // 2
    mats = [(bu == bt) & (u <= t), (bu == bt) & (u > t), bu < bt, bu > bt,
            (bu >= half) & (bu < bt), (bu > bt) & (bu < half)]
    m = np.concatenate([x.astype(np.float32) for x in mats], axis=0)
    sel = np.stack([(bt == bu) & (u <= t), bt == bu + 1, (bt >= half) & (bu < half)]).astype(np.float32)
    m_b = np.concatenate([x.astype(np.float32)[::-1, ::-1] for x in mats], axis=0)
    sel_b = sel[:, ::-1, ::-1]
    return np.stack([m, m_b]), np.stack([sel, sel_b])


def _gla_chunk(q, k, g, v_ref, vt_ref, j, m, sel, st_ref, o_ref):
    rows = pl.ds(j * GLA_CHUNK, GLA_CHUNK)
    c = GLA_CHUNK
    g_hi = g.astype(BF16)
    g_r1 = g - g_hi.astype(F32)
    g_mid = g_r1.astype(BF16)
    g_lo = (g_r1 - g_mid.astype(F32)).astype(BF16)
    cs = _dot(m, g_hi) + _dot(m, g_mid) + _dot(m, g_lo)
    own, rest, before, after = cs[0:c], cs[c:2 * c], cs[2 * c:3 * c], cs[3 * c:4 * c]
    far_q, far_k = cs[4 * c:5 * c], cs[5 * c:6 * c]
    qd = q * jnp.exp(own)
    kd = k * jnp.exp(-own)
    kh = k * jnp.exp(rest)
    q_in = qd * jnp.exp(before)
    k_out = kh * jnp.exp(after)
    q_far = qd * jnp.exp(far_q)
    k_far = kh * jnp.exp(far_k)
    decay = jnp.exp(own + rest + before + after)
    lane = lax.broadcasted_iota(jnp.int32, (c, LANES), 1)
    s_diag, s_adj, s_far = sel[0] > 0.5, sel[1] > 0.5, sel[2] > 0.5
    for p in range(GLA_HEADS // 2):
        lo, hi = p * LANES, (p + 1) * LANES
        kd_p = kd[:, lo:hi].astype(BF16)
        kh_p = kh[:, lo:hi].astype(BF16)
        kf_p = k_far[:, lo:hi].astype(BF16)
        st = st_ref[p]
        st_b = st.astype(BF16)
        upd = st * decay[0:1, lo:hi]
        for half in range(2):
            hd = 2 * p + half
            msk = (lane < GLA_DK) if half == 0 else (lane >= GLA_DK)
            pick = lambda z: jnp.where(msk, z[:, lo:hi], 0.0).astype(BF16)
            qd_h = pick(qd)
            a = jnp.where(s_diag, _dot_nt(qd_h, kd_p),
                          jnp.where(s_adj, _dot_nt(qd_h, kh_p),
                                    jnp.where(s_far, _dot_nt(pick(q_far), kf_p), 0.0)))
            v_h = v_ref[rows, hd * GLA_DV:(hd + 1) * GLA_DV]
            o_ref[rows, hd * GLA_DV:(hd + 1) * GLA_DV] = _dot(a.astype(BF16), v_h) + _dot_nt(pick(q_in), st_b)
            upd = upd + _dot(vt_ref[j, hd * GLA_DV:(hd + 1) * GLA_DV, :], pick(k_out))
        st_ref[p] = upd


def _gla_kernel(qf_ref, kf_ref, gf_ref, vf_ref, vtf_ref, qb_ref, kb_ref, gb_ref, vb_ref, vtb_ref, m_ref, sel_ref,
                of_ref, ob_ref, stf_ref, stb_ref):
    @pl.when(pl.program_id(0) == 0)
    def _():
        stf_ref[...] = jnp.zeros_like(stf_ref)
        stb_ref[...] = jnp.zeros_like(stb_ref)

    n = ROW_TILE // GLA_CHUNK
    for j in range(n):
        rows = pl.ds(j * GLA_CHUNK, GLA_CHUNK)
        _gla_chunk(qf_ref[rows, :], kf_ref[rows, :], gf_ref[rows, :], vf_ref, vtf_ref, j,
                   m_ref[0], sel_ref[0], stf_ref, of_ref)
        jb = n - 1 - j
        rows = pl.ds(jb * GLA_CHUNK, GLA_CHUNK)
        _gla_chunk(qb_ref[rows, :], kb_ref[rows, :], gb_ref[rows, :], vb_ref, vtb_ref, jb,
                   m_ref[1], sel_ref[1], stb_ref, ob_ref)


def _gla(q, k, v, gf, gb):
    t = q.shape[0]
    tm = ROW_TILE
    nb = t // tm
    n = tm // GLA_CHUNK
    m_np, sel_np = _gla_tables()
    m = jnp.asarray(m_np, BF16)
    sel = jnp.asarray(sel_np, F32)
    vt = jnp.transpose(v.reshape(t // GLA_CHUNK, GLA_CHUNK, GLA_VW), (0, 2, 1))
    fwd = lambda i: (i, 0)
    bwd = lambda i: (jnp.where(i == 0, 0, nb - i), 0)
    fwd_t = lambda i: (i, 0, 0)
    bwd_t = lambda i: (jnp.where(i == 0, 0, nb - i), 0, 0)
    c3 = lambda i: (0, 0, 0)
    c4 = lambda i: (0, 0, 0, 0)
    spec = lambda w, im: pl.BlockSpec((tm, w), im)
    return pl.pallas_call(
        _gla_kernel,
        grid=(nb,),
        in_specs=[spec(GLA_QK, fwd), spec(GLA_QK, fwd), spec(GLA_QK, fwd), spec(GLA_VW, fwd),
                  pl.BlockSpec((n, GLA_VW, GLA_CHUNK), fwd_t),
                  spec(GLA_QK, bwd), spec(GLA_QK, bwd), spec(GLA_QK, bwd), spec(GLA_VW, bwd),
                  pl.BlockSpec((n, GLA_VW, GLA_CHUNK), bwd_t),
                  pl.BlockSpec(m.shape, c3), pl.BlockSpec(sel.shape, c4)],
        out_specs=[spec(GLA_VW, fwd), spec(GLA_VW, bwd)],
        out_shape=[jax.ShapeDtypeStruct((t, GLA_VW), F32), jax.ShapeDtypeStruct((t, GLA_VW), F32)],
        scratch_shapes=[pltpu.VMEM((GLA_HEADS // 2, GLA_DV, LANES), F32),
                        pltpu.VMEM((GLA_HEADS // 2, GLA_DV, LANES), F32)],
        compiler_params=_cparams(("arbitrary",)),
        name="gla",
    )(q, k, gf, v, vt, q, k, gb, v, vt, m, sel)


def _flash_kernel(q_ref, k_ref, v_ref, o_ref, s_ref, mc_ref, m_ref, acc_ref, *, tk, n_chunks):
    m_ref[...] = jnp.full_like(m_ref, -jnp.inf)
    acc_ref[...] = jnp.zeros_like(acc_ref)
    q = q_ref[0]
    tq = q.shape[0]
    sub = min(tk, FLASH_SUB)
    dv = o_ref.shape[1]

    def scores(j, slot):
        off = pl.multiple_of(j * tk, tk)
        s = _dot_nt(q, k_ref[0, pl.ds(off, tk), :])
        s_ref[slot] = s
        mc_ref[slot] = jnp.broadcast_to(jnp.max(s, axis=-1, keepdims=True), (tq, LANES))

    def softmax_pv(j, slot):
        off = pl.multiple_of(j * tk, tk)
        m_old = m_ref[...]
        m_new = jnp.maximum(m_old, mc_ref[slot])
        alpha = jnp.exp2(m_old - m_new)
        m_ref[...] = m_new
        acc = jnp.concatenate([alpha, alpha], axis=1) * acc_ref[...]
        for c in range(tk // sub):
            ps = [jnp.exp2(s_ref[slot, :, c * sub + a * LANES:c * sub + (a + 1) * LANES] - m_new).astype(BF16)
                  for a in range(sub // LANES)]
            acc = acc + _dot(jnp.concatenate(ps, axis=1), v_ref[0, pl.ds(off + c * sub, sub), :])
        acc_ref[...] = acc

    scores(0, 0)

    def pair(i, carry):
        scores(2 * i + 1, 1)
        softmax_pv(2 * i, 0)
        scores(2 * i + 2, 0)
        softmax_pv(2 * i + 1, 1)
        return carry

    n_pairs = (n_chunks - 1) // 2
    lax.fori_loop(0, n_pairs, pair, 0)
    if n_chunks % 2 == 0:
        scores(n_chunks - 1, 1)
        softmax_pv(n_chunks - 2, 0)
        softmax_pv(n_chunks - 1, 1)
    else:
        softmax_pv(n_chunks - 1, 0)
    o_ref[...] = (acc_ref[:, :dv] * (1.0 / acc_ref[:, dv:])).astype(o_ref.dtype)


def _flash(q, k, v, tq, tk):
    h, s, dq = q.shape
    t, dv = k.shape[1], v.shape[2] // 2
    assert s % tq == 0 and t % tk == 0
    kern = functools.partial(_flash_kernel, tk=tk, n_chunks=t // tk)
    return pl.pallas_call(
        kern,
        grid=(h, s // tq),
        in_specs=[pl.BlockSpec((1, tq, dq), lambda hh, i: (hh, i, 0)),
                  pl.BlockSpec((1, t, dq), lambda hh, i: (hh, 0, 0), pipeline_mode=pl.Buffered(1)),
                  pl.BlockSpec((1, t, 2 * dv), lambda hh, i: (hh, 0, 0), pipeline_mode=pl.Buffered(1))],
        out_specs=pl.BlockSpec((tq, dv), lambda hh, i: (i, hh)),
        out_shape=jax.ShapeDtypeStruct((s, h * dv), BF16),
        scratch_shapes=[pltpu.VMEM((2, tq, tk), F32), pltpu.VMEM((2, tq, LANES), F32),
                        pltpu.VMEM((tq, LANES), F32), pltpu.VMEM((tq, 2 * dv), F32)],
        compiler_params=_cparams(("arbitrary", "arbitrary")),
        name="flash",
    )(q, k, v)


def _na_bias_tiles(rpb, rows):
    w = GRID_W
    qc = np.arange(w)[:, None]
    kc = np.arange(w)[None, :]
    c0 = np.clip(qc - NA_KW // 2, 0, w - NA_KW)
    col_ok = (kc >= c0) & (kc < c0 + NA_KW)
    dc = np.clip(kc - qc + NA_KW - 1, 0, 2 * NA_KW - 2)
    onehot = (np.arange(2 * NA_KW - 1)[:, None, None] == dc[None]).astype(np.float32)
    e1 = jnp.einsum('hrd,dqk->hrqk', rpb.astype(F32), jnp.asarray(onehot), precision=lax.Precision.HIGHEST)
    e1 = jnp.where(jnp.asarray(col_ok)[None, None], e1, NEG_BIG)
    e1 = jnp.concatenate([e1, jnp.full((NA_HEADS, 1, w, w), NEG_BIG, F32)], axis=1)
    idx = np.zeros((3, NA_QROWS, NA_WROWS), np.int32)
    for vi, (r_first, ks) in enumerate([(0, 0), (NA_QROWS, 0), (rows - NA_QROWS, rows - NA_WROWS)]):
        for a in range(NA_QROWS):
            qr = r_first + a
            r0 = min(max(qr - NA_KH // 2, 0), rows - NA_KH)
            for b in range(NA_WROWS):
                kr = ks + b
                idx[vi, a, b] = kr - qr + NA_KH - 1 if r0 <= kr < r0 + NA_KH else 2 * NA_KH - 1
    tiles = jnp.take(e1, jnp.asarray(idx.reshape(-1)), axis=1)
    tiles = tiles.reshape(NA_HEADS, 3, NA_QROWS, NA_WROWS, w, w)
    return jnp.transpose(tiles, (1, 0, 2, 4, 3, 5)).reshape(3, NA_HEADS, NA_QROWS * w, NA_WROWS * w)


def _na_kernel(q_ref, kv_k_ref, kv_v_ref, bias_ref, o_ref, sw_ref, sc_ref, m_ref, *, rows, n_groups):
    g = pl.program_id(1)
    nq = NA_QROWS * GRID_W
    nw = NA_WROWS * GRID_W
    lane = lax.broadcasted_iota(jnp.int32, (nq, LANES), 1)

    def window(grp):
        ks = jnp.clip(grp * NA_QROWS - NA_KH // 2, 0, rows - NA_WROWS)
        return pl.ds(pl.multiple_of(CTX_LEN + ks * GRID_W, GRID_W), nw)

    @pl.when(g == 0)
    def _():
        sw_ref[1] = jnp.zeros(sw_ref.shape[1:], F32)
        sc_ref[1] = jnp.zeros(sc_ref.shape[1:], F32)
        m_ref[1] = jnp.zeros(m_ref.shape[1:], F32)

    def step(slot, prev):
        q = q_ref[...]
        k_w = kv_k_ref[window(jnp.minimum(g, n_groups - 1)), :]
        k_c = kv_k_ref[0:CTX_LEN, :]
        for half in range(2):
            msk = (lane < NA_DH) if half == 0 else (lane >= NA_DH)
            q_h = jnp.where(msk, q, jnp.zeros_like(q))
            s_w = _dot_nt(q_h, k_w) + bias_ref[0, half]
            s_c = _dot_nt(q_h, k_c)
            sw_ref[slot, half] = s_w
            sc_ref[slot, half] = s_c
            m = jnp.maximum(jnp.max(s_w, axis=-1, keepdims=True), jnp.max(s_c, axis=-1, keepdims=True))
            m_ref[slot, half] = jnp.broadcast_to(m, (nq, LANES))
        v_w = kv_v_ref[window(jnp.maximum(g - 1, 0)), :]
        v_c = kv_v_ref[0:CTX_LEN, :]
        outs = []
        for half in range(2):
            m = m_ref[prev, half]
            p_w = jnp.exp(sw_ref[prev, half] - jnp.concatenate([m] * (nw // LANES), axis=1))
            p_c = jnp.exp(sc_ref[prev, half] - jnp.concatenate([m] * (CTX_LEN // LANES), axis=1))
            l = jnp.sum(p_w, axis=-1, keepdims=True) + jnp.sum(p_c, axis=-1, keepdims=True)
            o = _dot(p_w.astype(BF16), v_w) + _dot(p_c.astype(BF16), v_c)
            outs.append(o * (1.0 / l))
        o_ref[...] = jnp.where(lane < NA_DH, outs[0], outs[1]).astype(o_ref.dtype)

    @pl.when(g % 2 == 0)
    def _():
        step(0, 1)

    @pl.when(g % 2 == 1)
    def _():
        step(1, 0)


def _na(q, k, v, bias, rows):
    t, d = q.shape
    s = t - CTX_LEN
    nq = NA_QROWS * GRID_W
    nw = NA_WROWS * GRID_W
    assert nq == ROW_TILE and rows >= NA_WROWS and rows % NA_QROWS == 0
    n_groups = rows // NA_QROWS
    cur = lambda g: jnp.minimum(g, n_groups - 1)
    variant = lambda g: jnp.where(g == 0, 0, jnp.where(g == n_groups - 1, 2, 1))
    return pl.pallas_call(
        functools.partial(_na_kernel, rows=rows, n_groups=n_groups),
        grid=(NA_HEADS // 2, n_groups + 1),
        in_specs=[pl.BlockSpec((nq, LANES), lambda p, g: (cur(g) + 1, p)),
                  pl.BlockSpec((t, LANES), lambda p, g: (0, p)),
                  pl.BlockSpec((t, LANES), lambda p, g: (0, p)),
                  pl.BlockSpec((1, 2, nq, nw), lambda p, g: (variant(cur(g)), p, 0, 0))],
        out_specs=pl.BlockSpec((nq, LANES), lambda p, g: (jnp.maximum(g - 1, 0), p)),
        out_shape=jax.ShapeDtypeStruct((s, d), BF16),
        scratch_shapes=[pltpu.VMEM((2, 2, nq, nw), F32), pltpu.VMEM((2, 2, nq, CTX_LEN), F32),
                        pltpu.VMEM((2, 2, nq, LANES), F32)],
        compiler_params=_cparams(("arbitrary", "arbitrary")),
        name="na",
    )(q, k, v, bias)


def _attn_out_kernel(*refs, gla):
    if gla:
        (x_ref, of_ref, ob_ref, r_ref, om_ref, on_ref, w_ref, g1_ref, gate_ref, sc_ref, sh_ref,
         rwh_ref, rwl_ref, rb_ref, xo_ref, h_ref, lg_ref) = refs
        o = of_ref[...] + ob_ref[...]
        r = r_ref[...]
        parts = []
        for hd in range(GLA_HEADS):
            sl = slice(hd * GLA_DV, (hd + 1) * GLA_DV)
            rh = r[:, sl]
            parts.append(_rms(o[:, sl], on_ref[...]) * (rh * (1.0 / (1.0 + jnp.exp(-rh)))))
        mix_a = jnp.concatenate(parts, axis=1).astype(BF16)
        y = _dot(mix_a, w_ref[0:GLA_VW, :]) + _dot(om_ref[...], w_ref[GLA_VW:, :])
    else:
        (x_ref, om_ref, w_ref, g1_ref, gate_ref, sc_ref, sh_ref,
         rwh_ref, rwl_ref, rb_ref, xo_ref, h_ref, lg_ref) = refs
        y = _dot(om_ref[...], w_ref[...])
    x = x_ref[...] + _rms(y, g1_ref[...]) * gate_ref[0]
    xo_ref[...] = x
    h = x * lax.rsqrt(jnp.mean(x * x, axis=-1, keepdims=True) + EPS) * sc_ref[0] + sh_ref[0]
    for c in range(h.shape[1] // LANES):
        h_ref[:, c, :] = h[:, c * LANES:(c + 1) * LANES]
    h_hi = h.astype(BF16)
    h_lo = (h - h_hi.astype(F32)).astype(BF16)
    lg_ref[...] = _dot(h_hi, rwh_ref[...]) + _dot(h_lo, rwh_ref[...]) + _dot(h_hi, rwl_ref[...]) + rb_ref[...]


def _attn_out(xs, mix_inputs, w_out, g1, gate, scale, shift, rw_hi, rw_lo, rb, *, gla, first_tile):
    d = xs.shape[1]
    tm = ROW_TILE
    t = mix_inputs[0].shape[0]
    row = lambda i: (i, 0)
    xrow = lambda i: (i + first_tile, 0)
    const = lambda i: (0, 0)
    strm = lambda i: (_stream(i + first_tile), 0, 0)
    if gla:
        of, ob, r, om, onorm = mix_inputs
        mix_specs = [pl.BlockSpec((tm, GLA_VW), row), pl.BlockSpec((tm, GLA_VW), row), pl.BlockSpec((tm, GLA_VW), row),
                     pl.BlockSpec((tm, MLA_VW), row), pl.BlockSpec(onorm.shape, const)]
    else:
        mix_specs = [pl.BlockSpec((tm, d), row)]
    return pl.pallas_call(
        functools.partial(_attn_out_kernel, gla=gla),
        grid=(t // tm,),
        in_specs=[pl.BlockSpec((tm, d), xrow)] + mix_specs + [
            pl.BlockSpec(w_out.shape, const), pl.BlockSpec(g1.shape, const),
            pl.BlockSpec((1, 1, d), strm), pl.BlockSpec((1, 1, d), strm), pl.BlockSpec((1, 1, d), strm),
            pl.BlockSpec(rw_hi.shape, const), pl.BlockSpec(rw_lo.shape, const), pl.BlockSpec(rb.shape, const)],
        out_specs=[pl.BlockSpec((tm, d), row), pl.BlockSpec((tm, d // LANES, LANES), lambda i: (i, 0, 0)),
                   pl.BlockSpec((tm, LANES), row)],
        out_shape=[jax.ShapeDtypeStruct((t, d), F32), jax.ShapeDtypeStruct((t, d // LANES, LANES), F32),
                   jax.ShapeDtypeStruct((t, LANES), F32)],
        compiler_params=_cparams(("arbitrary",)),
        name="attn_out",
    )(xs, *mix_inputs, w_out, g1, gate, scale, shift, rw_hi, rw_lo, rb)


def _moe_kernel(be_ref, tok_cur_ref, tok_nxt_ref, hp_ref, gate_ref, w1_ref, p_ref, b1_ref, w2_ref, b2_ref, o_ref,
                xa_ref, xb_ref, w1b_ref, w2b_ref, sem_ref, *, queues):
    i = pl.program_id(0)
    last = pl.num_programs(0) - 1
    bm = o_ref.shape[0]
    f = w2_ref.shape[2]

    @pl.when((i == 0) | (be_ref[i] != be_ref[jnp.maximum(i - 1, 0)]))
    def _():
        for c in range(2 * f // (2 * LANES)):
            cols = slice(c * 2 * LANES, (c + 1) * 2 * LANES)
            w1b_ref[:, cols] = _dot(w1_ref[0, 0, :, cols].astype(BF16), p_ref[...]).astype(BF16)
        w2b_ref[...] = w2_ref[0, 0].astype(BF16)

    def row_copy(tok_ref, r, buf, sem):
        return pltpu.make_async_copy(hp_ref.at[pl.ds(tok_ref[0, 0, r], 1)], buf.at[pl.ds(r, 1)], sem)

    def wait_block(buf, sem):
        pltpu.make_async_copy(hp_ref.at[pl.ds(0, bm)], buf, sem).wait()

    @pl.when(i == 0)
    def _():
        for r in range(bm):
            row_copy(tok_cur_ref, r, xa_ref, sem_ref.at[0]).start(priority=queues[r % 2])

    def step(cur, sem_cur, nxt, sem_nxt):
        wait_block(cur, sem_cur)
        x = jnp.concatenate([cur[:, c, :] for c in range(cur.shape[1])], axis=1).astype(BF16)
        n_grp = f // LANES
        per = bm // n_grp
        acts = []
        for c in range(n_grp):
            u = _dot(x, w1b_ref[:, 2 * c * LANES:(2 * c + 2) * LANES]) \
                + b1_ref[0, 0, :, 2 * c * LANES:(2 * c + 2) * LANES]
            glu = jnp.minimum(u[:, :LANES], SWIGLU_LIMIT)
            lin = jnp.clip(u[:, LANES:], -SWIGLU_LIMIT, SWIGLU_LIMIT)
            acts.append((glu * (1.0 / (1.0 + jnp.exp(-SWIGLU_ALPHA * glu))) * (lin + 1.0)).astype(BF16))
            for r in range(c * per, (c + 1) * per):
                row_copy(tok_nxt_ref, r, nxt, sem_nxt).start(priority=queues[r % 2])
        y = _dot(jnp.concatenate(acts, axis=1), w2b_ref[...]) + b2_ref[0, 0]
        o_ref[...] = (y * gate_ref[...]).astype(o_ref.dtype)

        @pl.when(i == last)
        def _():
            wait_block(nxt, sem_nxt)

    @pl.when(i % 2 == 0)
    def _():
        step(xa_ref, sem_ref.at[0], xb_ref, sem_ref.at[1])

    @pl.when(i % 2 == 1)
    def _():
        step(xb_ref, sem_ref.at[1], xa_ref, sem_ref.at[0])


def _regroup_matrix():
    grp = 2 * LANES
    src = np.concatenate([np.arange(0, grp, 2), np.arange(1, grp, 2)])
    perm = np.zeros((grp, grp), np.float32)
    perm[src, np.arange(grp)] = 1.0
    return jnp.asarray(perm, BF16)


def _moe_experts(block_e, slot_tok, hp, slot_gate, w1, b1, w2, b2, layer):
    n_blocks, _, bm = slot_tok.shape
    row_shape = hp.shape[1:]
    d = row_shape[0] * row_shape[1]
    f = w2.shape[2]
    perm = _regroup_matrix()
    nxt = lambda i, be: (jnp.minimum(i + 1, n_blocks - 1), 0, 0)
    grid_spec = pltpu.PrefetchScalarGridSpec(
        num_scalar_prefetch=1,
        grid=(n_blocks,),
        in_specs=[pl.BlockSpec((1, 1, bm), lambda i, be: (i, 0, 0), memory_space=pltpu.SMEM),
                  pl.BlockSpec((1, 1, bm), nxt, memory_space=pltpu.SMEM),
                  pl.BlockSpec(memory_space=pl.ANY),
                  pl.BlockSpec((bm, 1), lambda i, be: (i, 0)),
                  pl.BlockSpec((1, 1, d, 2 * f), lambda i, be: (layer, be[i], 0, 0)),
                  pl.BlockSpec(perm.shape, lambda i, be: (0, 0)),
                  pl.BlockSpec((1, 1, 1, 2 * f), lambda i, be: (layer, be[i], 0, 0)),
                  pl.BlockSpec((1, 1, f, d), lambda i, be: (layer, be[i], 0, 0)),
                  pl.BlockSpec((1, 1, 1, d), lambda i, be: (layer, be[i], 0, 0))],
        out_specs=pl.BlockSpec((bm, d), lambda i, be: (i, 0)),
        scratch_shapes=[pltpu.VMEM((bm,) + row_shape, F32), pltpu.VMEM((bm,) + row_shape, F32),
                        pltpu.VMEM((d, 2 * f), BF16), pltpu.VMEM((f, d), BF16), pltpu.SemaphoreType.DMA((2,))],
    )
    return pl.pallas_call(
        functools.partial(_moe_kernel, queues=(0, 1) if layer == 0 else (1, 1)),
        grid_spec=grid_spec,
        out_shape=jax.ShapeDtypeStruct((n_blocks * bm, d), BF16),
        compiler_params=_cparams(("arbitrary",)),
        name="moe_experts",
    )(block_e, slot_tok, slot_tok, hp, slot_gate, w1, perm, b1, w2, b2)


def _moe(hp, logits, w1, b1, w2, b2, layer):
    t = hp.shape[0]
    bm = MOE_ROWS
    top_val, top_idx = lax.top_k(logits, TOP_K)
    gates = jax.nn.softmax(top_val, axis=-1)
    n_assign = t * TOP_K
    n_blocks = -(-n_assign // bm) + N_EXPERTS
    experts = jnp.arange(N_EXPERTS, dtype=jnp.int32)
    hit = jnp.any(top_idx[:, :, None] == experts, axis=1).astype(jnp.int32)
    counts = jnp.sum(hit, axis=0)
    before = jnp.cumsum(hit, axis=0) - hit
    padded = ((counts + bm - 1) // bm) * bm
    pad_end = jnp.cumsum(padded)
    pad_start = pad_end - padded
    start = jnp.cumsum(counts) - counts
    slot_of = jnp.take_along_axis(before + pad_start[None, :], top_idx, axis=1)
    block_first = jnp.arange(n_blocks, dtype=jnp.int32) * bm
    block_e = jnp.minimum(jnp.sum(block_first[:, None] >= pad_end[None, :], axis=1), N_EXPERTS - 1).astype(jnp.int32)
    order = jnp.argsort(top_idx.reshape(-1))
    local = (block_first - pad_start[block_e])[:, None] + jnp.arange(bm, dtype=jnp.int32)[None, :]
    valid = local < counts[block_e][:, None]
    assign = order[jnp.clip(start[block_e][:, None] + local, 0, n_assign - 1)]
    slot_tok = jnp.where(valid, assign // TOP_K, 0).astype(jnp.int32)
    slot_gate = jnp.where(valid, gates.reshape(-1)[assign], 0.0)
    yb = _moe_experts(block_e, slot_tok.reshape(n_blocks, 1, bm), hp, slot_gate.reshape(n_blocks * bm, 1),
                      w1, b1, w2, b2, layer)
    return yb[slot_of.T.reshape(-1)].reshape(TOP_K, t, yb.shape[1])


def _moe_biases(moe_b1, moe_b2):
    depth, e, f2 = moe_b1.shape
    b1p = jnp.transpose(moe_b1.reshape(depth, e, f2 // (2 * LANES), LANES, 2), (0, 1, 2, 4, 3))
    return b1p.reshape(depth, e, 1, f2), moe_b2.reshape(depth, e, 1, moe_b2.shape[2])


def _resid_kernel(*refs, project):
    if project:
        x_ref, y_ref, g3_ref, gate_ref, sc_ref, sh_ref, w_ref, xo_ref, q_ref, k_ref, v_ref = refs
    else:
        x_ref, y_ref, g3_ref, gate_ref, xo_ref = refs
    y = y_ref[0].astype(F32)
    for j in range(1, TOP_K):
        y = y + y_ref[j].astype(F32)
    x = x_ref[...] + _rms(y, g3_ref[...]) * gate_ref[0]
    xo_ref[...] = x
    if project:
        d = x.shape[1]
        h = x * lax.rsqrt(jnp.mean(x * x, axis=-1, keepdims=True) + EPS) * sc_ref[0] + sh_ref[0]
        u = _dot(h.astype(BF16), w_ref[...])
        q_ref[...] = u[:, 0:d].astype(BF16)
        k_ref[...] = u[:, d:2 * d].astype(BF16)
        v_ref[...] = u[:, 2 * d:3 * d].astype(BF16)


def _resid(xs, y, g3, gate, scale=None, shift=None, w=None, *, first_tile=0):
    t, d = xs.shape
    tm = ROW_TILE
    project = w is not None
    row = lambda i: (i, 0)
    const = lambda i: (0, 0)
    strm = lambda i: (_stream(i + first_tile), 0, 0)
    vec = pl.BlockSpec((1, 1, d), strm)
    in_specs = [pl.BlockSpec((tm, d), row), pl.BlockSpec((TOP_K, tm, d), lambda i: (0, i, 0)),
                pl.BlockSpec(g3.shape, const), vec]
    out_specs = [pl.BlockSpec((tm, d), row)]
    out_shape = [jax.ShapeDtypeStruct((t, d), F32)]
    args = [xs, y, g3, gate]
    if project:
        in_specs += [vec, vec, pl.BlockSpec(w.shape, const)]
        out_specs += [pl.BlockSpec((tm, d), row)] * 3
        out_shape += [jax.ShapeDtypeStruct((t, d), BF16)] * 3
        args += [scale, shift, w]
    return pl.pallas_call(
        functools.partial(_resid_kernel, project=project),
        grid=(t // tm,),
        in_specs=in_specs, out_specs=out_specs, out_shape=out_shape,
        compiler_params=_cparams(("arbitrary",)),
        name="resid",
    )(*args)


def _rope_tables(s):
    tok = jnp.arange(s)
    row = (tok // GRID_W).astype(F32)
    col = (tok % GRID_W).astype(F32)
    n_freq = MLA_ROPE // 4
    inv = ROPE_THETA ** (-jnp.arange(n_freq, dtype=F32) / n_freq)
    ang = jnp.concatenate([row[:, None] * inv, col[:, None] * inv], axis=-1)
    cos, sin = jnp.cos(ang), jnp.sin(ang)
    cos_t = jnp.concatenate([cos, cos, cos, cos], axis=-1)
    sin_t = jnp.concatenate([-sin, sin, -sin, sin], axis=-1)
    cos_t = jnp.concatenate([jnp.ones((CTX_LEN, LANES), F32), cos_t], axis=0)
    sin_t = jnp.concatenate([jnp.zeros((CTX_LEN, LANES), F32), sin_t], axis=0)
    return cos_t, sin_t


def _pad_cols(w, n):
    return jnp.pad(w, ((0, 0), (0, n - w.shape[1])))


def _swap_halves(w):
    half = w.shape[-1] // 2
    return jnp.concatenate([w[..., half:], w[..., :half]], axis=-1)


def _proj0_weights(in_w, wa_f, ba_f, wa_b, ba_b, wuq, wukv):
    o = np.cumsum((0, GLA_QK, GLA_QK, GLA_VW, GLA_VW, GLA_GATE_RANK, GLA_GATE_RANK, MLA_Q_LORA, MLA_KV_LORA,
                   MLA_ROPE))
    q, k, v, r, a_f, a_b, cq, ckv, kr = [in_w[:, o[i]:o[i + 1]] for i in range(9)]
    w_in = jnp.concatenate([q * (GLA_DK ** -0.5), k, v, r, cq, ckv, _pad_cols(kr, LANES),
                            _pad_cols(_swap_halves(kr), LANES),
                            _pad_cols(jnp.concatenate([a_f, a_b], axis=1), LANES)], axis=1).astype(BF16)
    wa = jnp.zeros((LANES, 2 * GLA_QK), F32)
    wa = wa.at[0:GLA_GATE_RANK, 0:GLA_QK].set(wa_f).at[GLA_GATE_RANK:2 * GLA_GATE_RANK, GLA_QK:].set(wa_b)
    ba = jnp.concatenate([ba_f, ba_b]).reshape(1, 2 * GLA_QK)
    wq = wuq.reshape(MLA_Q_LORA, MLA_HEADS, MLA_NOPE + MLA_ROPE)
    nope = wq[:, :, :MLA_NOPE].reshape(MLA_Q_LORA, MLA_HEADS * MLA_NOPE)
    rope = wq[:, :, MLA_NOPE:]
    pad3 = lambda z: jnp.pad(z, ((0, 0), (0, 0), (0, LANES - MLA_ROPE))).reshape(MLA_Q_LORA, MLA_HEADS * LANES)
    wuq_x = jnp.concatenate([nope, pad3(rope), pad3(_swap_halves(rope))], axis=1).astype(BF16)
    wkv = wukv.reshape(MLA_KV_LORA, MLA_HEADS, MLA_NOPE + MLA_V)
    wukv_x = jnp.concatenate([wkv[:, :, :MLA_NOPE].reshape(MLA_KV_LORA, -1),
                              wkv[:, :, MLA_NOPE:].reshape(MLA_KV_LORA, -1)], axis=1).astype(BF16)
    return w_in, wa.astype(BF16), ba, wuq_x, wukv_x


def _router_weights(rw, rb):
    rw = _pad_cols(rw, LANES)
    hi = rw.astype(BF16)
    lo = (rw - hi.astype(F32)).astype(BF16)
    return hi, lo, _pad_cols(rb.reshape(1, -1), LANES)


def kernel(x, c, ctx, c_ctx, ada_w, ada_b, norm_g, router_w, router_b, moe_w1, moe_b1, moe_w2, moe_b2, ab_in_w,
           gla_wa_f, gla_ba_f, gla_wa_b, gla_ba_b, gla_onorm, mla_qnorm, mla_wuq, mla_kvnorm, mla_wukv, ab_out_w,
           na_qkv_w, na_rpb, na_out_w):
    b, s, d = x.shape
    assert b == 1 and d == D_MODEL and ctx.shape == (1, CTX_LEN, d) and s % (NA_QROWS * GRID_W) == 0
    rows = s // GRID_W
    xs = jnp.concatenate([ctx[0], x[0]], axis=0)

    cc = jnp.zeros((8, d), F32).at[0].set(c[0]).at[1].set(c_ctx)
    mod = _ada(cc, ada_w, ada_b)

    def vectors(l):
        parts = jnp.split(mod[l, 0:2][::-1], 6, axis=-1)
        sh1, sc1, g1, sh2, sc2, g2 = [p.reshape(2, 1, d) for p in parts]
        return ((1 + sc1) * norm_g[l, 0], sh1, g1, (1 + sc2) * norm_g[l, 2], sh2, g2)

    sc1, sh1, g1, sc2, sh2, g2 = vectors(0)
    cos_t, sin_t = _rope_tables(s)
    w_in, wa, ba, wuq_x, wukv_x = _proj0_weights(ab_in_w[0], gla_wa_f[0], gla_ba_f[0], gla_wa_b[0], gla_ba_b[0],
                                                 mla_wuq[0], mla_wukv[0])
    qg, kg, vg, r, gf, gb, qm, km, vm = _proj0(xs, sc1, sh1, cos_t, sin_t, w_in, wa, ba,
                                               mla_qnorm[0].reshape(1, -1), wuq_x,
                                               mla_kvnorm[0].reshape(1, -1), wukv_x)
    o_f, o_b = _gla(qg, kg, vg, gf, gb)
    om_l = _flash(qm[:, CTX_LEN:], km, vm, FLASH_TQ, FLASH_TK)
    om_c = _flash(qm[:, :CTX_LEN], km[:, :CTX_LEN], vm[:, :CTX_LEN], CTX_LEN, CTX_LEN)
    om = jnp.concatenate([om_c, om_l], axis=0)
    rw_hi, rw_lo, rb = _router_weights(router_w[0], router_b[0])
    xs, h2, logits = _attn_out(xs, (o_f, o_b, r, om, gla_onorm[0].reshape(1, -1)), ab_out_w[0].astype(BF16),
                               norm_g[0, 1].reshape(1, -1), g1, sc2, sh2, rw_hi, rw_lo, rb, gla=True, first_tile=0)
    b1p, b2r = _moe_biases(moe_b1, moe_b2)
    y = _moe(h2, logits[:, :N_EXPERTS], moe_w1, b1p, moe_w2, b2r, 0)

    sc1n, sh1n, g1n, sc2n, sh2n, g2n = vectors(1)
    qkv_w = jnp.concatenate([na_qkv_w[0][:, :d] * (NA_DH ** -0.5), na_qkv_w[0][:, d:]], axis=1).astype(BF16)
    xs, q, k, v = _resid(xs, y, norm_g[0, 3].reshape(1, -1), g2, sc1n, sh1n, qkv_w)
    o_na = _na(q, k, v, _na_bias_tiles(na_rpb[0], rows), rows)
    rw_hi, rw_lo, rb = _router_weights(router_w[1], router_b[1])
    x_l, h2, logits = _attn_out(xs, (o_na,), na_out_w[0].astype(BF16), norm_g[1, 1].reshape(1, -1), g1n, sc2n, sh2n,
                                rw_hi, rw_lo, rb, gla=False, first_tile=1)
    y = _moe(h2, logits[:, :N_EXPERTS], moe_w1, b1p, moe_w2, b2r, 1)
    (x_l,) = _resid(x_l, y, norm_g[1, 3].reshape(1, -1), g2n, first_tile=1)
    return x_l.reshape(1, s, d)
```

```python
import functools

import numpy as np
import jax
import jax.numpy as jnp
from jax import lax
from jax.experimental import pallas as pl
from jax.experimental.pallas import tpu as pltpu

F32 = jnp.float32
BF16 = jnp.bfloat16

D_MODEL = 1024
DEPTH = 2
CTX_LEN = 256
GRID_W = 64
EPS = 1e-6
ROPE_THETA = 10000.0

GLA_HEADS = 4
GLA_DK = 64
GLA_DV = 128
GLA_GATE_RANK = 16
GLA_TAU = 16.0
GLA_QK = GLA_HEADS * GLA_DK
GLA_VW = GLA_HEADS * GLA_DV

MLA_HEADS = 4
MLA_Q_LORA = 256
MLA_KV_LORA = 128
MLA_NOPE = 128
MLA_ROPE = 64
MLA_V = 128
MLA_VW = MLA_HEADS * MLA_V
MLA_QK_PAD = 256

NA_HEADS = 16
NA_DH = D_MODEL // NA_HEADS
NA_KH = 8
NA_KW = 16
NA_QROWS = 4
NA_WROWS = NA_QROWS + NA_KH

N_EXPERTS = 32
TOP_K = 4
SWIGLU_LIMIT = 7.0
SWIGLU_ALPHA = 1.702

LANES = 128
ROW_TILE = 256
MOE_ROWS = 256
VMEM_LIMIT = 56 * 1024 * 1024
NEG_BIG = -1e30

GLA_CHUNK = 64
GLA_SUB = 16
FLASH_TQ = 512
FLASH_TK = 3328
FLASH_SUB = 256


def _cparams(sem):
    return pltpu.CompilerParams(dimension_semantics=sem, vmem_limit_bytes=VMEM_LIMIT)


def _dot(a, b):
    return jnp.dot(a, b, preferred_element_type=F32)


def _dot_nt(a, b):
    return lax.dot_general(a, b, (((1,), (1,)), ((), ())), preferred_element_type=F32)


def _rms(x, g):
    return x * lax.rsqrt(jnp.mean(x * x, axis=-1, keepdims=True) + EPS) * g


def _stream(i):
    return jnp.where(i == 0, 0, 1)


def _ada_kernel(c_ref, w_ref, b_ref, o_ref):
    c = c_ref[...]
    s = (c * (1.0 / (1.0 + jnp.exp(-c)))).astype(BF16)
    o_ref[0] = _dot(s, w_ref[0].astype(BF16)) + b_ref[0]


def _ada(cc, ada_w, ada_b):
    depth, d, n = ada_w.shape
    tn = 1536
    return pl.pallas_call(
        _ada_kernel,
        grid=(depth, n // tn),
        in_specs=[pl.BlockSpec((8, d), lambda l, j: (0, 0)),
                  pl.BlockSpec((1, d, tn), lambda l, j: (l, 0, j)),
                  pl.BlockSpec((1, 1, tn), lambda l, j: (l, 0, j))],
        out_specs=pl.BlockSpec((1, 8, tn), lambda l, j: (l, 0, j)),
        out_shape=jax.ShapeDtypeStruct((depth, 8, n), F32),
        compiler_params=_cparams(("arbitrary", "arbitrary")),
        name="ada",
    )(cc, ada_w, ada_b.reshape(depth, 1, n))


def _proj0_kernel(x_ref, sc_ref, sh_ref, cos_ref, sin_ref, win_ref, wa_ref, ba_ref, qn_ref, wuq_ref,
                  kvn_ref, wukv_ref,
                  qg_ref, kg_ref, vg_ref, r_ref, gf_ref, gb_ref, qm_ref, km_ref, vm_ref, *, mla_scale):
    x = x_ref[...]
    h = x * lax.rsqrt(jnp.mean(x * x, axis=-1, keepdims=True) + EPS) * sc_ref[0] + sh_ref[0]
    u = _dot(h.astype(BF16), win_ref[...])
    qg_ref[...] = u[:, 0:256]
    kg_ref[...] = u[:, 256:512]
    vg_ref[...] = u[:, 512:1024].astype(BF16)
    r_ref[...] = u[:, 1024:1536]
    cq = u[:, 1536:1792]
    ckv = u[:, 1792:1920]
    kr = u[:, 1920:2048]
    krs = u[:, 2048:2176]
    a = u[:, 2176:2304]

    ga = _dot(a.astype(BF16), wa_ref[...]) + ba_ref[...]
    la = (jnp.minimum(ga, 0.0) - jnp.log1p(jnp.exp(-jnp.abs(ga)))) * (1.0 / GLA_TAU)
    gf_ref[...] = la[:, :GLA_QK]
    gb_ref[...] = la[:, GLA_QK:]

    cos = cos_ref[...]
    sin = sin_ref[...]
    qm = _dot(_rms(cq, qn_ref[...]).astype(BF16), wuq_ref[...])
    kvm = _dot(_rms(ckv, kvn_ref[...]).astype(BF16), wukv_ref[...])
    kroped = (kr * cos + krs * sin).astype(BF16)
    for hd in range(MLA_HEADS):
        nope = qm[:, hd * 128:(hd + 1) * 128]
        rp = qm[:, 512 + hd * 128:512 + (hd + 1) * 128]
        rps = qm[:, 1024 + hd * 128:1024 + (hd + 1) * 128]
        roped = rp * cos + rps * sin
        qm_ref[hd] = (jnp.concatenate([nope, roped], axis=1) * mla_scale).astype(BF16)
        km_ref[hd] = jnp.concatenate([kvm[:, hd * 128:(hd + 1) * 128].astype(BF16), kroped], axis=1)
        v_h = kvm[:, 512 + hd * 128:512 + (hd + 1) * 128].astype(BF16)
        vm_ref[hd] = jnp.concatenate([v_h, jnp.ones_like(v_h)], axis=1)


def _proj0(xs, scale, shift, cos_t, sin_t, w_in, wa, ba, qn, wuq, kvn, wukv):
    t, d = xs.shape
    tm = ROW_TILE
    row = lambda i: (i, 0)
    const = lambda i: (0, 0)
    strm = lambda i: (_stream(i), 0, 0)
    hrow = lambda i: (0, i, 0)
    kern = functools.partial(_proj0_kernel, mla_scale=float((MLA_NOPE + MLA_ROPE) ** -0.5 * np.log2(np.e)))
    return pl.pallas_call(
        kern,
        grid=(t // tm,),
        in_specs=[pl.BlockSpec((tm, d), row),
                  pl.BlockSpec((1, 1, d), strm), pl.BlockSpec((1, 1, d), strm),
                  pl.BlockSpec((tm, LANES), row), pl.BlockSpec((tm, LANES), row),
                  pl.BlockSpec(w_in.shape, const), pl.BlockSpec(wa.shape, const), pl.BlockSpec(ba.shape, const),
                  pl.BlockSpec(qn.shape, const), pl.BlockSpec(wuq.shape, const),
                  pl.BlockSpec(kvn.shape, const), pl.BlockSpec(wukv.shape, const)],
        out_specs=[pl.BlockSpec((tm, GLA_QK), row), pl.BlockSpec((tm, GLA_QK), row),
                   pl.BlockSpec((tm, GLA_VW), row), pl.BlockSpec((tm, GLA_VW), row),
                   pl.BlockSpec((tm, GLA_QK), row), pl.BlockSpec((tm, GLA_QK), row),
                   pl.BlockSpec((MLA_HEADS, tm, MLA_QK_PAD), hrow),
                   pl.BlockSpec((MLA_HEADS, tm, MLA_QK_PAD), hrow),
                   pl.BlockSpec((MLA_HEADS, tm, 2 * MLA_V), hrow)],
        out_shape=[jax.ShapeDtypeStruct((t, GLA_QK), F32), jax.ShapeDtypeStruct((t, GLA_QK), F32),
                   jax.ShapeDtypeStruct((t, GLA_VW), BF16), jax.ShapeDtypeStruct((t, GLA_VW), F32),
                   jax.ShapeDtypeStruct((t, GLA_QK), F32), jax.ShapeDtypeStruct((t, GLA_QK), F32),
                   jax.ShapeDtypeStruct((MLA_HEADS, t, MLA_QK_PAD), BF16),
                   jax.ShapeDtypeStruct((MLA_HEADS, t, MLA_QK_PAD), BF16),
                   jax.ShapeDtypeStruct((MLA_HEADS, t, 2 * MLA_V), BF16)],
        compiler_params=_cparams(("arbitrary",)),
        name="proj0",
    )(xs, scale, shift, cos_t, sin_t, w_in, wa, ba, qn, wuq, kvn, wukv)


def _gla_tables():
    c, s = GLA_CHUNK, GLA_SUB
    t = np.arange(c)[:, None]
    u = np.arange(c)[None, :]
    bt, bu = t // s, u // s
    half = (c // s) // 2
    mats = [(bu == bt) & (u <= t), (bu == bt) & (u > t), bu < bt, bu > bt,
            (bu >= half) & (bu < bt), (bu > bt) & (bu < half)]
    m = np.concatenate([x.astype(np.float32) for x in mats], axis=0)
    sel = np.stack([(bt == bu) & (u <= t), bt == bu + 1, (bt >= half) & (bu < half)]).astype(np.float32)
    m_b = np.concatenate([x.astype(np.float32)[::-1, ::-1] for x in mats], axis=0)
    sel_b = sel[:, ::-1, ::-1]
    return np.stack([m, m_b]), np.stack([sel, sel_b])


def _gla_chunk(q, k, g, v_ref, vt_ref, j, m, sel, st_ref, o_ref):
    rows = pl.ds(j * GLA_CHUNK, GLA_CHUNK)
    c = GLA_CHUNK
    g_hi = g.astype(BF16)
    g_r1 = g - g_hi.astype(F32)
    g_mid = g_r1.astype(BF16)
    g_lo = (g_r1 - g_mid.astype(F32)).astype(BF16)
    cs = _dot(m, g_hi) + _dot(m, g_mid) + _dot(m, g_lo)
    own, rest, before, after = cs[0:c], cs[c:2 * c], cs[2 * c:3 * c], cs[3 * c:4 * c]
    far_q, far_k = cs[4 * c:5 * c], cs[5 * c:6 * c]
    qd = q * jnp.exp(own)
    kd = k * jnp.exp(-own)
    kh = k * jnp.exp(rest)
    q_in = qd * jnp.exp(before)
    k_out = kh * jnp.exp(after)
    q_far = qd * jnp.exp(far_q)
    k_far = kh * jnp.exp(far_k)
    decay = jnp.exp(own + rest + before + after)
    lane = lax.broadcasted_iota(jnp.int32, (c, LANES), 1)
    s_diag, s_adj, s_far = sel[0] > 0.5, sel[1] > 0.5, sel[2] > 0.5
    for p in range(GLA_HEADS // 2):
        lo, hi = p * LANES, (p + 1) * LANES
        kd_p = kd[:, lo:hi].astype(BF16)
        kh_p = kh[:, lo:hi].astype(BF16)
        kf_p = k_far[:, lo:hi].astype(BF16)
        st = st_ref[p]
        st_b = st.astype(BF16)
        upd = st * decay[0:1, lo:hi]
        for half in range(2):
            hd = 2 * p + half
            msk = (lane < GLA_DK) if half == 0 else (lane >= GLA_DK)
            pick = lambda z: jnp.where(msk, z[:, lo:hi], 0.0).astype(BF16)
            qd_h = pick(qd)
            a = jnp.where(s_diag, _dot_nt(qd_h, kd_p),
                          jnp.where(s_adj, _dot_nt(qd_h, kh_p),
                                    jnp.where(s_far, _dot_nt(pick(q_far), kf_p), 0.0)))
            v_h = v_ref[rows, hd * GLA_DV:(hd + 1) * GLA_DV]
            o_ref[rows, hd * GLA_DV:(hd + 1) * GLA_DV] = _dot(a.astype(BF16), v_h) + _dot_nt(pick(q_in), st_b)
            upd = upd + _dot(vt_ref[j, hd * GLA_DV:(hd + 1) * GLA_DV, :], pick(k_out))
        st_ref[p] = upd


def _gla_kernel(qf_ref, kf_ref, gf_ref, vf_ref, vtf_ref, qb_ref, kb_ref, gb_ref, vb_ref, vtb_ref, m_ref, sel_ref,
                of_ref, ob_ref, stf_ref, stb_ref):
    @pl.when(pl.program_id(0) == 0)
    def _():
        stf_ref[...] = jnp.zeros_like(stf_ref)
        stb_ref[...] = jnp.zeros_like(stb_ref)

    n = ROW_TILE // GLA_CHUNK
    for j in range(n):
        rows = pl.ds(j * GLA_CHUNK, GLA_CHUNK)
        _gla_chunk(qf_ref[rows, :], kf_ref[rows, :], gf_ref[rows, :], vf_ref, vtf_ref, j,
                   m_ref[0], sel_ref[0], stf_ref, of_ref)
        jb = n - 1 - j
        rows = pl.ds(jb * GLA_CHUNK, GLA_CHUNK)
        _gla_chunk(qb_ref[rows, :], kb_ref[rows, :], gb_ref[rows, :], vb_ref, vtb_ref, jb,
                   m_ref[1], sel_ref[1], stb_ref, ob_ref)


def _gla(q, k, v, gf, gb):
    t = q.shape[0]
    tm = ROW_TILE
    nb = t // tm
    n = tm // GLA_CHUNK
    m_np, sel_np = _gla_tables()
    m = jnp.asarray(m_np, BF16)
    sel = jnp.asarray(sel_np, F32)
    vt = jnp.transpose(v.reshape(t // GLA_CHUNK, GLA_CHUNK, GLA_VW), (0, 2, 1))
    fwd = lambda i: (i, 0)
    bwd = lambda i: (jnp.where(i == 0, 0, nb - i), 0)
    fwd_t = lambda i: (i, 0, 0)
    bwd_t = lambda i: (jnp.where(i == 0, 0, nb - i), 0, 0)
    c3 = lambda i: (0, 0, 0)
    c4 = lambda i: (0, 0, 0, 0)
    spec = lambda w, im: pl.BlockSpec((tm, w), im)
    return pl.pallas_call(
        _gla_kernel,
        grid=(nb,),
        in_specs=[spec(GLA_QK, fwd), spec(GLA_QK, fwd), spec(GLA_QK, fwd), spec(GLA_VW, fwd),
                  pl.BlockSpec((n, GLA_VW, GLA_CHUNK), fwd_t),
                  spec(GLA_QK, bwd), spec(GLA_QK, bwd), spec(GLA_QK, bwd), spec(GLA_VW, bwd),
                  pl.BlockSpec((n, GLA_VW, GLA_CHUNK), bwd_t),
                  pl.BlockSpec(m.shape, c3), pl.BlockSpec(sel.shape, c4)],
        out_specs=[spec(GLA_VW, fwd), spec(GLA_VW, bwd)],
        out_shape=[jax.ShapeDtypeStruct((t, GLA_VW), F32), jax.ShapeDtypeStruct((t, GLA_VW), F32)],
        scratch_shapes=[pltpu.VMEM((GLA_HEADS // 2, GLA_DV, LANES), F32),
                        pltpu.VMEM((GLA_HEADS // 2, GLA_DV, LANES), F32)],
        compiler_params=_cparams(("arbitrary",)),
        name="gla",
    )(q, k, gf, v, vt, q, k, gb, v, vt, m, sel)


def _flash_kernel(q_ref, k_ref, v_ref, o_ref, s_ref, mc_ref, m_ref, acc_ref, *, tk, n_chunks):
    m_ref[...] = jnp.full_like(m_ref, -jnp.inf)
    acc_ref[...] = jnp.zeros_like(acc_ref)
    q = q_ref[0]
    tq = q.shape[0]
    sub = min(tk, FLASH_SUB)
    dv = o_ref.shape[1]

    def scores(j, slot):
        off = pl.multiple_of(j * tk, tk)
        s = _dot_nt(q, k_ref[0, pl.ds(off, tk), :])
        s_ref[slot] = s
        mc_ref[slot] = jnp.broadcast_to(jnp.max(s, axis=-1, keepdims=True), (tq, LANES))

    def softmax_pv(j, slot):
        off = pl.multiple_of(j * tk, tk)
        m_old = m_ref[...]
        m_new = jnp.maximum(m_old, mc_ref[slot])
        alpha = jnp.exp2(m_old - m_new)
        m_ref[...] = m_new
        acc = jnp.concatenate([alpha, alpha], axis=1) * acc_ref[...]
        for c in range(tk // sub):
            ps = [jnp.exp2(s_ref[slot, :, c * sub + a * LANES:c * sub + (a + 1) * LANES] - m_new).astype(BF16)
                  for a in range(sub // LANES)]
            acc = acc + _dot(jnp.concatenate(ps, axis=1), v_ref[0, pl.ds(off + c * sub, sub), :])
        acc_ref[...] = acc

    scores(0, 0)

    def pair(i, carry):
        scores(2 * i + 1, 1)
        softmax_pv(2 * i, 0)
        scores(2 * i + 2, 0)
        softmax_pv(2 * i + 1, 1)
        return carry

    n_pairs = (n_chunks - 1) // 2
    lax.fori_loop(0, n_pairs, pair, 0)
    if n_chunks % 2 == 0:
        scores(n_chunks - 1, 1)
        softmax_pv(n_chunks - 2, 0)
        softmax_pv(n_chunks - 1, 1)
    else:
        softmax_pv(n_chunks - 1, 0)
    o_ref[...] = (acc_ref[:, :dv] * (1.0 / acc_ref[:, dv:])).astype(o_ref.dtype)


def _flash(q, k, v, tq, tk):
    h, s, dq = q.shape
    t, dv = k.shape[1], v.shape[2] // 2
    assert s % tq == 0 and t % tk == 0
    kern = functools.partial(_flash_kernel, tk=tk, n_chunks=t // tk)
    return pl.pallas_call(
        kern,
        grid=(h, s // tq),
        in_specs=[pl.BlockSpec((1, tq, dq), lambda hh, i: (hh, i, 0)),
                  pl.BlockSpec((1, t, dq), lambda hh, i: (hh, 0, 0), pipeline_mode=pl.Buffered(1)),
                  pl.BlockSpec((1, t, 2 * dv), lambda hh, i: (hh, 0, 0), pipeline_mode=pl.Buffered(1))],
        out_specs=pl.BlockSpec((tq, dv), lambda hh, i: (i, hh)),
        out_shape=jax.ShapeDtypeStruct((s, h * dv), BF16),
        scratch_shapes=[pltpu.VMEM((2, tq, tk), F32), pltpu.VMEM((2, tq, LANES), F32),
                        pltpu.VMEM((tq, LANES), F32), pltpu.VMEM((tq, 2 * dv), F32)],
        compiler_params=_cparams(("arbitrary", "arbitrary")),
        name="flash",
    )(q, k, v)


def _na_bias_tiles(rpb, rows):
    w = GRID_W
    qc = np.arange(w)[:, None]
    kc = np.arange(w)[None, :]
    c0 = np.clip(qc - NA_KW // 2, 0, w - NA_KW)
    col_ok = (kc >= c0) & (kc < c0 + NA_KW)
    dc = np.clip(kc - qc + NA_KW - 1, 0, 2 * NA_KW - 2)
    onehot = (np.arange(2 * NA_KW - 1)[:, None, None] == dc[None]).astype(np.float32)
    e1 = jnp.einsum('hrd,dqk->hrqk', rpb.astype(F32), jnp.asarray(onehot), precision=lax.Precision.HIGHEST)
    e1 = jnp.where(jnp.asarray(col_ok)[None, None], e1, NEG_BIG)
    e1 = jnp.concatenate([e1, jnp.full((NA_HEADS, 1, w, w), NEG_BIG, F32)], axis=1)
    idx = np.zeros((3, NA_QROWS, NA_WROWS), np.int32)
    for vi, (r_first, ks) in enumerate([(0, 0), (NA_QROWS, 0), (rows - NA_QROWS, rows - NA_WROWS)]):
        for a in range(NA_QROWS):
            qr = r_first + a
            r0 = min(max(qr - NA_KH // 2, 0), rows - NA_KH)
            for b in range(NA_WROWS):
                kr = ks + b
                idx[vi, a, b] = kr - qr + NA_KH - 1 if r0 <= kr < r0 + NA_KH else 2 * NA_KH - 1
    tiles = jnp.take(e1, jnp.asarray(idx.reshape(-1)), axis=1)
    tiles = tiles.reshape(NA_HEADS, 3, NA_QROWS, NA_WROWS, w, w)
    return jnp.transpose(tiles, (1, 0, 2, 4, 3, 5)).reshape(3, NA_HEADS, NA_QROWS * w, NA_WROWS * w)


def _na_kernel(q_ref, kv_k_ref, kv_v_ref, bias_ref, o_ref, sw_ref, sc_ref, m_ref, *, rows, n_groups):
    g = pl.program_id(1)
    nq = NA_QROWS * GRID_W
    nw = NA_WROWS * GRID_W
    lane = lax.broadcasted_iota(jnp.int32, (nq, LANES), 1)

    def window(grp):
        ks = jnp.clip(grp * NA_QROWS - NA_KH // 2, 0, rows - NA_WROWS)
        return pl.ds(pl.multiple_of(CTX_LEN + ks * GRID_W, GRID_W), nw)

    @pl.when(g == 0)
    def _():
        sw_ref[1] = jnp.zeros(sw_ref.shape[1:], F32)
        sc_ref[1] = jnp.zeros(sc_ref.shape[1:], F32)
        m_ref[1] = jnp.zeros(m_ref.shape[1:], F32)

    def step(slot, prev):
        q = q_ref[...]
        k_w = kv_k_ref[window(jnp.minimum(g, n_groups - 1)), :]
        k_c = kv_k_ref[0:CTX_LEN, :]
        for half in range(2):
            msk = (lane < NA_DH) if half == 0 else (lane >= NA_DH)
            q_h = jnp.where(msk, q, jnp.zeros_like(q))
            s_w = _dot_nt(q_h, k_w) + bias_ref[0, half]
            s_c = _dot_nt(q_h, k_c)
            sw_ref[slot, half] = s_w
            sc_ref[slot, half] = s_c
            m = jnp.maximum(jnp.max(s_w, axis=-1, keepdims=True), jnp.max(s_c, axis=-1, keepdims=True))
            m_ref[slot, half] = jnp.broadcast_to(m, (nq, LANES))
        v_w = kv_v_ref[window(jnp.maximum(g - 1, 0)), :]
        v_c = kv_v_ref[0:CTX_LEN, :]
        outs = []
        for half in range(2):
            m = m_ref[prev, half]
            p_w = jnp.exp(sw_ref[prev, half] - jnp.concatenate([m] * (nw // LANES), axis=1))
            p_c = jnp.exp(sc_ref[prev, half] - jnp.concatenate([m] * (CTX_LEN // LANES), axis=1))
            l = jnp.sum(p_w, axis=-1, keepdims=True) + jnp.sum(p_c, axis=-1, keepdims=True)
            o = _dot(p_w.astype(BF16), v_w) + _dot(p_c.astype(BF16), v_c)
            outs.append(o * (1.0 / l))
        o_ref[...] = jnp.where(lane < NA_DH, outs[0], outs[1]).astype(o_ref.dtype)

    @pl.when(g % 2 == 0)
    def _():
        step(0, 1)

    @pl.when(g % 2 == 1)
    def _():
        step(1, 0)


def _na(q, k, v, bias, rows):
    t, d = q.shape
    s = t - CTX_LEN
    nq = NA_QROWS * GRID_W
    nw = NA_WROWS * GRID_W
    assert nq == ROW_TILE and rows >= NA_WROWS and rows % NA_QROWS == 0
    n_groups = rows // NA_QROWS
    cur = lambda g: jnp.minimum(g, n_groups - 1)
    variant = lambda g: jnp.where(g == 0, 0, jnp.where(g == n_groups - 1, 2, 1))
    return pl.pallas_call(
        functools.partial(_na_kernel, rows=rows, n_groups=n_groups),
        grid=(NA_HEADS // 2, n_groups + 1),
        in_specs=[pl.BlockSpec((nq, LANES), lambda p, g: (cur(g) + 1, p)),
                  pl.BlockSpec((t, LANES), lambda p, g: (0, p)),
                  pl.BlockSpec((t, LANES), lambda p, g: (0, p)),
                  pl.BlockSpec((1, 2, nq, nw), lambda p, g: (variant(cur(g)), p, 0, 0))],
        out_specs=pl.BlockSpec((nq, LANES), lambda p, g: (jnp.maximum(g - 1, 0), p)),
        out_shape=jax.ShapeDtypeStruct((s, d), BF16),
        scratch_shapes=[pltpu.VMEM((2, 2, nq, nw), F32), pltpu.VMEM((2, 2, nq, CTX_LEN), F32),
                        pltpu.VMEM((2, 2, nq, LANES), F32)],
        compiler_params=_cparams(("arbitrary", "arbitrary")),
        name="na",
    )(q, k, v, bias)


def _attn_out_kernel(*refs, gla):
    if gla:
        (x_ref, of_ref, ob_ref, r_ref, om_ref, on_ref, w_ref, g1_ref, gate_ref, sc_ref, sh_ref,
         rwh_ref, rwl_ref, rb_ref, xo_ref, h_ref, lg_ref) = refs
        o = of_ref[...] + ob_ref[...]
        r = r_ref[...]
        parts = []
        for hd in range(GLA_HEADS):
            sl = slice(hd * GLA_DV, (hd + 1) * GLA_DV)
            rh = r[:, sl]
            parts.append(_rms(o[:, sl], on_ref[...]) * (rh * (1.0 / (1.0 + jnp.exp(-rh)))))
        mix_a = jnp.concatenate(parts, axis=1).astype(BF16)
        y = _dot(mix_a, w_ref[0:GLA_VW, :]) + _dot(om_ref[...], w_ref[GLA_VW:, :])
    else:
        (x_ref, om_ref, w_ref, g1_ref, gate_ref, sc_ref, sh_ref,
         rwh_ref, rwl_ref, rb_ref, xo_ref, h_ref, lg_ref) = refs
        y = _dot(om_ref[...], w_ref[...])
    x = x_ref[...] + _rms(y, g1_ref[...]) * gate_ref[0]
    xo_ref[...] = x
    h = x * lax.rsqrt(jnp.mean(x * x, axis=-1, keepdims=True) + EPS) * sc_ref[0] + sh_ref[0]
    for c in range(h.shape[1] // LANES):
        h_ref[:, c, :] = h[:, c * LANES:(c + 1) * LANES]
    h_hi = h.astype(BF16)
    h_lo = (h - h_hi.astype(F32)).astype(BF16)
    lg_ref[...] = _dot(h_hi, rwh_ref[...]) + _dot(h_lo, rwh_ref[...]) + _dot(h_hi, rwl_ref[...]) + rb_ref[...]


def _attn_out(xs, mix_inputs, w_out, g1, gate, scale, shift, rw_hi, rw_lo, rb, *, gla, first_tile):
    d = xs.shape[1]
    tm = ROW_TILE
    t = mix_inputs[0].shape[0]
    row = lambda i: (i, 0)
    xrow = lambda i: (i + first_tile, 0)
    const = lambda i: (0, 0)
    strm = lambda i: (_stream(i + first_tile), 0, 0)
    if gla:
        of, ob, r, om, onorm = mix_inputs
        mix_specs = [pl.BlockSpec((tm, GLA_VW), row), pl.BlockSpec((tm, GLA_VW), row), pl.BlockSpec((tm, GLA_VW), row),
                     pl.BlockSpec((tm, MLA_VW), row), pl.BlockSpec(onorm.shape, const)]
    else:
        mix_specs = [pl.BlockSpec((tm, d), row)]
    return pl.pallas_call(
        functools.partial(_attn_out_kernel, gla=gla),
        grid=(t // tm,),
        in_specs=[pl.BlockSpec((tm, d), xrow)] + mix_specs + [
            pl.BlockSpec(w_out.shape, const), pl.BlockSpec(g1.shape, const),
            pl.BlockSpec((1, 1, d), strm), pl.BlockSpec((1, 1, d), strm), pl.BlockSpec((1, 1, d), strm),
            pl.BlockSpec(rw_hi.shape, const), pl.BlockSpec(rw_lo.shape, const), pl.BlockSpec(rb.shape, const)],
        out_specs=[pl.BlockSpec((tm, d), row), pl.BlockSpec((tm, d // LANES, LANES), lambda i: (i, 0, 0)),
                   pl.BlockSpec((tm, LANES), row)],
        out_shape=[jax.ShapeDtypeStruct((t, d), F32), jax.ShapeDtypeStruct((t, d // LANES, LANES), F32),
                   jax.ShapeDtypeStruct((t, LANES), F32)],
        compiler_params=_cparams(("arbitrary",)),
        name="attn_out",
    )(xs, *mix_inputs, w_out, g1, gate, scale, shift, rw_hi, rw_lo, rb)


def _moe_kernel(be_ref, tok_cur_ref, tok_nxt_ref, hp_ref, gate_ref, w1_ref, p_ref, b1_ref, w2_ref, b2_ref, o_ref,
                xa_ref, xb_ref, w1b_ref, w2b_ref, sem_ref):
    i = pl.program_id(0)
    last = pl.num_programs(0) - 1
    bm = o_ref.shape[0]
    f = w2_ref.shape[2]

    @pl.when((i == 0) | (be_ref[i] != be_ref[jnp.maximum(i - 1, 0)]))
    def _():
        for c in range(2 * f // (2 * LANES)):
            cols = slice(c * 2 * LANES, (c + 1) * 2 * LANES)
            w1b_ref[:, cols] = _dot(w1_ref[0, 0, :, cols].astype(BF16), p_ref[...]).astype(BF16)
        w2b_ref[...] = w2_ref[0, 0].astype(BF16)

    def row_copy(tok_ref, r, buf, sem):
        return pltpu.make_async_copy(hp_ref.at[pl.ds(tok_ref[0, 0, r], 1)], buf.at[pl.ds(r, 1)], sem)

    def wait_block(buf, sem):
        pltpu.make_async_copy(hp_ref.at[pl.ds(0, bm)], buf, sem).wait()

    @pl.when(i == 0)
    def _():
        for r in range(bm):
            row_copy(tok_cur_ref, r, xa_ref, sem_ref.at[0]).start(priority=r % 2)

    def step(cur, sem_cur, nxt, sem_nxt):
        wait_block(cur, sem_cur)
        x = jnp.concatenate([cur[:, c, :] for c in range(cur.shape[1])], axis=1).astype(BF16)
        n_grp = f // LANES
        d_out = o_ref.shape[1]
        n_out = d_out // (2 * LANES)
        per = bm // (n_grp + 2 * n_out)

        def start_rows(first, count, after):
            bits = lax.bitcast_convert_type(after[0, 0], jnp.uint32)
            zero = ((bits >> 16) >> 16).astype(jnp.int32)
            for r in range(first, first + count):
                pltpu.make_async_copy(hp_ref.at[pl.ds(tok_nxt_ref[0, 0, r] + zero, 1)], nxt.at[pl.ds(r, 1)],
                                      sem_nxt).start(priority=r % 2)

        acts = []
        for c in range(n_grp):
            u = _dot(x, w1b_ref[:, 2 * c * LANES:(2 * c + 2) * LANES]) \
                + b1_ref[0, 0, :, 2 * c * LANES:(2 * c + 2) * LANES]
            glu = jnp.minimum(u[:, :LANES], SWIGLU_LIMIT)
            lin = jnp.clip(u[:, LANES:], -SWIGLU_LIMIT, SWIGLU_LIMIT)
            acts.append((glu * (1.0 / (1.0 + jnp.exp(-SWIGLU_ALPHA * glu))) * (lin + 1.0)).astype(BF16))
            start_rows(c * per, per, u)
        act = jnp.concatenate(acts, axis=1)
        gate = gate_ref[...]
        for c in range(n_out):
            cols = slice(c * 2 * LANES, (c + 1) * 2 * LANES)
            y = _dot(act, w2b_ref[:, cols]) + b2_ref[0, 0, :, cols]
            o_ref[:, cols] = (y * gate).astype(o_ref.dtype)
            start_rows(n_grp * per + c * 2 * per, 2 * per, y)

        @pl.when(i == last)
        def _():
            wait_block(nxt, sem_nxt)

    @pl.when(i % 2 == 0)
    def _():
        step(xa_ref, sem_ref.at[0], xb_ref, sem_ref.at[1])

    @pl.when(i % 2 == 1)
    def _():
        step(xb_ref, sem_ref.at[1], xa_ref, sem_ref.at[0])


def _regroup_matrix():
    grp = 2 * LANES
    src = np.concatenate([np.arange(0, grp, 2), np.arange(1, grp, 2)])
    perm = np.zeros((grp, grp), np.float32)
    perm[src, np.arange(grp)] = 1.0
    return jnp.asarray(perm, BF16)


def _moe_experts(block_e, slot_tok, hp, slot_gate, w1, b1, w2, b2, layer):
    n_blocks, _, bm = slot_tok.shape
    row_shape = hp.shape[1:]
    d = row_shape[0] * row_shape[1]
    f = w2.shape[2]
    perm = _regroup_matrix()
    nxt = lambda i, be: (jnp.minimum(i + 1, n_blocks - 1), 0, 0)
    grid_spec = pltpu.PrefetchScalarGridSpec(
        num_scalar_prefetch=1,
        grid=(n_blocks,),
        in_specs=[pl.BlockSpec((1, 1, bm), lambda i, be: (i, 0, 0), memory_space=pltpu.SMEM),
                  pl.BlockSpec((1, 1, bm), nxt, memory_space=pltpu.SMEM),
                  pl.BlockSpec(memory_space=pl.ANY),
                  pl.BlockSpec((bm, 1), lambda i, be: (i, 0)),
                  pl.BlockSpec((1, 1, d, 2 * f), lambda i, be: (layer, be[i], 0, 0)),
                  pl.BlockSpec(perm.shape, lambda i, be: (0, 0)),
                  pl.BlockSpec((1, 1, 1, 2 * f), lambda i, be: (layer, be[i], 0, 0)),
                  pl.BlockSpec((1, 1, f, d), lambda i, be: (layer, be[i], 0, 0)),
                  pl.BlockSpec((1, 1, 1, d), lambda i, be: (layer, be[i], 0, 0))],
        out_specs=pl.BlockSpec((bm, d), lambda i, be: (i, 0)),
        scratch_shapes=[pltpu.VMEM((bm,) + row_shape, F32), pltpu.VMEM((bm,) + row_shape, F32),
                        pltpu.VMEM((d, 2 * f), BF16), pltpu.VMEM((f, d), BF16), pltpu.SemaphoreType.DMA((2,))],
    )
    return pl.pallas_call(
        _moe_kernel,
        grid_spec=grid_spec,
        out_shape=jax.ShapeDtypeStruct((n_blocks * bm, d), BF16),
        compiler_params=_cparams(("arbitrary",)),
        name="moe_experts",
    )(block_e, slot_tok, slot_tok, hp, slot_gate, w1, perm, b1, w2, b2)


def _moe(hp, logits, w1, b1, w2, b2, layer):
    t = hp.shape[0]
    bm = MOE_ROWS
    top_val, top_idx = lax.top_k(logits, TOP_K)
    gates = jax.nn.softmax(top_val, axis=-1)
    n_assign = t * TOP_K
    n_blocks = -(-n_assign // bm) + N_EXPERTS
    experts = jnp.arange(N_EXPERTS, dtype=jnp.int32)
    hit = jnp.any(top_idx[:, :, None] == experts, axis=1).astype(jnp.int32)
    counts = jnp.sum(hit, axis=0)
    before = jnp.cumsum(hit, axis=0) - hit
    padded = ((counts + bm - 1) // bm) * bm
    pad_end = jnp.cumsum(padded)
    pad_start = pad_end - padded
    start = jnp.cumsum(counts) - counts
    slot_of = jnp.take_along_axis(before + pad_start[None, :], top_idx, axis=1)
    block_first = jnp.arange(n_blocks, dtype=jnp.int32) * bm
    block_e = jnp.minimum(jnp.sum(block_first[:, None] >= pad_end[None, :], axis=1), N_EXPERTS - 1).astype(jnp.int32)
    order = jnp.argsort(top_idx.reshape(-1))
    local = (block_first - pad_start[block_e])[:, None] + jnp.arange(bm, dtype=jnp.int32)[None, :]
    valid = local < counts[block_e][:, None]
    assign = order[jnp.clip(start[block_e][:, None] + local, 0, n_assign - 1)]
    slot_tok = jnp.where(valid, assign // TOP_K, 0).astype(jnp.int32)
    slot_gate = jnp.where(valid, gates.reshape(-1)[assign], 0.0)
    yb = _moe_experts(block_e, slot_tok.reshape(n_blocks, 1, bm), hp, slot_gate.reshape(n_blocks * bm, 1),
                      w1, b1, w2, b2, layer)
    return yb[slot_of.T.reshape(-1)].reshape(TOP_K, t, yb.shape[1])


def _moe_biases(moe_b1, moe_b2):
    depth, e, f2 = moe_b1.shape
    b1p = jnp.transpose(moe_b1.reshape(depth, e, f2 // (2 * LANES), LANES, 2), (0, 1, 2, 4, 3))
    return b1p.reshape(depth, e, 1, f2), moe_b2.reshape(depth, e, 1, moe_b2.shape[2])


def _resid_kernel(*refs, project):
    if project:
        x_ref, y_ref, g3_ref, gate_ref, sc_ref, sh_ref, w_ref, xo_ref, q_ref, k_ref, v_ref = refs
    else:
        x_ref, y_ref, g3_ref, gate_ref, xo_ref = refs
    y = y_ref[0].astype(F32)
    for j in range(1, TOP_K):
        y = y + y_ref[j].astype(F32)
    x = x_ref[...] + _rms(y, g3_ref[...]) * gate_ref[0]
    xo_ref[...] = x
    if project:
        d = x.shape[1]
        h = x * lax.rsqrt(jnp.mean(x * x, axis=-1, keepdims=True) + EPS) * sc_ref[0] + sh_ref[0]
        u = _dot(h.astype(BF16), w_ref[...])
        q_ref[...] = u[:, 0:d].astype(BF16)
        k_ref[...] = u[:, d:2 * d].astype(BF16)
        v_ref[...] = u[:, 2 * d:3 * d].astype(BF16)


def _resid(xs, y, g3, gate, scale=None, shift=None, w=None, *, first_tile=0):
    t, d = xs.shape
    tm = ROW_TILE
    project = w is not None
    row = lambda i: (i, 0)
    const = lambda i: (0, 0)
    strm = lambda i: (_stream(i + first_tile), 0, 0)
    vec = pl.BlockSpec((1, 1, d), strm)
    in_specs = [pl.BlockSpec((tm, d), row), pl.BlockSpec((TOP_K, tm, d), lambda i: (0, i, 0)),
                pl.BlockSpec(g3.shape, const), vec]
    out_specs = [pl.BlockSpec((tm, d), row)]
    out_shape = [jax.ShapeDtypeStruct((t, d), F32)]
    args = [xs, y, g3, gate]
    if project:
        in_specs += [vec, vec, pl.BlockSpec(w.shape, const)]
        out_specs += [pl.BlockSpec((tm, d), row)] * 3
        out_shape += [jax.ShapeDtypeStruct((t, d), BF16)] * 3
        args += [scale, shift, w]
    return pl.pallas_call(
        functools.partial(_resid_kernel, project=project),
        grid=(t // tm,),
        in_specs=in_specs, out_specs=out_specs, out_shape=out_shape,
        compiler_params=_cparams(("arbitrary",)),
        name="resid",
    )(*args)


def _rope_tables(s):
    tok = jnp.arange(s)
    row = (tok // GRID_W).astype(F32)
    col = (tok % GRID_W).astype(F32)
    n_freq = MLA_ROPE // 4
    inv = ROPE_THETA ** (-jnp.arange(n_freq, dtype=F32) / n_freq)
    ang = jnp.concatenate([row[:, None] * inv, col[:, None] * inv], axis=-1)
    cos, sin = jnp.cos(ang), jnp.sin(ang)
    cos_t = jnp.concatenate([cos, cos, cos, cos], axis=-1)
    sin_t = jnp.concatenate([-sin, sin, -sin, sin], axis=-1)
    cos_t = jnp.concatenate([jnp.ones((CTX_LEN, LANES), F32), cos_t], axis=0)
    sin_t = jnp.concatenate([jnp.zeros((CTX_LEN, LANES), F32), sin_t], axis=0)
    return cos_t, sin_t


def _pad_cols(w, n):
    return jnp.pad(w, ((0, 0), (0, n - w.shape[1])))


def _swap_halves(w):
    half = w.shape[-1] // 2
    return jnp.concatenate([w[..., half:], w[..., :half]], axis=-1)


def _proj0_weights(in_w, wa_f, ba_f, wa_b, ba_b, wuq, wukv):
    o = np.cumsum((0, GLA_QK, GLA_QK, GLA_VW, GLA_VW, GLA_GATE_RANK, GLA_GATE_RANK, MLA_Q_LORA, MLA_KV_LORA,
                   MLA_ROPE))
    q, k, v, r, a_f, a_b, cq, ckv, kr = [in_w[:, o[i]:o[i + 1]] for i in range(9)]
    w_in = jnp.concatenate([q * (GLA_DK ** -0.5), k, v, r, cq, ckv, _pad_cols(kr, LANES),
                            _pad_cols(_swap_halves(kr), LANES),
                            _pad_cols(jnp.concatenate([a_f, a_b], axis=1), LANES)], axis=1).astype(BF16)
    wa = jnp.zeros((LANES, 2 * GLA_QK), F32)
    wa = wa.at[0:GLA_GATE_RANK, 0:GLA_QK].set(wa_f).at[GLA_GATE_RANK:2 * GLA_GATE_RANK, GLA_QK:].set(wa_b)
    ba = jnp.concatenate([ba_f, ba_b]).reshape(1, 2 * GLA_QK)
    wq = wuq.reshape(MLA_Q_LORA, MLA_HEADS, MLA_NOPE + MLA_ROPE)
    nope = wq[:, :, :MLA_NOPE].reshape(MLA_Q_LORA, MLA_HEADS * MLA_NOPE)
    rope = wq[:, :, MLA_NOPE:]
    pad3 = lambda z: jnp.pad(z, ((0, 0), (0, 0), (0, LANES - MLA_ROPE))).reshape(MLA_Q_LORA, MLA_HEADS * LANES)
    wuq_x = jnp.concatenate([nope, pad3(rope), pad3(_swap_halves(rope))], axis=1).astype(BF16)
    wkv = wukv.reshape(MLA_KV_LORA, MLA_HEADS, MLA_NOPE + MLA_V)
    wukv_x = jnp.concatenate([wkv[:, :, :MLA_NOPE].reshape(MLA_KV_LORA, -1),
                              wkv[:, :, MLA_NOPE:].reshape(MLA_KV_LORA, -1)], axis=1).astype(BF16)
    return w_in, wa.astype(BF16), ba, wuq_x, wukv_x


def _router_weights(rw, rb):
    rw = _pad_cols(rw, LANES)
    hi = rw.astype(BF16)
    lo = (rw - hi.astype(F32)).astype(BF16)
    return hi, lo, _pad_cols(rb.reshape(1, -1), LANES)


def kernel(x, c, ctx, c_ctx, ada_w, ada_b, norm_g, router_w, router_b, moe_w1, moe_b1, moe_w2, moe_b2, ab_in_w,
           gla_wa_f, gla_ba_f, gla_wa_b, gla_ba_b, gla_onorm, mla_qnorm, mla_wuq, mla_kvnorm, mla_wukv, ab_out_w,
           na_qkv_w, na_rpb, na_out_w):
    b, s, d = x.shape
    assert b == 1 and d == D_MODEL and ctx.shape == (1, CTX_LEN, d) and s % (NA_QROWS * GRID_W) == 0
    rows = s // GRID_W
    xs = jnp.concatenate([ctx[0], x[0]], axis=0)

    cc = jnp.zeros((8, d), F32).at[0].set(c[0]).at[1].set(c_ctx)
    mod = _ada(cc, ada_w, ada_b)

    def vectors(l):
        parts = jnp.split(mod[l, 0:2][::-1], 6, axis=-1)
        sh1, sc1, g1, sh2, sc2, g2 = [p.reshape(2, 1, d) for p in parts]
        return ((1 + sc1) * norm_g[l, 0], sh1, g1, (1 + sc2) * norm_g[l, 2], sh2, g2)

    sc1, sh1, g1, sc2, sh2, g2 = vectors(0)
    cos_t, sin_t = _rope_tables(s)
    w_in, wa, ba, wuq_x, wukv_x = _proj0_weights(ab_in_w[0], gla_wa_f[0], gla_ba_f[0], gla_wa_b[0], gla_ba_b[0],
                                                 mla_wuq[0], mla_wukv[0])
    qg, kg, vg, r, gf, gb, qm, km, vm = _proj0(xs, sc1, sh1, cos_t, sin_t, w_in, wa, ba,
                                               mla_qnorm[0].reshape(1, -1), wuq_x,
                                               mla_kvnorm[0].reshape(1, -1), wukv_x)
    o_f, o_b = _gla(qg, kg, vg, gf, gb)
    om_l = _flash(qm[:, CTX_LEN:], km, vm, FLASH_TQ, FLASH_TK)
    om_c = _flash(qm[:, :CTX_LEN], km[:, :CTX_LEN], vm[:, :CTX_LEN], CTX_LEN, CTX_LEN)
    om = jnp.concatenate([om_c, om_l], axis=0)
    rw_hi, rw_lo, rb = _router_weights(router_w[0], router_b[0])
    xs, h2, logits = _attn_out(xs, (o_f, o_b, r, om, gla_onorm[0].reshape(1, -1)), ab_out_w[0].astype(BF16),
                               norm_g[0, 1].reshape(1, -1), g1, sc2, sh2, rw_hi, rw_lo, rb, gla=True, first_tile=0)
    b1p, b2r = _moe_biases(moe_b1, moe_b2)
    y = _moe(h2, logits[:, :N_EXPERTS], moe_w1, b1p, moe_w2, b2r, 0)

    sc1n, sh1n, g1n, sc2n, sh2n, g2n = vectors(1)
    qkv_w = jnp.concatenate([na_qkv_w[0][:, :d] * (NA_DH ** -0.5), na_qkv_w[0][:, d:]], axis=1).astype(BF16)
    xs, q, k, v = _resid(xs, y, norm_g[0, 3].reshape(1, -1), g2, sc1n, sh1n, qkv_w)
    o_na = _na(q, k, v, _na_bias_tiles(na_rpb[0], rows), rows)
    rw_hi, rw_lo, rb = _router_weights(router_w[1], router_b[1])
    x_l, h2, logits = _attn_out(xs, (o_na,), na_out_w[0].astype(BF16), norm_g[1, 1].reshape(1, -1), g1n, sc2n, sh2n,
                                rw_hi, rw_lo, rb, gla=False, first_tile=1)
    y = _moe(h2, logits[:, :N_EXPERTS], moe_w1, b1p, moe_w2, b2r, 1)
    (x_l,) = _resid(x_l, y, norm_g[1, 3].reshape(1, -1), g2n, first_tile=1)
    return x_l.reshape(1, s, d)
```

```python
import functools

import numpy as np
import jax
import jax.numpy as jnp
from jax import lax
from jax.experimental import pallas as pl
from jax.experimental.pallas import tpu as pltpu

F32 = jnp.float32
BF16 = jnp.bfloat16

D_MODEL = 1024
DEPTH = 2
CTX_LEN = 256
GRID_W = 64
EPS = 1e-6
ROPE_THETA = 10000.0

GLA_HEADS = 4
GLA_DK = 64
GLA_DV = 128
GLA_GATE_RANK = 16
GLA_TAU = 16.0
GLA_QK = GLA_HEADS * GLA_DK
GLA_VW = GLA_HEADS * GLA_DV

MLA_HEADS = 4
MLA_Q_LORA = 256
MLA_KV_LORA = 128
MLA_NOPE = 128
MLA_ROPE = 64
MLA_V = 128
MLA_VW = MLA_HEADS * MLA_V
MLA_QK_PAD = 256

NA_HEADS = 16
NA_DH = D_MODEL // NA_HEADS
NA_KH = 8
NA_KW = 16
NA_QROWS = 4
NA_WROWS = NA_QROWS + NA_KH

N_EXPERTS = 32
TOP_K = 4
SWIGLU_LIMIT = 7.0
SWIGLU_ALPHA = 1.702

LANES = 128
ROW_TILE = 256
MOE_ROWS = 256
VMEM_LIMIT = 56 * 1024 * 1024
NEG_BIG = -1e30

GLA_CHUNK = 64
GLA_SUB = 16
FLASH_TQ = 512
FLASH_TK = 3328
FLASH_SUB = 256


def _cparams(sem):
    return pltpu.CompilerParams(dimension_semantics=sem, vmem_limit_bytes=VMEM_LIMIT)


def _dot(a, b):
    return jnp.dot(a, b, preferred_element_type=F32)


def _dot_nt(a, b):
    return lax.dot_general(a, b, (((1,), (1,)), ((), ())), preferred_element_type=F32)


def _rms(x, g):
    return x * lax.rsqrt(jnp.mean(x * x, axis=-1, keepdims=True) + EPS) * g


def _stream(i):
    return jnp.where(i == 0, 0, 1)


def _ada_kernel(c_ref, w_ref, b_ref, o_ref):
    c = c_ref[...]
    s = (c * (1.0 / (1.0 + jnp.exp(-c)))).astype(BF16)
    o_ref[0] = _dot(s, w_ref[0].astype(BF16)) + b_ref[0]


def _ada(cc, ada_w, ada_b):
    depth, d, n = ada_w.shape
    tn = 1536
    return pl.pallas_call(
        _ada_kernel,
        grid=(depth, n // tn),
        in_specs=[pl.BlockSpec((8, d), lambda l, j: (0, 0)),
                  pl.BlockSpec((1, d, tn), lambda l, j: (l, 0, j)),
                  pl.BlockSpec((1, 1, tn), lambda l, j: (l, 0, j))],
        out_specs=pl.BlockSpec((1, 8, tn), lambda l, j: (l, 0, j)),
        out_shape=jax.ShapeDtypeStruct((depth, 8, n), F32),
        compiler_params=_cparams(("arbitrary", "arbitrary")),
        name="ada",
    )(cc, ada_w, ada_b.reshape(depth, 1, n))


def _proj0_kernel(x_ref, sc_ref, sh_ref, cos_ref, sin_ref, win_ref, wa_ref, ba_ref, qn_ref, wuq_ref,
                  kvn_ref, wukv_ref,
                  qg_ref, kg_ref, vg_ref, r_ref, gf_ref, gb_ref, qm_ref, km_ref, vm_ref, *, mla_scale):
    x = x_ref[...]
    h = x * lax.rsqrt(jnp.mean(x * x, axis=-1, keepdims=True) + EPS) * sc_ref[0] + sh_ref[0]
    u = _dot(h.astype(BF16), win_ref[...])
    qg_ref[...] = u[:, 0:256]
    kg_ref[...] = u[:, 256:512]
    vg_ref[...] = u[:, 512:1024].astype(BF16)
    r_ref[...] = u[:, 1024:1536]
    cq = u[:, 1536:1792]
    ckv = u[:, 1792:1920]
    kr = u[:, 1920:2048]
    krs = u[:, 2048:2176]
    a = u[:, 2176:2304]

    ga = _dot(a.astype(BF16), wa_ref[...]) + ba_ref[...]
    la = (jnp.minimum(ga, 0.0) - jnp.log1p(jnp.exp(-jnp.abs(ga)))) * (1.0 / GLA_TAU)
    gf_ref[...] = la[:, :GLA_QK]
    gb_ref[...] = la[:, GLA_QK:]

    cos = cos_ref[...]
    sin = sin_ref[...]
    qm = _dot(_rms(cq, qn_ref[...]).astype(BF16), wuq_ref[...])
    kvm = _dot(_rms(ckv, kvn_ref[...]).astype(BF16), wukv_ref[...])
    kroped = (kr * cos + krs * sin).astype(BF16)
    for hd in range(MLA_HEADS):
        nope = qm[:, hd * 128:(hd + 1) * 128]
        rp = qm[:, 512 + hd * 128:512 + (hd + 1) * 128]
        rps = qm[:, 1024 + hd * 128:1024 + (hd + 1) * 128]
        roped = rp * cos + rps * sin
        qm_ref[hd] = (jnp.concatenate([nope, roped], axis=1) * mla_scale).astype(BF16)
        km_ref[hd] = jnp.concatenate([kvm[:, hd * 128:(hd + 1) * 128].astype(BF16), kroped], axis=1)
        v_h = kvm[:, 512 + hd * 128:512 + (hd + 1) * 128].astype(BF16)
        vm_ref[hd] = jnp.concatenate([v_h, jnp.ones_like(v_h)], axis=1)


def _proj0(xs, scale, shift, cos_t, sin_t, w_in, wa, ba, qn, wuq, kvn, wukv):
    t, d = xs.shape
    tm = ROW_TILE
    row = lambda i: (i, 0)
    const = lambda i: (0, 0)
    strm = lambda i: (_stream(i), 0, 0)
    hrow = lambda i: (0, i, 0)
    kern = functools.partial(_proj0_kernel, mla_scale=float((MLA_NOPE + MLA_ROPE) ** -0.5 * np.log2(np.e)))
    return pl.pallas_call(
        kern,
        grid=(t // tm,),
        in_specs=[pl.BlockSpec((tm, d), row),
                  pl.BlockSpec((1, 1, d), strm), pl.BlockSpec((1, 1, d), strm),
                  pl.BlockSpec((tm, LANES), row), pl.BlockSpec((tm, LANES), row),
                  pl.BlockSpec(w_in.shape, const), pl.BlockSpec(wa.shape, const), pl.BlockSpec(ba.shape, const),
                  pl.BlockSpec(qn.shape, const), pl.BlockSpec(wuq.shape, const),
                  pl.BlockSpec(kvn.shape, const), pl.BlockSpec(wukv.shape, const)],
        out_specs=[pl.BlockSpec((tm, GLA_QK), row), pl.BlockSpec((tm, GLA_QK), row),
                   pl.BlockSpec((tm, GLA_VW), row), pl.BlockSpec((tm, GLA_VW), row),
                   pl.BlockSpec((tm, GLA_QK), row), pl.BlockSpec((tm, GLA_QK), row),
                   pl.BlockSpec((MLA_HEADS, tm, MLA_QK_PAD), hrow),
                   pl.BlockSpec((MLA_HEADS, tm, MLA_QK_PAD), hrow),
                   pl.BlockSpec((MLA_HEADS, tm, 2 * MLA_V), hrow)],
        out_shape=[jax.ShapeDtypeStruct((t, GLA_QK), F32), jax.ShapeDtypeStruct((t, GLA_QK), F32),
                   jax.ShapeDtypeStruct((t, GLA_VW), BF16), jax.ShapeDtypeStruct((t, GLA_VW), F32),
                   jax.ShapeDtypeStruct((t, GLA_QK), F32), jax.ShapeDtypeStruct((t, GLA_QK), F32),
                   jax.ShapeDtypeStruct((MLA_HEADS, t, MLA_QK_PAD), BF16),
                   jax.ShapeDtypeStruct((MLA_HEADS, t, MLA_QK_PAD), BF16),
                   jax.ShapeDtypeStruct((MLA_HEADS, t, 2 * MLA_V), BF16)],
        compiler_params=_cparams(("arbitrary",)),
        name="proj0",
    )(xs, scale, shift, cos_t, sin_t, w_in, wa, ba, qn, wuq, kvn, wukv)


def _gla_tables():
    c, s = GLA_CHUNK, GLA_SUB
    t = np.arange(c)[:, None]
    u = np.arange(c)[None, :]
    bt, bu = t // s, u // s
    half = (c // s) // 2
    mats = [(bu == bt) & (u <= t), (bu == bt) & (u > t), bu < bt, bu > bt,
            (bu >= half) & (bu < bt), (bu > bt) & (bu < half)]
    m = np.concatenate([x.astype(np.float32) for x in mats], axis=0)
    sel = np.stack([(bt == bu) & (u <= t), bt == bu + 1, (bt >= half) & (bu < half)]).astype(np.float32)
    m_b = np.concatenate([x.astype(np.float32)[::-1, ::-1] for x in mats], axis=0)
    sel_b = sel[:, ::-1, ::-1]
    return np.stack([m, m_b]), np.stack([sel, sel_b])


def _gla_chunk(q, k, g, v_ref, vt_ref, j, m, sel, st_ref, o_ref):
    rows = pl.ds(j * GLA_CHUNK, GLA_CHUNK)
    c = GLA_CHUNK
    g_hi = g.astype(BF16)
    g_r1 = g - g_hi.astype(F32)
    g_mid = g_r1.astype(BF16)
    g_lo = (g_r1 - g_mid.astype(F32)).astype(BF16)
    cs = _dot(m, g_hi) + _dot(m, g_mid) + _dot(m, g_lo)
    own, rest, before, after = cs[0:c], cs[c:2 * c], cs[2 * c:3 * c], cs[3 * c:4 * c]
    far_q, far_k = cs[4 * c:5 * c], cs[5 * c:6 * c]
    qd = q * jnp.exp(own)
    kd = k * jnp.exp(-own)
    kh = k * jnp.exp(rest)
    q_in = qd * jnp.exp(before)
    k_out = kh * jnp.exp(after)
    q_far = qd * jnp.exp(far_q)
    k_far = kh * jnp.exp(far_k)
    decay = jnp.exp(own + rest + before + after)
    lane = lax.broadcasted_iota(jnp.int32, (c, LANES), 1)
    s_diag, s_adj, s_far = sel[0] > 0.5, sel[1] > 0.5, sel[2] > 0.5
    for p in range(GLA_HEADS // 2):
        lo, hi = p * LANES, (p + 1) * LANES
        kd_p = kd[:, lo:hi].astype(BF16)
        kh_p = kh[:, lo:hi].astype(BF16)
        kf_p = k_far[:, lo:hi].astype(BF16)
        st = st_ref[p]
        st_b = st.astype(BF16)
        upd = st * decay[0:1, lo:hi]
        for half in range(2):
            hd = 2 * p + half
            msk = (lane < GLA_DK) if half == 0 else (lane >= GLA_DK)
            pick = lambda z: jnp.where(msk, z[:, lo:hi], 0.0).astype(BF16)
            qd_h = pick(qd)
            a = jnp.where(s_diag, _dot_nt(qd_h, kd_p),
                          jnp.where(s_adj, _dot_nt(qd_h, kh_p),
                                    jnp.where(s_far, _dot_nt(pick(q_far), kf_p), 0.0)))
            v_h = v_ref[rows, hd * GLA_DV:(hd + 1) * GLA_DV]
            o_ref[rows, hd * GLA_DV:(hd + 1) * GLA_DV] = _dot(a.astype(BF16), v_h) + _dot_nt(pick(q_in), st_b)
            upd = upd + _dot(vt_ref[j, hd * GLA_DV:(hd + 1) * GLA_DV, :], pick(k_out))
        st_ref[p] = upd


def _gla_kernel(qf_ref, kf_ref, gf_ref, vf_ref, vtf_ref, qb_ref, kb_ref, gb_ref, vb_ref, vtb_ref, m_ref, sel_ref,
                of_ref, ob_ref, stf_ref, stb_ref):
    @pl.when(pl.program_id(0) == 0)
    def _():
        stf_ref[...] = jnp.zeros_like(stf_ref)
        stb_ref[...] = jnp.zeros_like(stb_ref)

    n = ROW_TILE // GLA_CHUNK
    for j in range(n):
        rows = pl.ds(j * GLA_CHUNK, GLA_CHUNK)
        _gla_chunk(qf_ref[rows, :], kf_ref[rows, :], gf_ref[rows, :], vf_ref, vtf_ref, j,
                   m_ref[0], sel_ref[0], stf_ref, of_ref)
        jb = n - 1 - j
        rows = pl.ds(jb * GLA_CHUNK, GLA_CHUNK)
        _gla_chunk(qb_ref[rows, :], kb_ref[rows, :], gb_ref[rows, :], vb_ref, vtb_ref, jb,
                   m_ref[1], sel_ref[1], stb_ref, ob_ref)


def _gla(q, k, v, gf, gb):
    t = q.shape[0]
    tm = ROW_TILE
    nb = t // tm
    n = tm // GLA_CHUNK
    m_np, sel_np = _gla_tables()
    m = jnp.asarray(m_np, BF16)
    sel = jnp.asarray(sel_np, F32)
    vt = jnp.transpose(v.reshape(t // GLA_CHUNK, GLA_CHUNK, GLA_VW), (0, 2, 1))
    fwd = lambda i: (i, 0)
    bwd = lambda i: (jnp.where(i == 0, 0, nb - i), 0)
    fwd_t = lambda i: (i, 0, 0)
    bwd_t = lambda i: (jnp.where(i == 0, 0, nb - i), 0, 0)
    c3 = lambda i: (0, 0, 0)
    c4 = lambda i: (0, 0, 0, 0)
    spec = lambda w, im: pl.BlockSpec((tm, w), im)
    return pl.pallas_call(
        _gla_kernel,
        grid=(nb,),
        in_specs=[spec(GLA_QK, fwd), spec(GLA_QK, fwd), spec(GLA_QK, fwd), spec(GLA_VW, fwd),
                  pl.BlockSpec((n, GLA_VW, GLA_CHUNK), fwd_t),
                  spec(GLA_QK, bwd), spec(GLA_QK, bwd), spec(GLA_QK, bwd), spec(GLA_VW, bwd),
                  pl.BlockSpec((n, GLA_VW, GLA_CHUNK), bwd_t),
                  pl.BlockSpec(m.shape, c3), pl.BlockSpec(sel.shape, c4)],
        out_specs=[spec(GLA_VW, fwd), spec(GLA_VW, bwd)],
        out_shape=[jax.ShapeDtypeStruct((t, GLA_VW), F32), jax.ShapeDtypeStruct((t, GLA_VW), F32)],
        scratch_shapes=[pltpu.VMEM((GLA_HEADS // 2, GLA_DV, LANES), F32),
                        pltpu.VMEM((GLA_HEADS // 2, GLA_DV, LANES), F32)],
        compiler_params=_cparams(("arbitrary",)),
        name="gla",
    )(q, k, gf, v, vt, q, k, gb, v, vt, m, sel)


def _flash_kernel(q_ref, k_ref, v_ref, o_ref, s_ref, mc_ref, m_ref, acc_ref, *, tk, n_chunks):
    m_ref[...] = jnp.full_like(m_ref, -jnp.inf)
    acc_ref[...] = jnp.zeros_like(acc_ref)
    q = q_ref[0]
    tq = q.shape[0]
    sub = min(tk, FLASH_SUB)
    dv = o_ref.shape[1]

    def scores(j, slot):
        off = pl.multiple_of(j * tk, tk)
        s = _dot_nt(q, k_ref[0, pl.ds(off, tk), :])
        s_ref[slot] = s
        mc_ref[slot] = jnp.broadcast_to(jnp.max(s, axis=-1, keepdims=True), (tq, LANES))

    def softmax_pv(j, slot):
        off = pl.multiple_of(j * tk, tk)
        m_old = m_ref[...]
        m_new = jnp.maximum(m_old, mc_ref[slot])
        alpha = jnp.exp2(m_old - m_new)
        m_ref[...] = m_new
        acc = jnp.concatenate([alpha, alpha], axis=1) * acc_ref[...]
        for c in range(tk // sub):
            ps = [jnp.exp2(s_ref[slot, :, c * sub + a * LANES:c * sub + (a + 1) * LANES] - m_new).astype(BF16)
                  for a in range(sub // LANES)]
            acc = acc + _dot(jnp.concatenate(ps, axis=1), v_ref[0, pl.ds(off + c * sub, sub), :])
        acc_ref[...] = acc

    scores(0, 0)

    def pair(i, carry):
        scores(2 * i + 1, 1)
        softmax_pv(2 * i, 0)
        scores(2 * i + 2, 0)
        softmax_pv(2 * i + 1, 1)
        return carry

    n_pairs = (n_chunks - 1) // 2
    lax.fori_loop(0, n_pairs, pair, 0)
    if n_chunks % 2 == 0:
        scores(n_chunks - 1, 1)
        softmax_pv(n_chunks - 2, 0)
        softmax_pv(n_chunks - 1, 1)
    else:
        softmax_pv(n_chunks - 1, 0)
    o_ref[...] = (acc_ref[:, :dv] * (1.0 / acc_ref[:, dv:])).astype(o_ref.dtype)


def _flash(q, k, v, tq, tk):
    h, s, dq = q.shape
    t, dv = k.shape[1], v.shape[2] // 2
    assert s % tq == 0 and t % tk == 0
    kern = functools.partial(_flash_kernel, tk=tk, n_chunks=t // tk)
    return pl.pallas_call(
        kern,
        grid=(h, s // tq),
        in_specs=[pl.BlockSpec((1, tq, dq), lambda hh, i: (hh, i, 0)),
                  pl.BlockSpec((1, t, dq), lambda hh, i: (hh, 0, 0), pipeline_mode=pl.Buffered(1)),
                  pl.BlockSpec((1, t, 2 * dv), lambda hh, i: (hh, 0, 0), pipeline_mode=pl.Buffered(1))],
        out_specs=pl.BlockSpec((tq, dv), lambda hh, i: (i, hh)),
        out_shape=jax.ShapeDtypeStruct((s, h * dv), BF16),
        scratch_shapes=[pltpu.VMEM((2, tq, tk), F32), pltpu.VMEM((2, tq, LANES), F32),
                        pltpu.VMEM((tq, LANES), F32), pltpu.VMEM((tq, 2 * dv), F32)],
        compiler_params=_cparams(("arbitrary", "arbitrary")),
        name="flash",
    )(q, k, v)


def _na_bias_tiles(rpb, rows):
    w = GRID_W
    qc = np.arange(w)[:, None]
    kc = np.arange(w)[None, :]
    c0 = np.clip(qc - NA_KW // 2, 0, w - NA_KW)
    col_ok = (kc >= c0) & (kc < c0 + NA_KW)
    dc = np.clip(kc - qc + NA_KW - 1, 0, 2 * NA_KW - 2)
    onehot = (np.arange(2 * NA_KW - 1)[:, None, None] == dc[None]).astype(np.float32)
    e1 = jnp.einsum('hrd,dqk->hrqk', rpb.astype(F32), jnp.asarray(onehot), precision=lax.Precision.HIGHEST)
    e1 = jnp.where(jnp.asarray(col_ok)[None, None], e1, NEG_BIG)
    e1 = jnp.concatenate([e1, jnp.full((NA_HEADS, 1, w, w), NEG_BIG, F32)], axis=1)
    idx = np.zeros((3, NA_QROWS, NA_WROWS), np.int32)
    for vi, (r_first, ks) in enumerate([(0, 0), (NA_QROWS, 0), (rows - NA_QROWS, rows - NA_WROWS)]):
        for a in range(NA_QROWS):
            qr = r_first + a
            r0 = min(max(qr - NA_KH // 2, 0), rows - NA_KH)
            for b in range(NA_WROWS):
                kr = ks + b
                idx[vi, a, b] = kr - qr + NA_KH - 1 if r0 <= kr < r0 + NA_KH else 2 * NA_KH - 1
    tiles = jnp.take(e1, jnp.asarray(idx.reshape(-1)), axis=1)
    tiles = tiles.reshape(NA_HEADS, 3, NA_QROWS, NA_WROWS, w, w)
    return jnp.transpose(tiles, (1, 0, 2, 4, 3, 5)).reshape(3, NA_HEADS, NA_QROWS * w, NA_WROWS * w)


def _na_kernel(q_ref, kv_k_ref, kv_v_ref, bias_ref, o_ref, sw_ref, sc_ref, m_ref, *, rows, n_groups):
    g = pl.program_id(1)
    nq = NA_QROWS * GRID_W
    nw = NA_WROWS * GRID_W
    lane = lax.broadcasted_iota(jnp.int32, (nq, LANES), 1)

    def window(grp):
        ks = jnp.clip(grp * NA_QROWS - NA_KH // 2, 0, rows - NA_WROWS)
        return pl.ds(pl.multiple_of(CTX_LEN + ks * GRID_W, GRID_W), nw)

    @pl.when(g == 0)
    def _():
        sw_ref[1] = jnp.zeros(sw_ref.shape[1:], F32)
        sc_ref[1] = jnp.zeros(sc_ref.shape[1:], F32)
        m_ref[1] = jnp.zeros(m_ref.shape[1:], F32)

    def step(slot, prev):
        q = q_ref[...]
        k_w = kv_k_ref[window(jnp.minimum(g, n_groups - 1)), :]
        k_c = kv_k_ref[0:CTX_LEN, :]
        for half in range(2):
            msk = (lane < NA_DH) if half == 0 else (lane >= NA_DH)
            q_h = jnp.where(msk, q, jnp.zeros_like(q))
            s_w = _dot_nt(q_h, k_w) + bias_ref[0, half]
            s_c = _dot_nt(q_h, k_c)
            sw_ref[slot, half] = s_w
            sc_ref[slot, half] = s_c
            m = jnp.maximum(jnp.max(s_w, axis=-1, keepdims=True), jnp.max(s_c, axis=-1, keepdims=True))
            m_ref[slot, half] = jnp.broadcast_to(m, (nq, LANES))
        v_w = kv_v_ref[window(jnp.maximum(g - 1, 0)), :]
        v_c = kv_v_ref[0:CTX_LEN, :]
        outs = []
        for half in range(2):
            m = m_ref[prev, half]
            p_w = jnp.exp(sw_ref[prev, half] - jnp.concatenate([m] * (nw // LANES), axis=1))
            p_c = jnp.exp(sc_ref[prev, half] - jnp.concatenate([m] * (CTX_LEN // LANES), axis=1))
            l = jnp.sum(p_w, axis=-1, keepdims=True) + jnp.sum(p_c, axis=-1, keepdims=True)
            o = _dot(p_w.astype(BF16), v_w) + _dot(p_c.astype(BF16), v_c)
            outs.append(o * (1.0 / l))
        o_ref[...] = jnp.where(lane < NA_DH, outs[0], outs[1]).astype(o_ref.dtype)

    @pl.when(g % 2 == 0)
    def _():
        step(0, 1)

    @pl.when(g % 2 == 1)
    def _():
        step(1, 0)


def _na(q, k, v, bias, rows):
    t, d = q.shape
    s = t - CTX_LEN
    nq = NA_QROWS * GRID_W
    nw = NA_WROWS * GRID_W
    assert nq == ROW_TILE and rows >= NA_WROWS and rows % NA_QROWS == 0
    n_groups = rows // NA_QROWS
    cur = lambda g: jnp.minimum(g, n_groups - 1)
    variant = lambda g: jnp.where(g == 0, 0, jnp.where(g == n_groups - 1, 2, 1))
    return pl.pallas_call(
        functools.partial(_na_kernel, rows=rows, n_groups=n_groups),
        grid=(NA_HEADS // 2, n_groups + 1),
        in_specs=[pl.BlockSpec((nq, LANES), lambda p, g: (cur(g) + 1, p)),
                  pl.BlockSpec((t, LANES), lambda p, g: (0, p)),
                  pl.BlockSpec((t, LANES), lambda p, g: (0, p)),
                  pl.BlockSpec((1, 2, nq, nw), lambda p, g: (variant(cur(g)), p, 0, 0))],
        out_specs=pl.BlockSpec((nq, LANES), lambda p, g: (jnp.maximum(g - 1, 0), p)),
        out_shape=jax.ShapeDtypeStruct((s, d), BF16),
        scratch_shapes=[pltpu.VMEM((2, 2, nq, nw), F32), pltpu.VMEM((2, 2, nq, CTX_LEN), F32),
                        pltpu.VMEM((2, 2, nq, LANES), F32)],
        compiler_params=_cparams(("arbitrary", "arbitrary")),
        name="na",
    )(q, k, v, bias)


def _attn_out_kernel(*refs, gla):
    if gla:
        (x_ref, of_ref, ob_ref, r_ref, om_ref, on_ref, w_ref, g1_ref, gate_ref, sc_ref, sh_ref,
         rwh_ref, rwl_ref, rb_ref, xo_ref, h_ref, lg_ref) = refs
        o = of_ref[...] + ob_ref[...]
        r = r_ref[...]
        parts = []
        for hd in range(GLA_HEADS):
            sl = slice(hd * GLA_DV, (hd + 1) * GLA_DV)
            rh = r[:, sl]
            parts.append(_rms(o[:, sl], on_ref[...]) * (rh * (1.0 / (1.0 + jnp.exp(-rh)))))
        mix_a = jnp.concatenate(parts, axis=1).astype(BF16)
        y = _dot(mix_a, w_ref[0:GLA_VW, :]) + _dot(om_ref[...], w_ref[GLA_VW:, :])
    else:
        (x_ref, om_ref, w_ref, g1_ref, gate_ref, sc_ref, sh_ref,
         rwh_ref, rwl_ref, rb_ref, xo_ref, h_ref, lg_ref) = refs
        y = _dot(om_ref[...], w_ref[...])
    x = x_ref[...] + _rms(y, g1_ref[...]) * gate_ref[0]
    xo_ref[...] = x
    h = x * lax.rsqrt(jnp.mean(x * x, axis=-1, keepdims=True) + EPS) * sc_ref[0] + sh_ref[0]
    h_ref[...] = h
    h_hi = h.astype(BF16)
    h_lo = (h - h_hi.astype(F32)).astype(BF16)
    lg_ref[...] = _dot(h_hi, rwh_ref[...]) + _dot(h_lo, rwh_ref[...]) + _dot(h_hi, rwl_ref[...]) + rb_ref[...]


def _attn_out(xs, mix_inputs, w_out, g1, gate, scale, shift, rw_hi, rw_lo, rb, *, gla, first_tile):
    d = xs.shape[1]
    tm = ROW_TILE
    t = mix_inputs[0].shape[0]
    row = lambda i: (i, 0)
    xrow = lambda i: (i + first_tile, 0)
    const = lambda i: (0, 0)
    strm = lambda i: (_stream(i + first_tile), 0, 0)
    if gla:
        of, ob, r, om, onorm = mix_inputs
        mix_specs = [pl.BlockSpec((tm, GLA_VW), row), pl.BlockSpec((tm, GLA_VW), row), pl.BlockSpec((tm, GLA_VW), row),
                     pl.BlockSpec((tm, MLA_VW), row), pl.BlockSpec(onorm.shape, const)]
    else:
        mix_specs = [pl.BlockSpec((tm, d), row)]
    return pl.pallas_call(
        functools.partial(_attn_out_kernel, gla=gla),
        grid=(t // tm,),
        in_specs=[pl.BlockSpec((tm, d), xrow)] + mix_specs + [
            pl.BlockSpec(w_out.shape, const), pl.BlockSpec(g1.shape, const),
            pl.BlockSpec((1, 1, d), strm), pl.BlockSpec((1, 1, d), strm), pl.BlockSpec((1, 1, d), strm),
            pl.BlockSpec(rw_hi.shape, const), pl.BlockSpec(rw_lo.shape, const), pl.BlockSpec(rb.shape, const)],
        out_specs=[pl.BlockSpec((tm, d), row), pl.BlockSpec((tm, d), row), pl.BlockSpec((tm, LANES), row)],
        out_shape=[jax.ShapeDtypeStruct((t, d), F32), jax.ShapeDtypeStruct((t, d), F32),
                   jax.ShapeDtypeStruct((t, LANES), F32)],
        compiler_params=_cparams(("arbitrary",)),
        name="attn_out",
    )(xs, *mix_inputs, w_out, g1, gate, scale, shift, rw_hi, rw_lo, rb)


def _moe_kernel(be_ref, nv_ref, x_ref, gate_ref, w1_ref, p_ref, b1_ref, w2_ref, b2_ref, o_ref, w1b_ref, w2b_ref):
    i = pl.program_id(0)
    f = w2_ref.shape[2]

    @pl.when((i == 0) | (be_ref[i] != be_ref[jnp.maximum(i - 1, 0)]))
    def _():
        for c in range(2 * f // (2 * LANES)):
            cols = slice(c * 2 * LANES, (c + 1) * 2 * LANES)
            w1b_ref[:, cols] = _dot(w1_ref[0, 0, :, cols].astype(BF16), p_ref[...]).astype(BF16)
        w2b_ref[...] = w2_ref[0, 0].astype(BF16)

    @pl.when(i < nv_ref[0])
    def _():
        x = x_ref[...].astype(BF16)
        acts = []
        for c in range(f // LANES):
            u = _dot(x, w1b_ref[:, 2 * c * LANES:(2 * c + 2) * LANES]) \
                + b1_ref[0, 0, :, 2 * c * LANES:(2 * c + 2) * LANES]
            glu = jnp.minimum(u[:, :LANES], SWIGLU_LIMIT)
            lin = jnp.clip(u[:, LANES:], -SWIGLU_LIMIT, SWIGLU_LIMIT)
            acts.append((glu * (1.0 / (1.0 + jnp.exp(-SWIGLU_ALPHA * glu))) * (lin + 1.0)).astype(BF16))
        y = _dot(jnp.concatenate(acts, axis=1), w2b_ref[...]) + b2_ref[0, 0]
        o_ref[...] = (y * gate_ref[...]).astype(o_ref.dtype)

    @pl.when(i >= nv_ref[0])
    def _():
        o_ref[...] = jnp.zeros_like(o_ref)


def _regroup_matrix():
    grp = 2 * LANES
    src = np.concatenate([np.arange(0, grp, 2), np.arange(1, grp, 2)])
    perm = np.zeros((grp, grp), np.float32)
    perm[src, np.arange(grp)] = 1.0
    return jnp.asarray(perm, BF16)


def _moe_experts(block_e, n_valid, xb, slot_gate, w1, b1, w2, b2, layer):
    n_slots, d = xb.shape
    bm = MOE_ROWS
    f = w2.shape[2]
    perm = _regroup_matrix()
    grid_spec = pltpu.PrefetchScalarGridSpec(
        num_scalar_prefetch=2,
        grid=(n_slots // bm,),
        in_specs=[pl.BlockSpec((bm, d), lambda i, be, nv: (i, 0)),
                  pl.BlockSpec((bm, 1), lambda i, be, nv: (i, 0)),
                  pl.BlockSpec((1, 1, d, 2 * f), lambda i, be, nv: (layer, be[i], 0, 0)),
                  pl.BlockSpec(perm.shape, lambda i, be, nv: (0, 0)),
                  pl.BlockSpec((1, 1, 1, 2 * f), lambda i, be, nv: (layer, be[i], 0, 0)),
                  pl.BlockSpec((1, 1, f, d), lambda i, be, nv: (layer, be[i], 0, 0)),
                  pl.BlockSpec((1, 1, 1, d), lambda i, be, nv: (layer, be[i], 0, 0))],
        out_specs=pl.BlockSpec((bm, d), lambda i, be, nv: (i, 0)),
        scratch_shapes=[pltpu.VMEM((d, 2 * f), BF16), pltpu.VMEM((f, d), BF16)],
    )
    return pl.pallas_call(
        _moe_kernel,
        grid_spec=grid_spec,
        out_shape=jax.ShapeDtypeStruct((n_slots, d), BF16),
        compiler_params=_cparams(("arbitrary",)),
        name="moe_experts",
    )(block_e, n_valid, xb, slot_gate, w1, perm, b1, w2, b2)


def _moe(hp, logits, w1, b1, w2, b2, layer):
    t = hp.shape[0]
    bm = MOE_ROWS
    top_val, top_idx = lax.top_k(logits, TOP_K)
    gates = jax.nn.softmax(top_val, axis=-1)
    n_assign = t * TOP_K
    n_blocks = -(-n_assign // bm) + N_EXPERTS
    experts = jnp.arange(N_EXPERTS, dtype=jnp.int32)
    hit = jnp.any(top_idx[:, :, None] == experts, axis=1).astype(jnp.int32)
    counts = jnp.sum(hit, axis=0)
    before = jnp.cumsum(hit, axis=0) - hit
    padded = ((counts + bm - 1) // bm) * bm
    pad_end = jnp.cumsum(padded)
    pad_start = pad_end - padded
    start = jnp.cumsum(counts) - counts
    slot_of = jnp.take_along_axis(before + pad_start[None, :], top_idx, axis=1)
    block_first = jnp.arange(n_blocks, dtype=jnp.int32) * bm
    block_e = jnp.minimum(jnp.sum(block_first[:, None] >= pad_end[None, :], axis=1), N_EXPERTS - 1).astype(jnp.int32)
    order = jnp.argsort(top_idx.reshape(-1))
    local = (block_first - pad_start[block_e])[:, None] + jnp.arange(bm, dtype=jnp.int32)[None, :]
    valid = local < counts[block_e][:, None]
    assign = order[jnp.clip(start[block_e][:, None] + local, 0, n_assign - 1)]
    slot_tok = jnp.where(valid, assign // TOP_K, 0).astype(jnp.int32)
    slot_gate = jnp.where(valid, gates.reshape(-1)[assign], 0.0)
    n_valid = (pad_end[-1] // bm).astype(jnp.int32).reshape(1)
    yb = _moe_experts(block_e, n_valid, hp[slot_tok.reshape(-1)], slot_gate.reshape(n_blocks * bm, 1),
                      w1, b1, w2, b2, layer)
    return yb[slot_of.T.reshape(-1)].reshape(TOP_K, t, yb.shape[1])


def _moe_biases(moe_b1, moe_b2):
    depth, e, f2 = moe_b1.shape
    b1p = jnp.transpose(moe_b1.reshape(depth, e, f2 // (2 * LANES), LANES, 2), (0, 1, 2, 4, 3))
    return b1p.reshape(depth, e, 1, f2), moe_b2.reshape(depth, e, 1, moe_b2.shape[2])


def _resid_kernel(*refs, project):
    if project:
        x_ref, y_ref, g3_ref, gate_ref, sc_ref, sh_ref, w_ref, xo_ref, q_ref, k_ref, v_ref = refs
    else:
        x_ref, y_ref, g3_ref, gate_ref, xo_ref = refs
    y = y_ref[0].astype(F32)
    for j in range(1, TOP_K):
        y = y + y_ref[j].astype(F32)
    x = x_ref[...] + _rms(y, g3_ref[...]) * gate_ref[0]
    xo_ref[...] = x
    if project:
        d = x.shape[1]
        h = x * lax.rsqrt(jnp.mean(x * x, axis=-1, keepdims=True) + EPS) * sc_ref[0] + sh_ref[0]
        u = _dot(h.astype(BF16), w_ref[...])
        q_ref[...] = u[:, 0:d].astype(BF16)
        k_ref[...] = u[:, d:2 * d].astype(BF16)
        v_ref[...] = u[:, 2 * d:3 * d].astype(BF16)


def _resid(xs, y, g3, gate, scale=None, shift=None, w=None, *, first_tile=0):
    t, d = xs.shape
    tm = ROW_TILE
    project = w is not None
    row = lambda i: (i, 0)
    const = lambda i: (0, 0)
    strm = lambda i: (_stream(i + first_tile), 0, 0)
    vec = pl.BlockSpec((1, 1, d), strm)
    in_specs = [pl.BlockSpec((tm, d), row), pl.BlockSpec((TOP_K, tm, d), lambda i: (0, i, 0)),
                pl.BlockSpec(g3.shape, const), vec]
    out_specs = [pl.BlockSpec((tm, d), row)]
    out_shape = [jax.ShapeDtypeStruct((t, d), F32)]
    args = [xs, y, g3, gate]
    if project:
        in_specs += [vec, vec, pl.BlockSpec(w.shape, const)]
        out_specs += [pl.BlockSpec((tm, d), row)] * 3
        out_shape += [jax.ShapeDtypeStruct((t, d), BF16)] * 3
        args += [scale, shift, w]
    return pl.pallas_call(
        functools.partial(_resid_kernel, project=project),
        grid=(t // tm,),
        in_specs=in_specs, out_specs=out_specs, out_shape=out_shape,
        compiler_params=_cparams(("arbitrary",)),
        name="resid",
    )(*args)


def _rope_tables(s):
    tok = jnp.arange(s)
    row = (tok // GRID_W).astype(F32)
    col = (tok % GRID_W).astype(F32)
    n_freq = MLA_ROPE // 4
    inv = ROPE_THETA ** (-jnp.arange(n_freq, dtype=F32) / n_freq)
    ang = jnp.concatenate([row[:, None] * inv, col[:, None] * inv], axis=-1)
    cos, sin = jnp.cos(ang), jnp.sin(ang)
    cos_t = jnp.concatenate([cos, cos, cos, cos], axis=-1)
    sin_t = jnp.concatenate([-sin, sin, -sin, sin], axis=-1)
    cos_t = jnp.concatenate([jnp.ones((CTX_LEN, LANES), F32), cos_t], axis=0)
    sin_t = jnp.concatenate([jnp.zeros((CTX_LEN, LANES), F32), sin_t], axis=0)
    return cos_t, sin_t


def _pad_cols(w, n):
    return jnp.pad(w, ((0, 0), (0, n - w.shape[1])))


def _swap_halves(w):
    half = w.shape[-1] // 2
    return jnp.concatenate([w[..., half:], w[..., :half]], axis=-1)


def _proj0_weights(in_w, wa_f, ba_f, wa_b, ba_b, wuq, wukv):
    o = np.cumsum((0, GLA_QK, GLA_QK, GLA_VW, GLA_VW, GLA_GATE_RANK, GLA_GATE_RANK, MLA_Q_LORA, MLA_KV_LORA,
                   MLA_ROPE))
    q, k, v, r, a_f, a_b, cq, ckv, kr = [in_w[:, o[i]:o[i + 1]] for i in range(9)]
    w_in = jnp.concatenate([q * (GLA_DK ** -0.5), k, v, r, cq, ckv, _pad_cols(kr, LANES),
                            _pad_cols(_swap_halves(kr), LANES),
                            _pad_cols(jnp.concatenate([a_f, a_b], axis=1), LANES)], axis=1).astype(BF16)
    wa = jnp.zeros((LANES, 2 * GLA_QK), F32)
    wa = wa.at[0:GLA_GATE_RANK, 0:GLA_QK].set(wa_f).at[GLA_GATE_RANK:2 * GLA_GATE_RANK, GLA_QK:].set(wa_b)
    ba = jnp.concatenate([ba_f, ba_b]).reshape(1, 2 * GLA_QK)
    wq = wuq.reshape(MLA_Q_LORA, MLA_HEADS, MLA_NOPE + MLA_ROPE)
    nope = wq[:, :, :MLA_NOPE].reshape(MLA_Q_LORA, MLA_HEADS * MLA_NOPE)
    rope = wq[:, :, MLA_NOPE:]
    pad3 = lambda z: jnp.pad(z, ((0, 0), (0, 0), (0, LANES - MLA_ROPE))).reshape(MLA_Q_LORA, MLA_HEADS * LANES)
    wuq_x = jnp.concatenate([nope, pad3(rope), pad3(_swap_halves(rope))], axis=1).astype(BF16)
    wkv = wukv.reshape(MLA_KV_LORA, MLA_HEADS, MLA_NOPE + MLA_V)
    wukv_x = jnp.concatenate([wkv[:, :, :MLA_NOPE].reshape(MLA_KV_LORA, -1),
                              wkv[:, :, MLA_NOPE:].reshape(MLA_KV_LORA, -1)], axis=1).astype(BF16)
    return w_in, wa.astype(BF16), ba, wuq_x, wukv_x


def _router_weights(rw, rb):
    rw = _pad_cols(rw, LANES)
    hi = rw.astype(BF16)
    lo = (rw - hi.astype(F32)).astype(BF16)
    return hi, lo, _pad_cols(rb.reshape(1, -1), LANES)


def kernel(x, c, ctx, c_ctx, ada_w, ada_b, norm_g, router_w, router_b, moe_w1, moe_b1, moe_w2, moe_b2, ab_in_w,
           gla_wa_f, gla_ba_f, gla_wa_b, gla_ba_b, gla_onorm, mla_qnorm, mla_wuq, mla_kvnorm, mla_wukv, ab_out_w,
           na_qkv_w, na_rpb, na_out_w):
    b, s, d = x.shape
    assert b == 1 and d == D_MODEL and ctx.shape == (1, CTX_LEN, d) and s % (NA_QROWS * GRID_W) == 0
    rows = s // GRID_W
    xs = jnp.concatenate([ctx[0], x[0]], axis=0)

    cc = jnp.zeros((8, d), F32).at[0].set(c[0]).at[1].set(c_ctx)
    mod = _ada(cc, ada_w, ada_b)

    def vectors(l):
        parts = jnp.split(mod[l, 0:2][::-1], 6, axis=-1)
        sh1, sc1, g1, sh2, sc2, g2 = [p.reshape(2, 1, d) for p in parts]
        return ((1 + sc1) * norm_g[l, 0], sh1, g1, (1 + sc2) * norm_g[l, 2], sh2, g2)

    sc1, sh1, g1, sc2, sh2, g2 = vectors(0)
    cos_t, sin_t = _rope_tables(s)
    w_in, wa, ba, wuq_x, wukv_x = _proj0_weights(ab_in_w[0], gla_wa_f[0], gla_ba_f[0], gla_wa_b[0], gla_ba_b[0],
                                                 mla_wuq[0], mla_wukv[0])
    qg, kg, vg, r, gf, gb, qm, km, vm = _proj0(xs, sc1, sh1, cos_t, sin_t, w_in, wa, ba,
                                               mla_qnorm[0].reshape(1, -1), wuq_x,
                                               mla_kvnorm[0].reshape(1, -1), wukv_x)
    o_f, o_b = _gla(qg, kg, vg, gf, gb)
    om_l = _flash(qm[:, CTX_LEN:], km, vm, FLASH_TQ, FLASH_TK)
    om_c = _flash(qm[:, :CTX_LEN], km[:, :CTX_LEN], vm[:, :CTX_LEN], CTX_LEN, CTX_LEN)
    om = jnp.concatenate([om_c, om_l], axis=0)
    rw_hi, rw_lo, rb = _router_weights(router_w[0], router_b[0])
    xs, h2, logits = _attn_out(xs, (o_f, o_b, r, om, gla_onorm[0].reshape(1, -1)), ab_out_w[0].astype(BF16),
                               norm_g[0, 1].reshape(1, -1), g1, sc2, sh2, rw_hi, rw_lo, rb, gla=True, first_tile=0)
    b1p, b2r = _moe_biases(moe_b1, moe_b2)
    y = _moe(h2, logits[:, :N_EXPERTS], moe_w1, b1p, moe_w2, b2r, 0)

    sc1n, sh1n, g1n, sc2n, sh2n, g2n = vectors(1)
    qkv_w = jnp.concatenate([na_qkv_w[0][:, :d] * (NA_DH ** -0.5), na_qkv_w[0][:, d:]], axis=1).astype(BF16)
    xs, q, k, v = _resid(xs, y, norm_g[0, 3].reshape(1, -1), g2, sc1n, sh1n, qkv_w)
    o_na = _na(q, k, v, _na_bias_tiles(na_rpb[0], rows), rows)
    rw_hi, rw_lo, rb = _router_weights(router_w[1], router_b[1])
    x_l, h2, logits = _attn_out(xs, (o_na,), na_out_w[0].astype(BF16), norm_g[1, 1].reshape(1, -1), g1n, sc2n, sh2n,
                                rw_hi, rw_lo, rb, gla=False, first_tile=1)
    y = _moe(h2, logits[:, :N_EXPERTS], moe_w1, b1p, moe_w2, b2r, 1)
    (x_l,) = _resid(x_l, y, norm_g[1, 3].reshape(1, -1), g2n, first_tile=1)
    return x_l.reshape(1, s, d)
```

```python
import functools

import numpy as np
import jax
import jax.numpy as jnp
from jax import lax
from jax.experimental import pallas as pl
from jax.experimental.pallas import tpu as pltpu

F32 = jnp.float32
BF16 = jnp.bfloat16

D_MODEL = 1024
DEPTH = 2
CTX_LEN = 256
GRID_W = 64
EPS = 1e-6
ROPE_THETA = 10000.0

GLA_HEADS = 4
GLA_DK = 64
GLA_DV = 128
GLA_GATE_RANK = 16
GLA_TAU = 16.0
GLA_QK = GLA_HEADS * GLA_DK
GLA_VW = GLA_HEADS * GLA_DV

MLA_HEADS = 4
MLA_Q_LORA = 256
MLA_KV_LORA = 128
MLA_NOPE = 128
MLA_ROPE = 64
MLA_V = 128
MLA_VW = MLA_HEADS * MLA_V
MLA_QK_PAD = 256

NA_HEADS = 16
NA_DH = D_MODEL // NA_HEADS
NA_KH = 8
NA_KW = 16
NA_QROWS = 4
NA_WROWS = NA_QROWS + NA_KH

N_EXPERTS = 32
TOP_K = 4
SWIGLU_LIMIT = 7.0
SWIGLU_ALPHA = 1.702

LANES = 128
ROW_TILE = 256
MOE_ROWS = 512
VMEM_LIMIT = 56 * 1024 * 1024
NEG_BIG = -1e30

GLA_CHUNK = 64
GLA_SUB = 16
FLASH_TQ = 512
FLASH_TK = 3328
FLASH_SUB = 256


def _cparams(sem):
    return pltpu.CompilerParams(dimension_semantics=sem, vmem_limit_bytes=VMEM_LIMIT)


def _dot(a, b):
    return jnp.dot(a, b, preferred_element_type=F32)


def _dot_nt(a, b):
    return lax.dot_general(a, b, (((1,), (1,)), ((), ())), preferred_element_type=F32)


def _rms(x, g):
    return x * lax.rsqrt(jnp.mean(x * x, axis=-1, keepdims=True) + EPS) * g


def _stream(i):
    return jnp.where(i == 0, 0, 1)


def _ada_kernel(c_ref, w_ref, b_ref, o_ref):
    c = c_ref[...]
    s = (c * (1.0 / (1.0 + jnp.exp(-c)))).astype(BF16)
    o_ref[0] = _dot(s, w_ref[0].astype(BF16)) + b_ref[0]


def _ada(cc, ada_w, ada_b):
    depth, d, n = ada_w.shape
    tn = 1536
    return pl.pallas_call(
        _ada_kernel,
        grid=(depth, n // tn),
        in_specs=[pl.BlockSpec((8, d), lambda l, j: (0, 0)),
                  pl.BlockSpec((1, d, tn), lambda l, j: (l, 0, j)),
                  pl.BlockSpec((1, 1, tn), lambda l, j: (l, 0, j))],
        out_specs=pl.BlockSpec((1, 8, tn), lambda l, j: (l, 0, j)),
        out_shape=jax.ShapeDtypeStruct((depth, 8, n), F32),
        compiler_params=_cparams(("arbitrary", "arbitrary")),
        name="ada",
    )(cc, ada_w, ada_b.reshape(depth, 1, n))


def _proj0_kernel(x_ref, sc_ref, sh_ref, cos_ref, sin_ref, win_ref, wa_ref, ba_ref, qn_ref, wuq_ref,
                  kvn_ref, wukv_ref,
                  qg_ref, kg_ref, vg_ref, r_ref, gf_ref, gb_ref, qm_ref, km_ref, vm_ref, *, mla_scale):
    x = x_ref[...]
    h = x * lax.rsqrt(jnp.mean(x * x, axis=-1, keepdims=True) + EPS) * sc_ref[0] + sh_ref[0]
    u = _dot(h.astype(BF16), win_ref[...])
    qg_ref[...] = u[:, 0:256]
    kg_ref[...] = u[:, 256:512]
    vg_ref[...] = u[:, 512:1024].astype(BF16)
    r_ref[...] = u[:, 1024:1536]
    cq = u[:, 1536:1792]
    ckv = u[:, 1792:1920]
    kr = u[:, 1920:2048]
    krs = u[:, 2048:2176]
    a = u[:, 2176:2304]

    ga = _dot(a.astype(BF16), wa_ref[...]) + ba_ref[...]
    la = (jnp.minimum(ga, 0.0) - jnp.log1p(jnp.exp(-jnp.abs(ga)))) * (1.0 / GLA_TAU)
    gf_ref[...] = la[:, :GLA_QK]
    gb_ref[...] = la[:, GLA_QK:]

    cos = cos_ref[...]
    sin = sin_ref[...]
    qm = _dot(_rms(cq, qn_ref[...]).astype(BF16), wuq_ref[...])
    kvm = _dot(_rms(ckv, kvn_ref[...]).astype(BF16), wukv_ref[...])
    kroped = (kr * cos + krs * sin).astype(BF16)
    for hd in range(MLA_HEADS):
        nope = qm[:, hd * 128:(hd + 1) * 128]
        rp = qm[:, 512 + hd * 128:512 + (hd + 1) * 128]
        rps = qm[:, 1024 + hd * 128:1024 + (hd + 1) * 128]
        roped = rp * cos + rps * sin
        qm_ref[hd] = (jnp.concatenate([nope, roped], axis=1) * mla_scale).astype(BF16)
        km_ref[hd] = jnp.concatenate([kvm[:, hd * 128:(hd + 1) * 128].astype(BF16), kroped], axis=1)
        v_h = kvm[:, 512 + hd * 128:512 + (hd + 1) * 128].astype(BF16)
        vm_ref[hd] = jnp.concatenate([v_h, jnp.ones_like(v_h)], axis=1)


def _proj0(xs, scale, shift, cos_t, sin_t, w_in, wa, ba, qn, wuq, kvn, wukv):
    t, d = xs.shape
    tm = ROW_TILE
    row = lambda i: (i, 0)
    const = lambda i: (0, 0)
    strm = lambda i: (_stream(i), 0, 0)
    hrow = lambda i: (0, i, 0)
    kern = functools.partial(_proj0_kernel, mla_scale=float((MLA_NOPE + MLA_ROPE) ** -0.5 * np.log2(np.e)))
    return pl.pallas_call(
        kern,
        grid=(t // tm,),
        in_specs=[pl.BlockSpec((tm, d), row),
                  pl.BlockSpec((1, 1, d), strm), pl.BlockSpec((1, 1, d), strm),
                  pl.BlockSpec((tm, LANES), row), pl.BlockSpec((tm, LANES), row),
                  pl.BlockSpec(w_in.shape, const), pl.BlockSpec(wa.shape, const), pl.BlockSpec(ba.shape, const),
                  pl.BlockSpec(qn.shape, const), pl.BlockSpec(wuq.shape, const),
                  pl.BlockSpec(kvn.shape, const), pl.BlockSpec(wukv.shape, const)],
        out_specs=[pl.BlockSpec((tm, GLA_QK), row), pl.BlockSpec((tm, GLA_QK), row),
                   pl.BlockSpec((tm, GLA_VW), row), pl.BlockSpec((tm, GLA_VW), row),
                   pl.BlockSpec((tm, GLA_QK), row), pl.BlockSpec((tm, GLA_QK), row),
                   pl.BlockSpec((MLA_HEADS, tm, MLA_QK_PAD), hrow),
                   pl.BlockSpec((MLA_HEADS, tm, MLA_QK_PAD), hrow),
                   pl.BlockSpec((MLA_HEADS, tm, 2 * MLA_V), hrow)],
        out_shape=[jax.ShapeDtypeStruct((t, GLA_QK), F32), jax.ShapeDtypeStruct((t, GLA_QK), F32),
                   jax.ShapeDtypeStruct((t, GLA_VW), BF16), jax.ShapeDtypeStruct((t, GLA_VW), F32),
                   jax.ShapeDtypeStruct((t, GLA_QK), F32), jax.ShapeDtypeStruct((t, GLA_QK), F32),
                   jax.ShapeDtypeStruct((MLA_HEADS, t, MLA_QK_PAD), BF16),
                   jax.ShapeDtypeStruct((MLA_HEADS, t, MLA_QK_PAD), BF16),
                   jax.ShapeDtypeStruct((MLA_HEADS, t, 2 * MLA_V), BF16)],
        compiler_params=_cparams(("arbitrary",)),
        name="proj0",
    )(xs, scale, shift, cos_t, sin_t, w_in, wa, ba, qn, wuq, kvn, wukv)


def _gla_tables():
    c, s = GLA_CHUNK, GLA_SUB
    t = np.arange(c)[:, None]
    u = np.arange(c)[None, :]
    bt, bu = t // s, u // s
    half = (c // s) // 2
    mats = [(bu == bt) & (u <= t), (bu == bt) & (u > t), bu < bt, bu > bt,
            (bu >= half) & (bu < bt), (bu > bt) & (bu < half)]
    m = np.concatenate([x.astype(np.float32) for x in mats], axis=0)
    sel = np.stack([(bt == bu) & (u <= t), bt == bu + 1, (bt >= half) & (bu < half)]).astype(np.float32)
    m_b = np.concatenate([x.astype(np.float32)[::-1, ::-1] for x in mats], axis=0)
    sel_b = sel[:, ::-1, ::-1]
    return np.stack([m, m_b]), np.stack([sel, sel_b])


def _gla_chunk(q, k, g, v_ref, vt_ref, j, m, sel, st_ref, o_ref):
    rows = pl.ds(j * GLA_CHUNK, GLA_CHUNK)
    c = GLA_CHUNK
    g_hi = g.astype(BF16)
    g_r1 = g - g_hi.astype(F32)
    g_mid = g_r1.astype(BF16)
    g_lo = (g_r1 - g_mid.astype(F32)).astype(BF16)
    cs = _dot(m, g_hi) + _dot(m, g_mid) + _dot(m, g_lo)
    own, rest, before, after = cs[0:c], cs[c:2 * c], cs[2 * c:3 * c], cs[3 * c:4 * c]
    far_q, far_k = cs[4 * c:5 * c], cs[5 * c:6 * c]
    qd = q * jnp.exp(own)
    kd = k * jnp.exp(-own)
    kh = k * jnp.exp(rest)
    q_in = qd * jnp.exp(before)
    k_out = kh * jnp.exp(after)
    q_far = qd * jnp.exp(far_q)
    k_far = kh * jnp.exp(far_k)
    decay = jnp.exp(own + rest + before + after)
    lane = lax.broadcasted_iota(jnp.int32, (c, LANES), 1)
    s_diag, s_adj, s_far = sel[0] > 0.5, sel[1] > 0.5, sel[2] > 0.5
    for p in range(GLA_HEADS // 2):
        lo, hi = p * LANES, (p + 1) * LANES
        kd_p = kd[:, lo:hi].astype(BF16)
        kh_p = kh[:, lo:hi].astype(BF16)
        kf_p = k_far[:, lo:hi].astype(BF16)
        st = st_ref[p]
        st_b = st.astype(BF16)
        upd = st * decay[0:1, lo:hi]
        for half in range(2):
            hd = 2 * p + half
            msk = (lane < GLA_DK) if half == 0 else (lane >= GLA_DK)
            pick = lambda z: jnp.where(msk, z[:, lo:hi], 0.0).astype(BF16)
            qd_h = pick(qd)
            a = jnp.where(s_diag, _dot_nt(qd_h, kd_p),
                          jnp.where(s_adj, _dot_nt(qd_h, kh_p),
                                    jnp.where(s_far, _dot_nt(pick(q_far), kf_p), 0.0)))
            v_h = v_ref[rows, hd * GLA_DV:(hd + 1) * GLA_DV]
            o_ref[rows, hd * GLA_DV:(hd + 1) * GLA_DV] = _dot(a.astype(BF16), v_h) + _dot_nt(pick(q_in), st_b)
            upd = upd + _dot(vt_ref[j, hd * GLA_DV:(hd + 1) * GLA_DV, :], pick(k_out))
        st_ref[p] = upd


def _gla_kernel(qf_ref, kf_ref, gf_ref, vf_ref, vtf_ref, qb_ref, kb_ref, gb_ref, vb_ref, vtb_ref, m_ref, sel_ref,
                of_ref, ob_ref, stf_ref, stb_ref):
    @pl.when(pl.program_id(0) == 0)
    def _():
        stf_ref[...] = jnp.zeros_like(stf_ref)
        stb_ref[...] = jnp.zeros_like(stb_ref)

    n = ROW_TILE // GLA_CHUNK
    for j in range(n):
        rows = pl.ds(j * GLA_CHUNK, GLA_CHUNK)
        _gla_chunk(qf_ref[rows, :], kf_ref[rows, :], gf_ref[rows, :], vf_ref, vtf_ref, j,
                   m_ref[0], sel_ref[0], stf_ref, of_ref)
        jb = n - 1 - j
        rows = pl.ds(jb * GLA_CHUNK, GLA_CHUNK)
        _gla_chunk(qb_ref[rows, :], kb_ref[rows, :], gb_ref[rows, :], vb_ref, vtb_ref, jb,
                   m_ref[1], sel_ref[1], stb_ref, ob_ref)


def _gla(q, k, v, gf, gb):
    t = q.shape[0]
    tm = ROW_TILE
    nb = t // tm
    n = tm // GLA_CHUNK
    m_np, sel_np = _gla_tables()
    m = jnp.asarray(m_np, BF16)
    sel = jnp.asarray(sel_np, F32)
    vt = jnp.transpose(v.reshape(t // GLA_CHUNK, GLA_CHUNK, GLA_VW), (0, 2, 1))
    fwd = lambda i: (i, 0)
    bwd = lambda i: (jnp.where(i == 0, 0, nb - i), 0)
    fwd_t = lambda i: (i, 0, 0)
    bwd_t = lambda i: (jnp.where(i == 0, 0, nb - i), 0, 0)
    c3 = lambda i: (0, 0, 0)
    c4 = lambda i: (0, 0, 0, 0)
    spec = lambda w, im: pl.BlockSpec((tm, w), im)
    return pl.pallas_call(
        _gla_kernel,
        grid=(nb,),
        in_specs=[spec(GLA_QK, fwd), spec(GLA_QK, fwd), spec(GLA_QK, fwd), spec(GLA_VW, fwd),
                  pl.BlockSpec((n, GLA_VW, GLA_CHUNK), fwd_t),
                  spec(GLA_QK, bwd), spec(GLA_QK, bwd), spec(GLA_QK, bwd), spec(GLA_VW, bwd),
                  pl.BlockSpec((n, GLA_VW, GLA_CHUNK), bwd_t),
                  pl.BlockSpec(m.shape, c3), pl.BlockSpec(sel.shape, c4)],
        out_specs=[spec(GLA_VW, fwd), spec(GLA_VW, bwd)],
        out_shape=[jax.ShapeDtypeStruct((t, GLA_VW), F32), jax.ShapeDtypeStruct((t, GLA_VW), F32)],
        scratch_shapes=[pltpu.VMEM((GLA_HEADS // 2, GLA_DV, LANES), F32),
                        pltpu.VMEM((GLA_HEADS // 2, GLA_DV, LANES), F32)],
        compiler_params=_cparams(("arbitrary",)),
        name="gla",
    )(q, k, gf, v, vt, q, k, gb, v, vt, m, sel)


def _flash_kernel(q_ref, k_ref, v_ref, o_ref, s_ref, mc_ref, m_ref, acc_ref, *, tk, n_chunks):
    m_ref[...] = jnp.full_like(m_ref, -jnp.inf)
    acc_ref[...] = jnp.zeros_like(acc_ref)
    q = q_ref[0]
    tq = q.shape[0]
    sub = min(tk, FLASH_SUB)
    dv = o_ref.shape[1]

    def scores(j, slot):
        off = pl.multiple_of(j * tk, tk)
        s = _dot_nt(q, k_ref[0, pl.ds(off, tk), :])
        s_ref[slot] = s
        mc_ref[slot] = jnp.broadcast_to(jnp.max(s, axis=-1, keepdims=True), (tq, LANES))

    def softmax_pv(j, slot):
        off = pl.multiple_of(j * tk, tk)
        m_old = m_ref[...]
        m_new = jnp.maximum(m_old, mc_ref[slot])
        alpha = jnp.exp2(m_old - m_new)
        m_ref[...] = m_new
        acc = jnp.concatenate([alpha, alpha], axis=1) * acc_ref[...]
        for c in range(tk // sub):
            ps = [jnp.exp2(s_ref[slot, :, c * sub + a * LANES:c * sub + (a + 1) * LANES] - m_new).astype(BF16)
                  for a in range(sub // LANES)]
            acc = acc + _dot(jnp.concatenate(ps, axis=1), v_ref[0, pl.ds(off + c * sub, sub), :])
        acc_ref[...] = acc

    scores(0, 0)

    def pair(i, carry):
        scores(2 * i + 1, 1)
        softmax_pv(2 * i, 0)
        scores(2 * i + 2, 0)
        softmax_pv(2 * i + 1, 1)
        return carry

    n_pairs = (n_chunks - 1) // 2
    lax.fori_loop(0, n_pairs, pair, 0)
    if n_chunks % 2 == 0:
        scores(n_chunks - 1, 1)
        softmax_pv(n_chunks - 2, 0)
        softmax_pv(n_chunks - 1, 1)
    else:
        softmax_pv(n_chunks - 1, 0)
    o_ref[...] = (acc_ref[:, :dv] * (1.0 / acc_ref[:, dv:])).astype(o_ref.dtype)


def _flash(q, k, v, tq, tk):
    h, s, dq = q.shape
    t, dv = k.shape[1], v.shape[2] // 2
    assert s % tq == 0 and t % tk == 0
    kern = functools.partial(_flash_kernel, tk=tk, n_chunks=t // tk)
    return pl.pallas_call(
        kern,
        grid=(h, s // tq),
        in_specs=[pl.BlockSpec((1, tq, dq), lambda hh, i: (hh, i, 0)),
                  pl.BlockSpec((1, t, dq), lambda hh, i: (hh, 0, 0), pipeline_mode=pl.Buffered(1)),
                  pl.BlockSpec((1, t, 2 * dv), lambda hh, i: (hh, 0, 0), pipeline_mode=pl.Buffered(1))],
        out_specs=pl.BlockSpec((tq, dv), lambda hh, i: (i, hh)),
        out_shape=jax.ShapeDtypeStruct((s, h * dv), BF16),
        scratch_shapes=[pltpu.VMEM((2, tq, tk), F32), pltpu.VMEM((2, tq, LANES), F32),
                        pltpu.VMEM((tq, LANES), F32), pltpu.VMEM((tq, 2 * dv), F32)],
        compiler_params=_cparams(("arbitrary", "arbitrary")),
        name="flash",
    )(q, k, v)


def _na_bias_tiles(rpb, rows):
    w = GRID_W
    qc = np.arange(w)[:, None]
    kc = np.arange(w)[None, :]
    c0 = np.clip(qc - NA_KW // 2, 0, w - NA_KW)
    col_ok = (kc >= c0) & (kc < c0 + NA_KW)
    dc = np.clip(kc - qc + NA_KW - 1, 0, 2 * NA_KW - 2)
    onehot = (np.arange(2 * NA_KW - 1)[:, None, None] == dc[None]).astype(np.float32)
    e1 = jnp.einsum('hrd,dqk->hrqk', rpb.astype(F32), jnp.asarray(onehot), precision=lax.Precision.HIGHEST)
    e1 = jnp.where(jnp.asarray(col_ok)[None, None], e1, NEG_BIG)
    e1 = jnp.concatenate([e1, jnp.full((NA_HEADS, 1, w, w), NEG_BIG, F32)], axis=1)
    idx = np.zeros((3, NA_QROWS, NA_WROWS), np.int32)
    for vi, (r_first, ks) in enumerate([(0, 0), (NA_QROWS, 0), (rows - NA_QROWS, rows - NA_WROWS)]):
        for a in range(NA_QROWS):
            qr = r_first + a
            r0 = min(max(qr - NA_KH // 2, 0), rows - NA_KH)
            for b in range(NA_WROWS):
                kr = ks + b
                idx[vi, a, b] = kr - qr + NA_KH - 1 if r0 <= kr < r0 + NA_KH else 2 * NA_KH - 1
    tiles = jnp.take(e1, jnp.asarray(idx.reshape(-1)), axis=1)
    tiles = tiles.reshape(NA_HEADS, 3, NA_QROWS, NA_WROWS, w, w)
    return jnp.transpose(tiles, (1, 0, 2, 4, 3, 5)).reshape(3, NA_HEADS, NA_QROWS * w, NA_WROWS * w)


def _na_kernel(q_ref, kv_k_ref, kv_v_ref, bias_ref, o_ref, sw_ref, sc_ref, m_ref, *, rows, n_groups):
    g = pl.program_id(1)
    nq = NA_QROWS * GRID_W
    nw = NA_WROWS * GRID_W
    lane = lax.broadcasted_iota(jnp.int32, (nq, LANES), 1)

    def window(grp):
        ks = jnp.clip(grp * NA_QROWS - NA_KH // 2, 0, rows - NA_WROWS)
        return pl.ds(pl.multiple_of(CTX_LEN + ks * GRID_W, GRID_W), nw)

    @pl.when(g == 0)
    def _():
        sw_ref[1] = jnp.zeros(sw_ref.shape[1:], F32)
        sc_ref[1] = jnp.zeros(sc_ref.shape[1:], F32)
        m_ref[1] = jnp.zeros(m_ref.shape[1:], F32)

    def step(slot, prev):
        q = q_ref[...]
        k_w = kv_k_ref[window(jnp.minimum(g, n_groups - 1)), :]
        k_c = kv_k_ref[0:CTX_LEN, :]
        for half in range(2):
            msk = (lane < NA_DH) if half == 0 else (lane >= NA_DH)
            q_h = jnp.where(msk, q, jnp.zeros_like(q))
            s_w = _dot_nt(q_h, k_w) + bias_ref[0, half]
            s_c = _dot_nt(q_h, k_c)
            sw_ref[slot, half] = s_w
            sc_ref[slot, half] = s_c
            m = jnp.maximum(jnp.max(s_w, axis=-1, keepdims=True), jnp.max(s_c, axis=-1, keepdims=True))
            m_ref[slot, half] = jnp.broadcast_to(m, (nq, LANES))
        v_w = kv_v_ref[window(jnp.maximum(g - 1, 0)), :]
        v_c = kv_v_ref[0:CTX_LEN, :]
        outs = []
        for half in range(2):
            m = m_ref[prev, half]
            p_w = jnp.exp(sw_ref[prev, half] - jnp.concatenate([m] * (nw // LANES), axis=1))
            p_c = jnp.exp(sc_ref[prev, half] - jnp.concatenate([m] * (CTX_LEN // LANES), axis=1))
            l = jnp.sum(p_w, axis=-1, keepdims=True) + jnp.sum(p_c, axis=-1, keepdims=True)
            o = _dot(p_w.astype(BF16), v_w) + _dot(p_c.astype(BF16), v_c)
            outs.append(o * (1.0 / l))
        o_ref[...] = jnp.where(lane < NA_DH, outs[0], outs[1]).astype(o_ref.dtype)

    @pl.when(g % 2 == 0)
    def _():
        step(0, 1)

    @pl.when(g % 2 == 1)
    def _():
        step(1, 0)


def _na(q, k, v, bias, rows):
    t, d = q.shape
    s = t - CTX_LEN
    nq = NA_QROWS * GRID_W
    nw = NA_WROWS * GRID_W
    assert nq == ROW_TILE and rows >= NA_WROWS and rows % NA_QROWS == 0
    n_groups = rows // NA_QROWS
    cur = lambda g: jnp.minimum(g, n_groups - 1)
    variant = lambda g: jnp.where(g == 0, 0, jnp.where(g == n_groups - 1, 2, 1))
    return pl.pallas_call(
        functools.partial(_na_kernel, rows=rows, n_groups=n_groups),
        grid=(NA_HEADS // 2, n_groups + 1),
        in_specs=[pl.BlockSpec((nq, LANES), lambda p, g: (cur(g) + 1, p)),
                  pl.BlockSpec((t, LANES), lambda p, g: (0, p)),
                  pl.BlockSpec((t, LANES), lambda p, g: (0, p)),
                  pl.BlockSpec((1, 2, nq, nw), lambda p, g: (variant(cur(g)), p, 0, 0))],
        out_specs=pl.BlockSpec((nq, LANES), lambda p, g: (jnp.maximum(g - 1, 0), p)),
        out_shape=jax.ShapeDtypeStruct((s, d), BF16),
        scratch_shapes=[pltpu.VMEM((2, 2, nq, nw), F32), pltpu.VMEM((2, 2, nq, CTX_LEN), F32),
                        pltpu.VMEM((2, 2, nq, LANES), F32)],
        compiler_params=_cparams(("arbitrary", "arbitrary")),
        name="na",
    )(q, k, v, bias)


def _attn_out_kernel(*refs, gla):
    if gla:
        (x_ref, of_ref, ob_ref, r_ref, om_ref, on_ref, w_ref, g1_ref, gate_ref, sc_ref, sh_ref,
         rwh_ref, rwl_ref, rb_ref, xo_ref, h_ref, lg_ref) = refs
        o = of_ref[...] + ob_ref[...]
        r = r_ref[...]
        parts = []
        for hd in range(GLA_HEADS):
            sl = slice(hd * GLA_DV, (hd + 1) * GLA_DV)
            rh = r[:, sl]
            parts.append(_rms(o[:, sl], on_ref[...]) * (rh * (1.0 / (1.0 + jnp.exp(-rh)))))
        mix_a = jnp.concatenate(parts, axis=1).astype(BF16)
        y = _dot(mix_a, w_ref[0:GLA_VW, :]) + _dot(om_ref[...], w_ref[GLA_VW:, :])
    else:
        (x_ref, om_ref, w_ref, g1_ref, gate_ref, sc_ref, sh_ref,
         rwh_ref, rwl_ref, rb_ref, xo_ref, h_ref, lg_ref) = refs
        y = _dot(om_ref[...], w_ref[...])
    x = x_ref[...] + _rms(y, g1_ref[...]) * gate_ref[0]
    xo_ref[...] = x
    h = x * lax.rsqrt(jnp.mean(x * x, axis=-1, keepdims=True) + EPS) * sc_ref[0] + sh_ref[0]
    h_ref[...] = h
    h_hi = h.astype(BF16)
    h_lo = (h - h_hi.astype(F32)).astype(BF16)
    lg_ref[...] = _dot(h_hi, rwh_ref[...]) + _dot(h_lo, rwh_ref[...]) + _dot(h_hi, rwl_ref[...]) + rb_ref[...]


def _attn_out(xs, mix_inputs, w_out, g1, gate, scale, shift, rw_hi, rw_lo, rb, *, gla, first_tile):
    d = xs.shape[1]
    tm = ROW_TILE
    t = mix_inputs[0].shape[0]
    row = lambda i: (i, 0)
    xrow = lambda i: (i + first_tile, 0)
    const = lambda i: (0, 0)
    strm = lambda i: (_stream(i + first_tile), 0, 0)
    if gla:
        of, ob, r, om, onorm = mix_inputs
        mix_specs = [pl.BlockSpec((tm, GLA_VW), row), pl.BlockSpec((tm, GLA_VW), row), pl.BlockSpec((tm, GLA_VW), row),
                     pl.BlockSpec((tm, MLA_VW), row), pl.BlockSpec(onorm.shape, const)]
    else:
        mix_specs = [pl.BlockSpec((tm, d), row)]
    return pl.pallas_call(
        functools.partial(_attn_out_kernel, gla=gla),
        grid=(t // tm,),
        in_specs=[pl.BlockSpec((tm, d), xrow)] + mix_specs + [
            pl.BlockSpec(w_out.shape, const), pl.BlockSpec(g1.shape, const),
            pl.BlockSpec((1, 1, d), strm), pl.BlockSpec((1, 1, d), strm), pl.BlockSpec((1, 1, d), strm),
            pl.BlockSpec(rw_hi.shape, const), pl.BlockSpec(rw_lo.shape, const), pl.BlockSpec(rb.shape, const)],
        out_specs=[pl.BlockSpec((tm, d), row), pl.BlockSpec((tm, d), row), pl.BlockSpec((tm, LANES), row)],
        out_shape=[jax.ShapeDtypeStruct((t, d), F32), jax.ShapeDtypeStruct((t, d), F32),
                   jax.ShapeDtypeStruct((t, LANES), F32)],
        compiler_params=_cparams(("arbitrary",)),
        name="attn_out",
    )(xs, *mix_inputs, w_out, g1, gate, scale, shift, rw_hi, rw_lo, rb)


def _moe_kernel(be_ref, nv_ref, x_ref, gate_ref, w1_ref, p_ref, b1_ref, w2_ref, b2_ref, o_ref, w1b_ref, w2b_ref):
    i = pl.program_id(0)
    f = w2_ref.shape[2]

    @pl.when((i == 0) | (be_ref[i] != be_ref[jnp.maximum(i - 1, 0)]))
    def _():
        for c in range(2 * f // (2 * LANES)):
            cols = slice(c * 2 * LANES, (c + 1) * 2 * LANES)
            w1b_ref[:, cols] = _dot(w1_ref[0, 0, :, cols].astype(BF16), p_ref[...]).astype(BF16)
        w2b_ref[...] = w2_ref[0, 0].astype(BF16)

    @pl.when(i < nv_ref[0])
    def _():
        x = x_ref[...].astype(BF16)
        acts = []
        for c in range(f // LANES):
            u = _dot(x, w1b_ref[:, 2 * c * LANES:(2 * c + 2) * LANES]) \
                + b1_ref[0, 0, :, 2 * c * LANES:(2 * c + 2) * LANES]
            glu = jnp.minimum(u[:, :LANES], SWIGLU_LIMIT)
            lin = jnp.clip(u[:, LANES:], -SWIGLU_LIMIT, SWIGLU_LIMIT)
            acts.append((glu * (1.0 / (1.0 + jnp.exp(-SWIGLU_ALPHA * glu))) * (lin + 1.0)).astype(BF16))
        y = _dot(jnp.concatenate(acts, axis=1), w2b_ref[...]) + b2_ref[0, 0]
        o_ref[...] = (y * gate_ref[...]).astype(o_ref.dtype)

    @pl.when(i >= nv_ref[0])
    def _():
        o_ref[...] = jnp.zeros_like(o_ref)


def _regroup_matrix():
    grp = 2 * LANES
    src = np.concatenate([np.arange(0, grp, 2), np.arange(1, grp, 2)])
    perm = np.zeros((grp, grp), np.float32)
    perm[src, np.arange(grp)] = 1.0
    return jnp.asarray(perm, BF16)


def _moe_experts(block_e, n_valid, xb, slot_gate, w1, b1, w2, b2, layer):
    n_slots, d = xb.shape
    bm = MOE_ROWS
    f = w2.shape[2]
    perm = _regroup_matrix()
    grid_spec = pltpu.PrefetchScalarGridSpec(
        num_scalar_prefetch=2,
        grid=(n_slots // bm,),
        in_specs=[pl.BlockSpec((bm, d), lambda i, be, nv: (i, 0)),
                  pl.BlockSpec((bm, 1), lambda i, be, nv: (i, 0)),
                  pl.BlockSpec((1, 1, d, 2 * f), lambda i, be, nv: (layer, be[i], 0, 0)),
                  pl.BlockSpec(perm.shape, lambda i, be, nv: (0, 0)),
                  pl.BlockSpec((1, 1, 1, 2 * f), lambda i, be, nv: (layer, be[i], 0, 0)),
                  pl.BlockSpec((1, 1, f, d), lambda i, be, nv: (layer, be[i], 0, 0)),
                  pl.BlockSpec((1, 1, 1, d), lambda i, be, nv: (layer, be[i], 0, 0))],
        out_specs=pl.BlockSpec((bm, d), lambda i, be, nv: (i, 0)),
        scratch_shapes=[pltpu.VMEM((d, 2 * f), BF16), pltpu.VMEM((f, d), BF16)],
    )
    return pl.pallas_call(
        _moe_kernel,
        grid_spec=grid_spec,
        out_shape=jax.ShapeDtypeStruct((n_slots, d), BF16),
        compiler_params=_cparams(("arbitrary",)),
        name="moe_experts",
    )(block_e, n_valid, xb, slot_gate, w1, perm, b1, w2, b2)


def _moe(hp, logits, w1, b1, w2, b2, layer):
    t = hp.shape[0]
    bm = MOE_ROWS
    top_val, top_idx = lax.top_k(logits, TOP_K)
    gates = jax.nn.softmax(top_val, axis=-1)
    n_assign = t * TOP_K
    n_blocks = -(-n_assign // bm) + N_EXPERTS
    experts = jnp.arange(N_EXPERTS, dtype=jnp.int32)
    hit = jnp.any(top_idx[:, :, None] == experts, axis=1).astype(jnp.int32)
    counts = jnp.sum(hit, axis=0)
    before = jnp.cumsum(hit, axis=0) - hit
    padded = ((counts + bm - 1) // bm) * bm
    pad_end = jnp.cumsum(padded)
    pad_start = pad_end - padded
    start = jnp.cumsum(counts) - counts
    slot_of = jnp.take_along_axis(before + pad_start[None, :], top_idx, axis=1)
    block_first = jnp.arange(n_blocks, dtype=jnp.int32) * bm
    block_e = jnp.minimum(jnp.sum(block_first[:, None] >= pad_end[None, :], axis=1), N_EXPERTS - 1).astype(jnp.int32)
    order = jnp.argsort(top_idx.reshape(-1))
    local = (block_first - pad_start[block_e])[:, None] + jnp.arange(bm, dtype=jnp.int32)[None, :]
    valid = local < counts[block_e][:, None]
    assign = order[jnp.clip(start[block_e][:, None] + local, 0, n_assign - 1)]
    slot_tok = jnp.where(valid, assign // TOP_K, 0).astype(jnp.int32)
    slot_gate = jnp.where(valid, gates.reshape(-1)[assign], 0.0)
    n_valid = (pad_end[-1] // bm).astype(jnp.int32).reshape(1)
    yb = _moe_experts(block_e, n_valid, hp[slot_tok.reshape(-1)], slot_gate.reshape(n_blocks * bm, 1),
                      w1, b1, w2, b2, layer)
    return yb[slot_of.T.reshape(-1)].reshape(TOP_K, t, yb.shape[1])


def _moe_biases(moe_b1, moe_b2):
    depth, e, f2 = moe_b1.shape
    b1p = jnp.transpose(moe_b1.reshape(depth, e, f2 // (2 * LANES), LANES, 2), (0, 1, 2, 4, 3))
    return b1p.reshape(depth, e, 1, f2), moe_b2.reshape(depth, e, 1, moe_b2.shape[2])


def _resid_kernel(*refs, project):
    if project:
        x_ref, y_ref, g3_ref, gate_ref, sc_ref, sh_ref, w_ref, xo_ref, q_ref, k_ref, v_ref = refs
    else:
        x_ref, y_ref, g3_ref, gate_ref, xo_ref = refs
    y = y_ref[0].astype(F32)
    for j in range(1, TOP_K):
        y = y + y_ref[j].astype(F32)
    x = x_ref[...] + _rms(y, g3_ref[...]) * gate_ref[0]
    xo_ref[...] = x
    if project:
        d = x.shape[1]
        h = x * lax.rsqrt(jnp.mean(x * x, axis=-1, keepdims=True) + EPS) * sc_ref[0] + sh_ref[0]
        u = _dot(h.astype(BF16), w_ref[...])
        q_ref[...] = u[:, 0:d].astype(BF16)
        k_ref[...] = u[:, d:2 * d].astype(BF16)
        v_ref[...] = u[:, 2 * d:3 * d].astype(BF16)


def _resid(xs, y, g3, gate, scale=None, shift=None, w=None, *, first_tile=0):
    t, d = xs.shape
    tm = ROW_TILE
    project = w is not None
    row = lambda i: (i, 0)
    const = lambda i: (0, 0)
    strm = lambda i: (_stream(i + first_tile), 0, 0)
    vec = pl.BlockSpec((1, 1, d), strm)
    in_specs = [pl.BlockSpec((tm, d), row), pl.BlockSpec((TOP_K, tm, d), lambda i: (0, i, 0)),
                pl.BlockSpec(g3.shape, const), vec]
    out_specs = [pl.BlockSpec((tm, d), row)]
    out_shape = [jax.ShapeDtypeStruct((t, d), F32)]
    args = [xs, y, g3, gate]
    if project:
        in_specs += [vec, vec, pl.BlockSpec(w.shape, const)]
        out_specs += [pl.BlockSpec((tm, d), row)] * 3
        out_shape += [jax.ShapeDtypeStruct((t, d), BF16)] * 3
        args += [scale, shift, w]
    return pl.pallas_call(
        functools.partial(_resid_kernel, project=project),
        grid=(t // tm,),
        in_specs=in_specs, out_specs=out_specs, out_shape=out_shape,
        compiler_params=_cparams(("arbitrary",)),
        name="resid",
    )(*args)


def _rope_tables(s):
    tok = jnp.arange(s)
    row = (tok // GRID_W).astype(F32)
    col = (tok % GRID_W).astype(F32)
    n_freq = MLA_ROPE // 4
    inv = ROPE_THETA ** (-jnp.arange(n_freq, dtype=F32) / n_freq)
    ang = jnp.concatenate([row[:, None] * inv, col[:, None] * inv], axis=-1)
    cos, sin = jnp.cos(ang), jnp.sin(ang)
    cos_t = jnp.concatenate([cos, cos, cos, cos], axis=-1)
    sin_t = jnp.concatenate([-sin, sin, -sin, sin], axis=-1)
    cos_t = jnp.concatenate([jnp.ones((CTX_LEN, LANES), F32), cos_t], axis=0)
    sin_t = jnp.concatenate([jnp.zeros((CTX_LEN, LANES), F32), sin_t], axis=0)
    return cos_t, sin_t


def _pad_cols(w, n):
    return jnp.pad(w, ((0, 0), (0, n - w.shape[1])))


def _swap_halves(w):
    half = w.shape[-1] // 2
    return jnp.concatenate([w[..., half:], w[..., :half]], axis=-1)


def _proj0_weights(in_w, wa_f, ba_f, wa_b, ba_b, wuq, wukv):
    o = np.cumsum((0, GLA_QK, GLA_QK, GLA_VW, GLA_VW, GLA_GATE_RANK, GLA_GATE_RANK, MLA_Q_LORA, MLA_KV_LORA,
                   MLA_ROPE))
    q, k, v, r, a_f, a_b, cq, ckv, kr = [in_w[:, o[i]:o[i + 1]] for i in range(9)]
    w_in = jnp.concatenate([q * (GLA_DK ** -0.5), k, v, r, cq, ckv, _pad_cols(kr, LANES),
                            _pad_cols(_swap_halves(kr), LANES),
                            _pad_cols(jnp.concatenate([a_f, a_b], axis=1), LANES)], axis=1).astype(BF16)
    wa = jnp.zeros((LANES, 2 * GLA_QK), F32)
    wa = wa.at[0:GLA_GATE_RANK, 0:GLA_QK].set(wa_f).at[GLA_GATE_RANK:2 * GLA_GATE_RANK, GLA_QK:].set(wa_b)
    ba = jnp.concatenate([ba_f, ba_b]).reshape(1, 2 * GLA_QK)
    wq = wuq.reshape(MLA_Q_LORA, MLA_HEADS, MLA_NOPE + MLA_ROPE)
    nope = wq[:, :, :MLA_NOPE].reshape(MLA_Q_LORA, MLA_HEADS * MLA_NOPE)
    rope = wq[:, :, MLA_NOPE:]
    pad3 = lambda z: jnp.pad(z, ((0, 0), (0, 0), (0, LANES - MLA_ROPE))).reshape(MLA_Q_LORA, MLA_HEADS * LANES)
    wuq_x = jnp.concatenate([nope, pad3(rope), pad3(_swap_halves(rope))], axis=1).astype(BF16)
    wkv = wukv.reshape(MLA_KV_LORA, MLA_HEADS, MLA_NOPE + MLA_V)
    wukv_x = jnp.concatenate([wkv[:, :, :MLA_NOPE].reshape(MLA_KV_LORA, -1),
                              wkv[:, :, MLA_NOPE:].reshape(MLA_KV_LORA, -1)], axis=1).astype(BF16)
    return w_in, wa.astype(BF16), ba, wuq_x, wukv_x


def _router_weights(rw, rb):
    rw = _pad_cols(rw, LANES)
    hi = rw.astype(BF16)
    lo = (rw - hi.astype(F32)).astype(BF16)
    return hi, lo, _pad_cols(rb.reshape(1, -1), LANES)


def kernel(x, c, ctx, c_ctx, ada_w, ada_b, norm_g, router_w, router_b, moe_w1, moe_b1, moe_w2, moe_b2, ab_in_w,
           gla_wa_f, gla_ba_f, gla_wa_b, gla_ba_b, gla_onorm, mla_qnorm, mla_wuq, mla_kvnorm, mla_wukv, ab_out_w,
           na_qkv_w, na_rpb, na_out_w):
    b, s, d = x.shape
    assert b == 1 and d == D_MODEL and ctx.shape == (1, CTX_LEN, d) and s % (NA_QROWS * GRID_W) == 0
    rows = s // GRID_W
    xs = jnp.concatenate([ctx[0], x[0]], axis=0)

    cc = jnp.zeros((8, d), F32).at[0].set(c[0]).at[1].set(c_ctx)
    mod = _ada(cc, ada_w, ada_b)

    def vectors(l):
        parts = jnp.split(mod[l, 0:2][::-1], 6, axis=-1)
        sh1, sc1, g1, sh2, sc2, g2 = [p.reshape(2, 1, d) for p in parts]
        return ((1 + sc1) * norm_g[l, 0], sh1, g1, (1 + sc2) * norm_g[l, 2], sh2, g2)

    sc1, sh1, g1, sc2, sh2, g2 = vectors(0)
    cos_t, sin_t = _rope_tables(s)
    w_in, wa, ba, wuq_x, wukv_x = _proj0_weights(ab_in_w[0], gla_wa_f[0], gla_ba_f[0], gla_wa_b[0], gla_ba_b[0],
                                                 mla_wuq[0], mla_wukv[0])
    qg, kg, vg, r, gf, gb, qm, km, vm = _proj0(xs, sc1, sh1, cos_t, sin_t, w_in, wa, ba,
                                               mla_qnorm[0].reshape(1, -1), wuq_x,
                                               mla_kvnorm[0].reshape(1, -1), wukv_x)
    o_f, o_b = _gla(qg, kg, vg, gf, gb)
    om_l = _flash(qm[:, CTX_LEN:], km, vm, FLASH_TQ, FLASH_TK)
    om_c = _flash(qm[:, :CTX_LEN], km[:, :CTX_LEN], vm[:, :CTX_LEN], CTX_LEN, CTX_LEN)
    om = jnp.concatenate([om_c, om_l], axis=0)
    rw_hi, rw_lo, rb = _router_weights(router_w[0], router_b[0])
    xs, h2, logits = _attn_out(xs, (o_f, o_b, r, om, gla_onorm[0].reshape(1, -1)), ab_out_w[0].astype(BF16),
                               norm_g[0, 1].reshape(1, -1), g1, sc2, sh2, rw_hi, rw_lo, rb, gla=True, first_tile=0)
    b1p, b2r = _moe_biases(moe_b1, moe_b2)
    y = _moe(h2, logits[:, :N_EXPERTS], moe_w1, b1p, moe_w2, b2r, 0)

    sc1n, sh1n, g1n, sc2n, sh2n, g2n = vectors(1)
    qkv_w = jnp.concatenate([na_qkv_w[0][:, :d] * (NA_DH ** -0.5), na_qkv_w[0][:, d:]], axis=1).astype(BF16)
    xs, q, k, v = _resid(xs, y, norm_g[0, 3].reshape(1, -1), g2, sc1n, sh1n, qkv_w)
    o_na = _na(q, k, v, _na_bias_tiles(na_rpb[0], rows), rows)
    rw_hi, rw_lo, rb = _router_weights(router_w[1], router_b[1])
    x_l, h2, logits = _attn_out(xs, (o_na,), na_out_w[0].astype(BF16), norm_g[1, 1].reshape(1, -1), g1n, sc2n, sh2n,
                                rw_hi, rw_lo, rb, gla=False, first_tile=1)
    y = _moe(h2, logits[:, :N_EXPERTS], moe_w1, b1p, moe_w2, b2r, 1)
    (x_l,) = _resid(x_l, y, norm_g[1, 3].reshape(1, -1), g2n, first_tile=1)
    return x_l.reshape(1, s, d)
```

```python
import functools

import numpy as np
import jax
import jax.numpy as jnp
from jax import lax
from jax.experimental import pallas as pl
from jax.experimental.pallas import tpu as pltpu

F32 = jnp.float32
BF16 = jnp.bfloat16

D_MODEL = 1024
DEPTH = 2
CTX_LEN = 256
GRID_W = 64
EPS = 1e-6
ROPE_THETA = 10000.0

GLA_HEADS = 4
GLA_DK = 64
GLA_DV = 128
GLA_GATE_RANK = 16
GLA_TAU = 16.0
GLA_QK = GLA_HEADS * GLA_DK
GLA_VW = GLA_HEADS * GLA_DV

MLA_HEADS = 4
MLA_Q_LORA = 256
MLA_KV_LORA = 128
MLA_NOPE = 128
MLA_ROPE = 64
MLA_V = 128
MLA_VW = MLA_HEADS * MLA_V
MLA_QK_PAD = 256

NA_HEADS = 16
NA_DH = D_MODEL // NA_HEADS
NA_KH = 8
NA_KW = 16
NA_QROWS = 4
NA_WROWS = NA_QROWS + NA_KH

N_EXPERTS = 32
TOP_K = 4
SWIGLU_LIMIT = 7.0
SWIGLU_ALPHA = 1.702

LANES = 128
ROW_TILE = 256
MOE_ROWS = 512
VMEM_LIMIT = 56 * 1024 * 1024
NEG_BIG = -1e30

GLA_CHUNK = 64
GLA_SUB = 16
FLASH_TQ = 1024
FLASH_TK = 3328
FLASH_SUB = 256


def _cparams(sem):
    return pltpu.CompilerParams(dimension_semantics=sem, vmem_limit_bytes=VMEM_LIMIT)


def _dot(a, b):
    return jnp.dot(a, b, preferred_element_type=F32)


def _dot_nt(a, b):
    return lax.dot_general(a, b, (((1,), (1,)), ((), ())), preferred_element_type=F32)


def _rms(x, g):
    return x * lax.rsqrt(jnp.mean(x * x, axis=-1, keepdims=True) + EPS) * g


def _stream(i):
    return jnp.where(i == 0, 0, 1)


def _ada_kernel(c_ref, w_ref, b_ref, o_ref):
    c = c_ref[...]
    s = (c * (1.0 / (1.0 + jnp.exp(-c)))).astype(BF16)
    o_ref[0] = _dot(s, w_ref[0].astype(BF16)) + b_ref[0]


def _ada(cc, ada_w, ada_b):
    depth, d, n = ada_w.shape
    tn = 1536
    return pl.pallas_call(
        _ada_kernel,
        grid=(depth, n // tn),
        in_specs=[pl.BlockSpec((8, d), lambda l, j: (0, 0)),
                  pl.BlockSpec((1, d, tn), lambda l, j: (l, 0, j)),
                  pl.BlockSpec((1, 1, tn), lambda l, j: (l, 0, j))],
        out_specs=pl.BlockSpec((1, 8, tn), lambda l, j: (l, 0, j)),
        out_shape=jax.ShapeDtypeStruct((depth, 8, n), F32),
        compiler_params=_cparams(("arbitrary", "arbitrary")),
        name="ada",
    )(cc, ada_w, ada_b.reshape(depth, 1, n))


def _proj0_kernel(x_ref, sc_ref, sh_ref, cos_ref, sin_ref, win_ref, wa_ref, ba_ref, qn_ref, wuq_ref,
                  kvn_ref, wukv_ref,
                  qg_ref, kg_ref, vg_ref, r_ref, gf_ref, gb_ref, qm_ref, km_ref, vm_ref, *, mla_scale):
    x = x_ref[...]
    h = x * lax.rsqrt(jnp.mean(x * x, axis=-1, keepdims=True) + EPS) * sc_ref[0] + sh_ref[0]
    u = _dot(h.astype(BF16), win_ref[...])
    qg_ref[...] = u[:, 0:256]
    kg_ref[...] = u[:, 256:512]
    vg_ref[...] = u[:, 512:1024].astype(BF16)
    r_ref[...] = u[:, 1024:1536]
    cq = u[:, 1536:1792]
    ckv = u[:, 1792:1920]
    kr = u[:, 1920:2048]
    krs = u[:, 2048:2176]
    a = u[:, 2176:2304]

    ga = _dot(a.astype(BF16), wa_ref[...]) + ba_ref[...]
    la = (jnp.minimum(ga, 0.0) - jnp.log1p(jnp.exp(-jnp.abs(ga)))) * (1.0 / GLA_TAU)
    gf_ref[...] = la[:, :GLA_QK]
    gb_ref[...] = la[:, GLA_QK:]

    cos = cos_ref[...]
    sin = sin_ref[...]
    qm = _dot(_rms(cq, qn_ref[...]).astype(BF16), wuq_ref[...])
    kvm = _dot(_rms(ckv, kvn_ref[...]).astype(BF16), wukv_ref[...])
    kroped = (kr * cos + krs * sin).astype(BF16)
    for hd in range(MLA_HEADS):
        nope = qm[:, hd * 128:(hd + 1) * 128]
        rp = qm[:, 512 + hd * 128:512 + (hd + 1) * 128]
        rps = qm[:, 1024 + hd * 128:1024 + (hd + 1) * 128]
        roped = rp * cos + rps * sin
        qm_ref[hd] = (jnp.concatenate([nope, roped], axis=1) * mla_scale).astype(BF16)
        km_ref[hd] = jnp.concatenate([kvm[:, hd * 128:(hd + 1) * 128].astype(BF16), kroped], axis=1)
        v_h = kvm[:, 512 + hd * 128:512 + (hd + 1) * 128].astype(BF16)
        vm_ref[hd] = jnp.concatenate([v_h, jnp.ones_like(v_h)], axis=1)


def _proj0(xs, scale, shift, cos_t, sin_t, w_in, wa, ba, qn, wuq, kvn, wukv):
    t, d = xs.shape
    tm = ROW_TILE
    row = lambda i: (i, 0)
    const = lambda i: (0, 0)
    strm = lambda i: (_stream(i), 0, 0)
    hrow = lambda i: (0, i, 0)
    kern = functools.partial(_proj0_kernel, mla_scale=float((MLA_NOPE + MLA_ROPE) ** -0.5 * np.log2(np.e)))
    return pl.pallas_call(
        kern,
        grid=(t // tm,),
        in_specs=[pl.BlockSpec((tm, d), row),
                  pl.BlockSpec((1, 1, d), strm), pl.BlockSpec((1, 1, d), strm),
                  pl.BlockSpec((tm, LANES), row), pl.BlockSpec((tm, LANES), row),
                  pl.BlockSpec(w_in.shape, const), pl.BlockSpec(wa.shape, const), pl.BlockSpec(ba.shape, const),
                  pl.BlockSpec(qn.shape, const), pl.BlockSpec(wuq.shape, const),
                  pl.BlockSpec(kvn.shape, const), pl.BlockSpec(wukv.shape, const)],
        out_specs=[pl.BlockSpec((tm, GLA_QK), row), pl.BlockSpec((tm, GLA_QK), row),
                   pl.BlockSpec((tm, GLA_VW), row), pl.BlockSpec((tm, GLA_VW), row),
                   pl.BlockSpec((tm, GLA_QK), row), pl.BlockSpec((tm, GLA_QK), row),
                   pl.BlockSpec((MLA_HEADS, tm, MLA_QK_PAD), hrow),
                   pl.BlockSpec((MLA_HEADS, tm, MLA_QK_PAD), hrow),
                   pl.BlockSpec((MLA_HEADS, tm, 2 * MLA_V), hrow)],
        out_shape=[jax.ShapeDtypeStruct((t, GLA_QK), F32), jax.ShapeDtypeStruct((t, GLA_QK), F32),
                   jax.ShapeDtypeStruct((t, GLA_VW), BF16), jax.ShapeDtypeStruct((t, GLA_VW), F32),
                   jax.ShapeDtypeStruct((t, GLA_QK), F32), jax.ShapeDtypeStruct((t, GLA_QK), F32),
                   jax.ShapeDtypeStruct((MLA_HEADS, t, MLA_QK_PAD), BF16),
                   jax.ShapeDtypeStruct((MLA_HEADS, t, MLA_QK_PAD), BF16),
                   jax.ShapeDtypeStruct((MLA_HEADS, t, 2 * MLA_V), BF16)],
        compiler_params=_cparams(("arbitrary",)),
        name="proj0",
    )(xs, scale, shift, cos_t, sin_t, w_in, wa, ba, qn, wuq, kvn, wukv)


def _gla_tables():
    c, s = GLA_CHUNK, GLA_SUB
    t = np.arange(c)[:, None]
    u = np.arange(c)[None, :]
    bt, bu = t // s, u // s
    half = (c // s) // 2
    mats = [(bu == bt) & (u <= t), (bu == bt) & (u > t), bu < bt, bu > bt,
            (bu >= half) & (bu < bt), (bu > bt) & (bu < half)]
    m = np.concatenate([x.astype(np.float32) for x in mats], axis=0)
    sel = np.stack([(bt == bu) & (u <= t), bt == bu + 1, (bt >= half) & (bu < half)]).astype(np.float32)
    m_b = np.concatenate([x.astype(np.float32)[::-1, ::-1] for x in mats], axis=0)
    sel_b = sel[:, ::-1, ::-1]
    return np.stack([m, m_b]), np.stack([sel, sel_b])


def _gla_chunk(q, k, g, v_ref, vt_ref, j, m, sel, st_ref, o_ref):
    rows = pl.ds(j * GLA_CHUNK, GLA_CHUNK)
    c = GLA_CHUNK
    g_hi = g.astype(BF16)
    g_r1 = g - g_hi.astype(F32)
    g_mid = g_r1.astype(BF16)
    g_lo = (g_r1 - g_mid.astype(F32)).astype(BF16)
    cs = _dot(m, g_hi) + _dot(m, g_mid) + _dot(m, g_lo)
    own, rest, before, after = cs[0:c], cs[c:2 * c], cs[2 * c:3 * c], cs[3 * c:4 * c]
    far_q, far_k = cs[4 * c:5 * c], cs[5 * c:6 * c]
    qd = q * jnp.exp(own)
    kd = k * jnp.exp(-own)
    kh = k * jnp.exp(rest)
    q_in = qd * jnp.exp(before)
    k_out = kh * jnp.exp(after)
    q_far = qd * jnp.exp(far_q)
    k_far = kh * jnp.exp(far_k)
    decay = jnp.exp(own + rest + before + after)
    lane = lax.broadcasted_iota(jnp.int32, (c, LANES), 1)
    s_diag, s_adj, s_far = sel[0] > 0.5, sel[1] > 0.5, sel[2] > 0.5
    for p in range(GLA_HEADS // 2):
        lo, hi = p * LANES, (p + 1) * LANES
        kd_p = kd[:, lo:hi].astype(BF16)
        kh_p = kh[:, lo:hi].astype(BF16)
        kf_p = k_far[:, lo:hi].astype(BF16)
        st = st_ref[p]
        st_b = st.astype(BF16)
        upd = st * decay[0:1, lo:hi]
        for half in range(2):
            hd = 2 * p + half
            msk = (lane < GLA_DK) if half == 0 else (lane >= GLA_DK)
            pick = lambda z: jnp.where(msk, z[:, lo:hi], 0.0).astype(BF16)
            qd_h = pick(qd)
            a = jnp.where(s_diag, _dot_nt(qd_h, kd_p),
                          jnp.where(s_adj, _dot_nt(qd_h, kh_p),
                                    jnp.where(s_far, _dot_nt(pick(q_far), kf_p), 0.0)))
            v_h = v_ref[rows, hd * GLA_DV:(hd + 1) * GLA_DV]
            o_ref[rows, hd * GLA_DV:(hd + 1) * GLA_DV] = _dot(a.astype(BF16), v_h) + _dot_nt(pick(q_in), st_b)
            upd = upd + _dot(vt_ref[j, hd * GLA_DV:(hd + 1) * GLA_DV, :], pick(k_out))
        st_ref[p] = upd


def _gla_kernel(qf_ref, kf_ref, gf_ref, vf_ref, vtf_ref, qb_ref, kb_ref, gb_ref, vb_ref, vtb_ref, m_ref, sel_ref,
                of_ref, ob_ref, stf_ref, stb_ref):
    @pl.when(pl.program_id(0) == 0)
    def _():
        stf_ref[...] = jnp.zeros_like(stf_ref)
        stb_ref[...] = jnp.zeros_like(stb_ref)

    n = ROW_TILE // GLA_CHUNK
    for j in range(n):
        rows = pl.ds(j * GLA_CHUNK, GLA_CHUNK)
        _gla_chunk(qf_ref[rows, :], kf_ref[rows, :], gf_ref[rows, :], vf_ref, vtf_ref, j,
                   m_ref[0], sel_ref[0], stf_ref, of_ref)
        jb = n - 1 - j
        rows = pl.ds(jb * GLA_CHUNK, GLA_CHUNK)
        _gla_chunk(qb_ref[rows, :], kb_ref[rows, :], gb_ref[rows, :], vb_ref, vtb_ref, jb,
                   m_ref[1], sel_ref[1], stb_ref, ob_ref)


def _gla(q, k, v, gf, gb):
    t = q.shape[0]
    tm = ROW_TILE
    nb = t // tm
    n = tm // GLA_CHUNK
    m_np, sel_np = _gla_tables()
    m = jnp.asarray(m_np, BF16)
    sel = jnp.asarray(sel_np, F32)
    vt = jnp.transpose(v.reshape(t // GLA_CHUNK, GLA_CHUNK, GLA_VW), (0, 2, 1))
    fwd = lambda i: (i, 0)
    bwd = lambda i: (jnp.where(i == 0, 0, nb - i), 0)
    fwd_t = lambda i: (i, 0, 0)
    bwd_t = lambda i: (jnp.where(i == 0, 0, nb - i), 0, 0)
    c3 = lambda i: (0, 0, 0)
    c4 = lambda i: (0, 0, 0, 0)
    spec = lambda w, im: pl.BlockSpec((tm, w), im)
    return pl.pallas_call(
        _gla_kernel,
        grid=(nb,),
        in_specs=[spec(GLA_QK, fwd), spec(GLA_QK, fwd), spec(GLA_QK, fwd), spec(GLA_VW, fwd),
                  pl.BlockSpec((n, GLA_VW, GLA_CHUNK), fwd_t),
                  spec(GLA_QK, bwd), spec(GLA_QK, bwd), spec(GLA_QK, bwd), spec(GLA_VW, bwd),
                  pl.BlockSpec((n, GLA_VW, GLA_CHUNK), bwd_t),
                  pl.BlockSpec(m.shape, c3), pl.BlockSpec(sel.shape, c4)],
        out_specs=[spec(GLA_VW, fwd), spec(GLA_VW, bwd)],
        out_shape=[jax.ShapeDtypeStruct((t, GLA_VW), F32), jax.ShapeDtypeStruct((t, GLA_VW), F32)],
        scratch_shapes=[pltpu.VMEM((GLA_HEADS // 2, GLA_DV, LANES), F32),
                        pltpu.VMEM((GLA_HEADS // 2, GLA_DV, LANES), F32)],
        compiler_params=_cparams(("arbitrary",)),
        name="gla",
    )(q, k, gf, v, vt, q, k, gb, v, vt, m, sel)


def _flash_kernel(q_ref, k_ref, v_ref, o_ref, s_ref, mc_ref, m_ref, acc_ref, *, tk, n_chunks):
    m_ref[...] = jnp.full_like(m_ref, -jnp.inf)
    acc_ref[...] = jnp.zeros_like(acc_ref)
    q = q_ref[0]
    tq = q.shape[0]
    sub = min(tk, FLASH_SUB)
    dv = o_ref.shape[1]

    def scores(j, slot):
        off = pl.multiple_of(j * tk, tk)
        s = _dot_nt(q, k_ref[0, pl.ds(off, tk), :])
        s_ref[slot] = s
        mc_ref[slot] = jnp.broadcast_to(jnp.max(s, axis=-1, keepdims=True), (tq, LANES))

    def softmax_pv(j, slot):
        off = pl.multiple_of(j * tk, tk)
        m_old = m_ref[...]
        m_new = jnp.maximum(m_old, mc_ref[slot])
        alpha = jnp.exp2(m_old - m_new)
        m_ref[...] = m_new
        acc = jnp.concatenate([alpha, alpha], axis=1) * acc_ref[...]
        for c in range(tk // sub):
            ps = [jnp.exp2(s_ref[slot, :, c * sub + a * LANES:c * sub + (a + 1) * LANES] - m_new).astype(BF16)
                  for a in range(sub // LANES)]
            acc = acc + _dot(jnp.concatenate(ps, axis=1), v_ref[0, pl.ds(off + c * sub, sub), :])
        acc_ref[...] = acc

    scores(0, 0)

    def pair(i, carry):
        scores(2 * i + 1, 1)
        softmax_pv(2 * i, 0)
        scores(2 * i + 2, 0)
        softmax_pv(2 * i + 1, 1)
        return carry

    n_pairs = (n_chunks - 1) // 2
    lax.fori_loop(0, n_pairs, pair, 0)
    if n_chunks % 2 == 0:
        scores(n_chunks - 1, 1)
        softmax_pv(n_chunks - 2, 0)
        softmax_pv(n_chunks - 1, 1)
    else:
        softmax_pv(n_chunks - 1, 0)
    o_ref[...] = (acc_ref[:, :dv] * (1.0 / acc_ref[:, dv:])).astype(o_ref.dtype)


def _flash(q, k, v, tq, tk):
    h, s, dq = q.shape
    t, dv = k.shape[1], v.shape[2] // 2
    assert s % tq == 0 and t % tk == 0
    kern = functools.partial(_flash_kernel, tk=tk, n_chunks=t // tk)
    return pl.pallas_call(
        kern,
        grid=(h, s // tq),
        in_specs=[pl.BlockSpec((1, tq, dq), lambda hh, i: (hh, i, 0)),
                  pl.BlockSpec((1, t, dq), lambda hh, i: (hh, 0, 0), pipeline_mode=pl.Buffered(1)),
                  pl.BlockSpec((1, t, 2 * dv), lambda hh, i: (hh, 0, 0), pipeline_mode=pl.Buffered(1))],
        out_specs=pl.BlockSpec((tq, dv), lambda hh, i: (i, hh)),
        out_shape=jax.ShapeDtypeStruct((s, h * dv), BF16),
        scratch_shapes=[pltpu.VMEM((2, tq, tk), F32), pltpu.VMEM((2, tq, LANES), F32),
                        pltpu.VMEM((tq, LANES), F32), pltpu.VMEM((tq, 2 * dv), F32)],
        compiler_params=_cparams(("arbitrary", "arbitrary")),
        name="flash",
    )(q, k, v)


def _na_bias_tiles(rpb, rows):
    w = GRID_W
    qc = np.arange(w)[:, None]
    kc = np.arange(w)[None, :]
    c0 = np.clip(qc - NA_KW // 2, 0, w - NA_KW)
    col_ok = (kc >= c0) & (kc < c0 + NA_KW)
    dc = np.clip(kc - qc + NA_KW - 1, 0, 2 * NA_KW - 2)
    onehot = (np.arange(2 * NA_KW - 1)[:, None, None] == dc[None]).astype(np.float32)
    e1 = jnp.einsum('hrd,dqk->hrqk', rpb.astype(F32), jnp.asarray(onehot), precision=lax.Precision.HIGHEST)
    e1 = jnp.where(jnp.asarray(col_ok)[None, None], e1, NEG_BIG)
    e1 = jnp.concatenate([e1, jnp.full((NA_HEADS, 1, w, w), NEG_BIG, F32)], axis=1)
    idx = np.zeros((3, NA_QROWS, NA_WROWS), np.int32)
    for vi, (r_first, ks) in enumerate([(0, 0), (NA_QROWS, 0), (rows - NA_QROWS, rows - NA_WROWS)]):
        for a in range(NA_QROWS):
            qr = r_first + a
            r0 = min(max(qr - NA_KH // 2, 0), rows - NA_KH)
            for b in range(NA_WROWS):
                kr = ks + b
                idx[vi, a, b] = kr - qr + NA_KH - 1 if r0 <= kr < r0 + NA_KH else 2 * NA_KH - 1
    tiles = jnp.take(e1, jnp.asarray(idx.reshape(-1)), axis=1)
    tiles = tiles.reshape(NA_HEADS, 3, NA_QROWS, NA_WROWS, w, w)
    return jnp.transpose(tiles, (1, 0, 2, 4, 3, 5)).reshape(3, NA_HEADS, NA_QROWS * w, NA_WROWS * w)


def _na_kernel(q_ref, kv_k_ref, kv_v_ref, bias_ref, o_ref, sw_ref, sc_ref, m_ref, *, rows, n_groups):
    g = pl.program_id(1)
    nq = NA_QROWS * GRID_W
    nw = NA_WROWS * GRID_W
    lane = lax.broadcasted_iota(jnp.int32, (nq, LANES), 1)

    def window(grp):
        ks = jnp.clip(grp * NA_QROWS - NA_KH // 2, 0, rows - NA_WROWS)
        return pl.ds(pl.multiple_of(CTX_LEN + ks * GRID_W, GRID_W), nw)

    @pl.when(g == 0)
    def _():
        sw_ref[1] = jnp.zeros(sw_ref.shape[1:], F32)
        sc_ref[1] = jnp.zeros(sc_ref.shape[1:], F32)
        m_ref[1] = jnp.zeros(m_ref.shape[1:], F32)

    def step(slot, prev):
        q = q_ref[...]
        k_w = kv_k_ref[window(jnp.minimum(g, n_groups - 1)), :]
        k_c = kv_k_ref[0:CTX_LEN, :]
        for half in range(2):
            msk = (lane < NA_DH) if half == 0 else (lane >= NA_DH)
            q_h = jnp.where(msk, q, jnp.zeros_like(q))
            s_w = _dot_nt(q_h, k_w) + bias_ref[0, half]
            s_c = _dot_nt(q_h, k_c)
            sw_ref[slot, half] = s_w
            sc_ref[slot, half] = s_c
            m = jnp.maximum(jnp.max(s_w, axis=-1, keepdims=True), jnp.max(s_c, axis=-1, keepdims=True))
            m_ref[slot, half] = jnp.broadcast_to(m, (nq, LANES))
        v_w = kv_v_ref[window(jnp.maximum(g - 1, 0)), :]
        v_c = kv_v_ref[0:CTX_LEN, :]
        outs = []
        for half in range(2):
            m = m_ref[prev, half]
            p_w = jnp.exp(sw_ref[prev, half] - jnp.concatenate([m] * (nw // LANES), axis=1))
            p_c = jnp.exp(sc_ref[prev, half] - jnp.concatenate([m] * (CTX_LEN // LANES), axis=1))
            l = jnp.sum(p_w, axis=-1, keepdims=True) + jnp.sum(p_c, axis=-1, keepdims=True)
            o = _dot(p_w.astype(BF16), v_w) + _dot(p_c.astype(BF16), v_c)
            outs.append(o * (1.0 / l))
        o_ref[...] = jnp.where(lane < NA_DH, outs[0], outs[1]).astype(o_ref.dtype)

    @pl.when(g % 2 == 0)
    def _():
        step(0, 1)

    @pl.when(g % 2 == 1)
    def _():
        step(1, 0)


def _na(q, k, v, bias, rows):
    t, d = q.shape
    s = t - CTX_LEN
    nq = NA_QROWS * GRID_W
    nw = NA_WROWS * GRID_W
    assert nq == ROW_TILE and rows >= NA_WROWS and rows % NA_QROWS == 0
    n_groups = rows // NA_QROWS
    cur = lambda g: jnp.minimum(g, n_groups - 1)
    variant = lambda g: jnp.where(g == 0, 0, jnp.where(g == n_groups - 1, 2, 1))
    return pl.pallas_call(
        functools.partial(_na_kernel, rows=rows, n_groups=n_groups),
        grid=(NA_HEADS // 2, n_groups + 1),
        in_specs=[pl.BlockSpec((nq, LANES), lambda p, g: (cur(g) + 1, p)),
                  pl.BlockSpec((t, LANES), lambda p, g: (0, p)),
                  pl.BlockSpec((t, LANES), lambda p, g: (0, p)),
                  pl.BlockSpec((1, 2, nq, nw), lambda p, g: (variant(cur(g)), p, 0, 0))],
        out_specs=pl.BlockSpec((nq, LANES), lambda p, g: (jnp.maximum(g - 1, 0), p)),
        out_shape=jax.ShapeDtypeStruct((s, d), BF16),
        scratch_shapes=[pltpu.VMEM((2, 2, nq, nw), F32), pltpu.VMEM((2, 2, nq, CTX_LEN), F32),
                        pltpu.VMEM((2, 2, nq, LANES), F32)],
        compiler_params=_cparams(("arbitrary", "arbitrary")),
        name="na",
    )(q, k, v, bias)


def _attn_out_kernel(*refs, gla):
    if gla:
        (x_ref, of_ref, ob_ref, r_ref, om_ref, on_ref, w_ref, g1_ref, gate_ref, sc_ref, sh_ref,
         rwh_ref, rwl_ref, rb_ref, xo_ref, h_ref, lg_ref) = refs
        o = of_ref[...] + ob_ref[...]
        r = r_ref[...]
        parts = []
        for hd in range(GLA_HEADS):
            sl = slice(hd * GLA_DV, (hd + 1) * GLA_DV)
            rh = r[:, sl]
            parts.append(_rms(o[:, sl], on_ref[...]) * (rh * (1.0 / (1.0 + jnp.exp(-rh)))))
        mix_a = jnp.concatenate(parts, axis=1).astype(BF16)
        y = _dot(mix_a, w_ref[0:GLA_VW, :]) + _dot(om_ref[...], w_ref[GLA_VW:, :])
    else:
        (x_ref, om_ref, w_ref, g1_ref, gate_ref, sc_ref, sh_ref,
         rwh_ref, rwl_ref, rb_ref, xo_ref, h_ref, lg_ref) = refs
        y = _dot(om_ref[...], w_ref[...])
    x = x_ref[...] + _rms(y, g1_ref[...]) * gate_ref[0]
    xo_ref[...] = x
    h = x * lax.rsqrt(jnp.mean(x * x, axis=-1, keepdims=True) + EPS) * sc_ref[0] + sh_ref[0]
    h_ref[...] = h
    h_hi = h.astype(BF16)
    h_lo = (h - h_hi.astype(F32)).astype(BF16)
    lg_ref[...] = _dot(h_hi, rwh_ref[...]) + _dot(h_lo, rwh_ref[...]) + _dot(h_hi, rwl_ref[...]) + rb_ref[...]


def _attn_out(xs, mix_inputs, w_out, g1, gate, scale, shift, rw_hi, rw_lo, rb, *, gla, first_tile):
    d = xs.shape[1]
    tm = ROW_TILE
    t = mix_inputs[0].shape[0]
    row = lambda i: (i, 0)
    xrow = lambda i: (i + first_tile, 0)
    const = lambda i: (0, 0)
    strm = lambda i: (_stream(i + first_tile), 0, 0)
    if gla:
        of, ob, r, om, onorm = mix_inputs
        mix_specs = [pl.BlockSpec((tm, GLA_VW), row), pl.BlockSpec((tm, GLA_VW), row), pl.BlockSpec((tm, GLA_VW), row),
                     pl.BlockSpec((tm, MLA_VW), row), pl.BlockSpec(onorm.shape, const)]
    else:
        mix_specs = [pl.BlockSpec((tm, d), row)]
    return pl.pallas_call(
        functools.partial(_attn_out_kernel, gla=gla),
        grid=(t // tm,),
        in_specs=[pl.BlockSpec((tm, d), xrow)] + mix_specs + [
            pl.BlockSpec(w_out.shape, const), pl.BlockSpec(g1.shape, const),
            pl.BlockSpec((1, 1, d), strm), pl.BlockSpec((1, 1, d), strm), pl.BlockSpec((1, 1, d), strm),
            pl.BlockSpec(rw_hi.shape, const), pl.BlockSpec(rw_lo.shape, const), pl.BlockSpec(rb.shape, const)],
        out_specs=[pl.BlockSpec((tm, d), row), pl.BlockSpec((tm, d), row), pl.BlockSpec((tm, LANES), row)],
        out_shape=[jax.ShapeDtypeStruct((t, d), F32), jax.ShapeDtypeStruct((t, d), F32),
                   jax.ShapeDtypeStruct((t, LANES), F32)],
        compiler_params=_cparams(("arbitrary",)),
        name="attn_out",
    )(xs, *mix_inputs, w_out, g1, gate, scale, shift, rw_hi, rw_lo, rb)


def _moe_kernel(be_ref, nv_ref, x_ref, gate_ref, w1_ref, p_ref, b1_ref, w2_ref, b2_ref, o_ref, w1b_ref, w2b_ref):
    i = pl.program_id(0)
    f = w2_ref.shape[2]

    @pl.when((i == 0) | (be_ref[i] != be_ref[jnp.maximum(i - 1, 0)]))
    def _():
        for c in range(2 * f // (2 * LANES)):
            cols = slice(c * 2 * LANES, (c + 1) * 2 * LANES)
            w1b_ref[:, cols] = _dot(w1_ref[0, 0, :, cols].astype(BF16), p_ref[...]).astype(BF16)
        w2b_ref[...] = w2_ref[0, 0].astype(BF16)

    @pl.when(i < nv_ref[0])
    def _():
        x = x_ref[...].astype(BF16)
        acts = []
        for c in range(f // LANES):
            u = _dot(x, w1b_ref[:, 2 * c * LANES:(2 * c + 2) * LANES]) \
                + b1_ref[0, 0, :, 2 * c * LANES:(2 * c + 2) * LANES]
            glu = jnp.minimum(u[:, :LANES], SWIGLU_LIMIT)
            lin = jnp.clip(u[:, LANES:], -SWIGLU_LIMIT, SWIGLU_LIMIT)
            acts.append((glu * (1.0 / (1.0 + jnp.exp(-SWIGLU_ALPHA * glu))) * (lin + 1.0)).astype(BF16))
        y = _dot(jnp.concatenate(acts, axis=1), w2b_ref[...]) + b2_ref[0, 0]
        o_ref[...] = (y * gate_ref[...]).astype(o_ref.dtype)

    @pl.when(i >= nv_ref[0])
    def _():
        o_ref[...] = jnp.zeros_like(o_ref)


def _regroup_matrix():
    grp = 2 * LANES
    src = np.concatenate([np.arange(0, grp, 2), np.arange(1, grp, 2)])
    perm = np.zeros((grp, grp), np.float32)
    perm[src, np.arange(grp)] = 1.0
    return jnp.asarray(perm, BF16)


def _moe_experts(block_e, n_valid, xb, slot_gate, w1, b1, w2, b2, layer):
    n_slots, d = xb.shape
    bm = MOE_ROWS
    f = w2.shape[2]
    perm = _regroup_matrix()
    grid_spec = pltpu.PrefetchScalarGridSpec(
        num_scalar_prefetch=2,
        grid=(n_slots // bm,),
        in_specs=[pl.BlockSpec((bm, d), lambda i, be, nv: (i, 0)),
                  pl.BlockSpec((bm, 1), lambda i, be, nv: (i, 0)),
                  pl.BlockSpec((1, 1, d, 2 * f), lambda i, be, nv: (layer, be[i], 0, 0)),
                  pl.BlockSpec(perm.shape, lambda i, be, nv: (0, 0)),
                  pl.BlockSpec((1, 1, 1, 2 * f), lambda i, be, nv: (layer, be[i], 0, 0)),
                  pl.BlockSpec((1, 1, f, d), lambda i, be, nv: (layer, be[i], 0, 0)),
                  pl.BlockSpec((1, 1, 1, d), lambda i, be, nv: (layer, be[i], 0, 0))],
        out_specs=pl.BlockSpec((bm, d), lambda i, be, nv: (i, 0)),
        scratch_shapes=[pltpu.VMEM((d, 2 * f), BF16), pltpu.VMEM((f, d), BF16)],
    )
    return pl.pallas_call(
        _moe_kernel,
        grid_spec=grid_spec,
        out_shape=jax.ShapeDtypeStruct((n_slots, d), BF16),
        compiler_params=_cparams(("arbitrary",)),
        name="moe_experts",
    )(block_e, n_valid, xb, slot_gate, w1, perm, b1, w2, b2)


def _moe(hp, logits, w1, b1, w2, b2, layer):
    t = hp.shape[0]
    bm = MOE_ROWS
    top_val, top_idx = lax.top_k(logits, TOP_K)
    gates = jax.nn.softmax(top_val, axis=-1)
    n_assign = t * TOP_K
    n_blocks = -(-n_assign // bm) + N_EXPERTS
    experts = jnp.arange(N_EXPERTS, dtype=jnp.int32)
    hit = jnp.any(top_idx[:, :, None] == experts, axis=1).astype(jnp.int32)
    counts = jnp.sum(hit, axis=0)
    before = jnp.cumsum(hit, axis=0) - hit
    padded = ((counts + bm - 1) // bm) * bm
    pad_end = jnp.cumsum(padded)
    pad_start = pad_end - padded
    start = jnp.cumsum(counts) - counts
    slot_of = jnp.take_along_axis(before + pad_start[None, :], top_idx, axis=1)
    block_first = jnp.arange(n_blocks, dtype=jnp.int32) * bm
    block_e = jnp.minimum(jnp.sum(block_first[:, None] >= pad_end[None, :], axis=1), N_EXPERTS - 1).astype(jnp.int32)
    order = jnp.argsort(top_idx.reshape(-1))
    local = (block_first - pad_start[block_e])[:, None] + jnp.arange(bm, dtype=jnp.int32)[None, :]
    valid = local < counts[block_e][:, None]
    assign = order[jnp.clip(start[block_e][:, None] + local, 0, n_assign - 1)]
    slot_tok = jnp.where(valid, assign // TOP_K, 0).astype(jnp.int32)
    slot_gate = jnp.where(valid, gates.reshape(-1)[assign], 0.0)
    n_valid = (pad_end[-1] // bm).astype(jnp.int32).reshape(1)
    yb = _moe_experts(block_e, n_valid, hp[slot_tok.reshape(-1)], slot_gate.reshape(n_blocks * bm, 1),
                      w1, b1, w2, b2, layer)
    return yb[slot_of.T.reshape(-1)].reshape(TOP_K, t, yb.shape[1])


def _moe_biases(moe_b1, moe_b2):
    depth, e, f2 = moe_b1.shape
    b1p = jnp.transpose(moe_b1.reshape(depth, e, f2 // (2 * LANES), LANES, 2), (0, 1, 2, 4, 3))
    return b1p.reshape(depth, e, 1, f2), moe_b2.reshape(depth, e, 1, moe_b2.shape[2])


def _resid_kernel(*refs, project):
    if project:
        x_ref, y_ref, g3_ref, gate_ref, sc_ref, sh_ref, w_ref, xo_ref, q_ref, k_ref, v_ref = refs
    else:
        x_ref, y_ref, g3_ref, gate_ref, xo_ref = refs
    y = y_ref[0].astype(F32)
    for j in range(1, TOP_K):
        y = y + y_ref[j].astype(F32)
    x = x_ref[...] + _rms(y, g3_ref[...]) * gate_ref[0]
    xo_ref[...] = x
    if project:
        d = x.shape[1]
        h = x * lax.rsqrt(jnp.mean(x * x, axis=-1, keepdims=True) + EPS) * sc_ref[0] + sh_ref[0]
        u = _dot(h.astype(BF16), w_ref[...])
        q_ref[...] = u[:, 0:d].astype(BF16)
        k_ref[...] = u[:, d:2 * d].astype(BF16)
        v_ref[...] = u[:, 2 * d:3 * d].astype(BF16)


def _resid(xs, y, g3, gate, scale=None, shift=None, w=None, *, first_tile=0):
    t, d = xs.shape
    tm = ROW_TILE
    project = w is not None
    row = lambda i: (i, 0)
    const = lambda i: (0, 0)
    strm = lambda i: (_stream(i + first_tile), 0, 0)
    vec = pl.BlockSpec((1, 1, d), strm)
    in_specs = [pl.BlockSpec((tm, d), row), pl.BlockSpec((TOP_K, tm, d), lambda i: (0, i, 0)),
                pl.BlockSpec(g3.shape, const), vec]
    out_specs = [pl.BlockSpec((tm, d), row)]
    out_shape = [jax.ShapeDtypeStruct((t, d), F32)]
    args = [xs, y, g3, gate]
    if project:
        in_specs += [vec, vec, pl.BlockSpec(w.shape, const)]
        out_specs += [pl.BlockSpec((tm, d), row)] * 3
        out_shape += [jax.ShapeDtypeStruct((t, d), BF16)] * 3
        args += [scale, shift, w]
    return pl.pallas_call(
        functools.partial(_resid_kernel, project=project),
        grid=(t // tm,),
        in_specs=in_specs, out_specs=out_specs, out_shape=out_shape,
        compiler_params=_cparams(("arbitrary",)),
        name="resid",
    )(*args)


def _rope_tables(s):
    tok = jnp.arange(s)
    row = (tok // GRID_W).astype(F32)
    col = (tok % GRID_W).astype(F32)
    n_freq = MLA_ROPE // 4
    inv = ROPE_THETA ** (-jnp.arange(n_freq, dtype=F32) / n_freq)
    ang = jnp.concatenate([row[:, None] * inv, col[:, None] * inv], axis=-1)
    cos, sin = jnp.cos(ang), jnp.sin(ang)
    cos_t = jnp.concatenate([cos, cos, cos, cos], axis=-1)
    sin_t = jnp.concatenate([-sin, sin, -sin, sin], axis=-1)
    cos_t = jnp.concatenate([jnp.ones((CTX_LEN, LANES), F32), cos_t], axis=0)
    sin_t = jnp.concatenate([jnp.zeros((CTX_LEN, LANES), F32), sin_t], axis=0)
    return cos_t, sin_t


def _pad_cols(w, n):
    return jnp.pad(w, ((0, 0), (0, n - w.shape[1])))


def _swap_halves(w):
    half = w.shape[-1] // 2
    return jnp.concatenate([w[..., half:], w[..., :half]], axis=-1)


def _proj0_weights(in_w, wa_f, ba_f, wa_b, ba_b, wuq, wukv):
    o = np.cumsum((0, GLA_QK, GLA_QK, GLA_VW, GLA_VW, GLA_GATE_RANK, GLA_GATE_RANK, MLA_Q_LORA, MLA_KV_LORA,
                   MLA_ROPE))
    q, k, v, r, a_f, a_b, cq, ckv, kr = [in_w[:, o[i]:o[i + 1]] for i in range(9)]
    w_in = jnp.concatenate([q * (GLA_DK ** -0.5), k, v, r, cq, ckv, _pad_cols(kr, LANES),
                            _pad_cols(_swap_halves(kr), LANES),
                            _pad_cols(jnp.concatenate([a_f, a_b], axis=1), LANES)], axis=1).astype(BF16)
    wa = jnp.zeros((LANES, 2 * GLA_QK), F32)
    wa = wa.at[0:GLA_GATE_RANK, 0:GLA_QK].set(wa_f).at[GLA_GATE_RANK:2 * GLA_GATE_RANK, GLA_QK:].set(wa_b)
    ba = jnp.concatenate([ba_f, ba_b]).reshape(1, 2 * GLA_QK)
    wq = wuq.reshape(MLA_Q_LORA, MLA_HEADS, MLA_NOPE + MLA_ROPE)
    nope = wq[:, :, :MLA_NOPE].reshape(MLA_Q_LORA, MLA_HEADS * MLA_NOPE)
    rope = wq[:, :, MLA_NOPE:]
    pad3 = lambda z: jnp.pad(z, ((0, 0), (0, 0), (0, LANES - MLA_ROPE))).reshape(MLA_Q_LORA, MLA_HEADS * LANES)
    wuq_x = jnp.concatenate([nope, pad3(rope), pad3(_swap_halves(rope))], axis=1).astype(BF16)
    wkv = wukv.reshape(MLA_KV_LORA, MLA_HEADS, MLA_NOPE + MLA_V)
    wukv_x = jnp.concatenate([wkv[:, :, :MLA_NOPE].reshape(MLA_KV_LORA, -1),
                              wkv[:, :, MLA_NOPE:].reshape(MLA_KV_LORA, -1)], axis=1).astype(BF16)
    return w_in, wa.astype(BF16), ba, wuq_x, wukv_x


def _router_weights(rw, rb):
    rw = _pad_cols(rw, LANES)
    hi = rw.astype(BF16)
    lo = (rw - hi.astype(F32)).astype(BF16)
    return hi, lo, _pad_cols(rb.reshape(1, -1), LANES)


def kernel(x, c, ctx, c_ctx, ada_w, ada_b, norm_g, router_w, router_b, moe_w1, moe_b1, moe_w2, moe_b2, ab_in_w,
           gla_wa_f, gla_ba_f, gla_wa_b, gla_ba_b, gla_onorm, mla_qnorm, mla_wuq, mla_kvnorm, mla_wukv, ab_out_w,
           na_qkv_w, na_rpb, na_out_w):
    b, s, d = x.shape
    assert b == 1 and d == D_MODEL and ctx.shape == (1, CTX_LEN, d) and s % (NA_QROWS * GRID_W) == 0
    rows = s // GRID_W
    xs = jnp.concatenate([ctx[0], x[0]], axis=0)

    cc = jnp.zeros((8, d), F32).at[0].set(c[0]).at[1].set(c_ctx)
    mod = _ada(cc, ada_w, ada_b)

    def vectors(l):
        parts = jnp.split(mod[l, 0:2][::-1], 6, axis=-1)
        sh1, sc1, g1, sh2, sc2, g2 = [p.reshape(2, 1, d) for p in parts]
        return ((1 + sc1) * norm_g[l, 0], sh1, g1, (1 + sc2) * norm_g[l, 2], sh2, g2)

    sc1, sh1, g1, sc2, sh2, g2 = vectors(0)
    cos_t, sin_t = _rope_tables(s)
    w_in, wa, ba, wuq_x, wukv_x = _proj0_weights(ab_in_w[0], gla_wa_f[0], gla_ba_f[0], gla_wa_b[0], gla_ba_b[0],
                                                 mla_wuq[0], mla_wukv[0])
    qg, kg, vg, r, gf, gb, qm, km, vm = _proj0(xs, sc1, sh1, cos_t, sin_t, w_in, wa, ba,
                                               mla_qnorm[0].reshape(1, -1), wuq_x,
                                               mla_kvnorm[0].reshape(1, -1), wukv_x)
    o_f, o_b = _gla(qg, kg, vg, gf, gb)
    om_l = _flash(qm[:, CTX_LEN:], km, vm, FLASH_TQ, FLASH_TK)
    om_c = _flash(qm[:, :CTX_LEN], km[:, :CTX_LEN], vm[:, :CTX_LEN], CTX_LEN, CTX_LEN)
    om = jnp.concatenate([om_c, om_l], axis=0)
    rw_hi, rw_lo, rb = _router_weights(router_w[0], router_b[0])
    xs, h2, logits = _attn_out(xs, (o_f, o_b, r, om, gla_onorm[0].reshape(1, -1)), ab_out_w[0].astype(BF16),
                               norm_g[0, 1].reshape(1, -1), g1, sc2, sh2, rw_hi, rw_lo, rb, gla=True, first_tile=0)
    b1p, b2r = _moe_biases(moe_b1, moe_b2)
    y = _moe(h2, logits[:, :N_EXPERTS], moe_w1, b1p, moe_w2, b2r, 0)

    sc1n, sh1n, g1n, sc2n, sh2n, g2n = vectors(1)
    qkv_w = jnp.concatenate([na_qkv_w[0][:, :d] * (NA_DH ** -0.5), na_qkv_w[0][:, d:]], axis=1).astype(BF16)
    xs, q, k, v = _resid(xs, y, norm_g[0, 3].reshape(1, -1), g2, sc1n, sh1n, qkv_w)
    o_na = _na(q, k, v, _na_bias_tiles(na_rpb[0], rows), rows)
    rw_hi, rw_lo, rb = _router_weights(router_w[1], router_b[1])
    x_l, h2, logits = _attn_out(xs, (o_na,), na_out_w[0].astype(BF16), norm_g[1, 1].reshape(1, -1), g1n, sc2n, sh2n,
                                rw_hi, rw_lo, rb, gla=False, first_tile=1)
    y = _moe(h2, logits[:, :N_EXPERTS], moe_w1, b1p, moe_w2, b2r, 1)
    (x_l,) = _resid(x_l, y, norm_g[1, 3].reshape(1, -1), g2n, first_tile=1)
    return x_l.reshape(1, s, d)
```
